```python
import jax, jax.numpy as jnp
from jax import lax
import numpy as np

D_MODEL = 2048
BATCH = 4
SEQ = 2048
DEPTH = 1
DEC_BATCH = 128
DEC_SEQ = 8
PAST_LEN = 16384
PAGE_SIZE = 128

LRU_WIDTH = D_MODEL // 2
LRU_BLOCKS = 8
LRU_BLOCK_W = LRU_WIDTH // LRU_BLOCKS
LRU_C = 8.0
CONV_W = 4
GDN_HEADS = 8
GDN_DK = 128
GDN_DV = 128
GDN_KDIM = GDN_HEADS * GDN_DK
GDN_VDIM = GDN_HEADS * GDN_DV
GDN_CONV_DIM = 2 * GDN_KDIM + GDN_VDIM
GDN_CHUNK = 64
MEM_LEN = 256
XA_HEADS = 4
XA_HEAD_DIM = 128
XA_DIM = XA_HEADS * XA_HEAD_DIM
N_EXPERTS = 32
TOP_K = 4
D_FF = D_MODEL
SWIGLU_ALPHA = 1.702
SWIGLU_LIMIT = 7.0
MOE_BLOCK = 128
NORM_EPS = 1e-6
IN_SIZES = (LRU_WIDTH, LRU_WIDTH, GDN_CONV_DIM, GDN_VDIM, GDN_HEADS, GDN_HEADS, D_MODEL, D_MODEL)
IN_COLS = sum(IN_SIZES)

kernel_name = 'hybrid_rglru_gdn_memxattn_moe_step'


def _split(x, sizes):
    offs = np.cumsum(np.array(sizes))[:-1].tolist()
    return jnp.split(x, offs, axis=-1)


def rmsnorm(x, g):
    xf = x.astype(jnp.float32)
    r = lax.rsqrt(jnp.mean(xf * xf, axis=-1, keepdims=True) + NORM_EPS)
    return (xf * r * g.astype(jnp.float32)).astype(x.dtype)


def l2norm(x):
    xf = x.astype(jnp.float32)
    return xf * lax.rsqrt(jnp.sum(xf * xf, axis=-1, keepdims=True) + NORM_EPS)


def causal_conv(x, buf, w):
    T = x.shape[1]
    xp = jnp.concatenate([buf.astype(x.dtype), x], axis=1)
    y = xp[:, 0:T] * w[0]
    for j in range(1, CONV_W):
        y = y + xp[:, j:j + T] * w[j]
    return y, xp[:, T:]


def _lru_combine(e1, e2):
    a1, b1 = e1
    a2, b2 = e2
    return a1 * a2, a2 * b1 + b2


def rg_lru(x, h0, wx, bx, wa, ba, a_param, pos0):
    B, T, _ = x.shape
    xf = x.astype(jnp.float32)
    xb = xf.reshape(B, T, LRU_BLOCKS, LRU_BLOCK_W)
    gate_x = jax.nn.sigmoid(jnp.einsum('btni,nij->btnj', xb, wx).reshape(B, T, LRU_WIDTH) + bx)
    gate_a = jax.nn.sigmoid(jnp.einsum('btni,nij->btnj', xb, wa).reshape(B, T, LRU_WIDTH) + ba)
    log_a = -LRU_C * gate_a * jax.nn.softplus(a_param.astype(jnp.float32))
    a = jnp.exp(log_a)
    mult = jnp.sqrt(-jnp.expm1(2.0 * log_a))
    pos = pos0 + jnp.arange(T)
    mult = jnp.where((pos == 0)[None, :, None], 1.0, mult)
    b = xf * gate_x * mult
    b = b.at[:, 0].add(a[:, 0] * h0.astype(jnp.float32))
    _, h = lax.associative_scan(_lru_combine, (a, b), axis=1)
    return h.astype(x.dtype), h[:, -1]


def gated_delta_rule(q, k, v, g, beta, S0):
    B, T, H, DK = q.shape
    DV = v.shape[-1]
    C = min(GDN_CHUNK, T)
    n = -(-T // C)
    pad = n * C - T

    def chunks(a):
        a = jnp.pad(a.astype(jnp.float32), [(0, 0), (0, pad)] + [(0, 0)] * (a.ndim - 2))
        a = a.reshape((B, n, C) + a.shape[2:])
        return jnp.moveaxis(a, (1, 3), (0, 2))

    qc = chunks(q) * (DK ** -0.5)
    kc = chunks(k)
    vc = chunks(v)
    bc = chunks(beta)
    gcum = jnp.cumsum(chunks(g), axis=-1)
    idx = jnp.arange(C)
    incl = idx[:, None] >= idx[None, :]
    strict = idx[:, None] > idx[None, :]
    decay = jnp.exp(jnp.where(incl, gcum[..., :, None] - gcum[..., None, :], -jnp.inf))
    kb = kc * bc[..., None]
    A = jnp.where(strict, jnp.einsum('nbhik,nbhjk->nbhij', kb, kc) * decay, 0.0)
    rhs = jnp.concatenate([vc * bc[..., None], kb * jnp.exp(gcum)[..., None]], axis=-1)
    sol = lax.linalg.triangular_solve(A + jnp.eye(C, dtype=jnp.float32), rhs,
                                      left_side=True, lower=True, unit_diagonal=True)
    u, w = sol[..., :DV], sol[..., DV:]
    qk = jnp.einsum('nbhik,nbhjk->nbhij', qc, kc) * decay
    q_dec = qc * jnp.exp(gcum)[..., None]
    k_dec = kc * jnp.exp(gcum[..., -1:] - gcum)[..., None]
    g_last = jnp.exp(gcum[..., -1])

    def step(S, inp):
        u_i, w_i, qk_i, qd_i, kd_i, gl_i = inp
        v_new = u_i - jnp.einsum('bhck,bhkv->bhcv', w_i, S)
        o = jnp.einsum('bhck,bhkv->bhcv', qd_i, S) + jnp.einsum('bhij,bhjv->bhiv', qk_i, v_new)
        S = S * gl_i[..., None, None] + jnp.einsum('bhck,bhcv->bhkv', kd_i, v_new)
        return S, o

    S, o = lax.scan(step, S0.astype(jnp.float32), (u, w, qk, q_dec, k_dec, g_last))
    o = jnp.moveaxis(o, (0, 2), (1, 3)).reshape(B, n * C, H, DV)[:, :T]
    return o, S


def hybrid_mixer(h, lru_conv, lru_h, gdn_conv, gdn_S, pos0, p):
    B, T, _ = h.shape
    proj = h @ p['w_in']
    xa, ya, qkv, z, b_raw, a_raw, g_a, g_b = _split(proj, IN_SIZES)
    xc, lru_conv_new = causal_conv(xa, lru_conv, p['lru_conv_w'])
    xc = xc + p['lru_conv_b']
    hs, lru_h_new = rg_lru(xc, lru_h, p['lru_wx'], p['lru_bx'], p['lru_wa'], p['lru_ba'], p['lru_a_param'], pos0)
    y_a = hs * jax.nn.gelu(ya)
    qkv_c, gdn_conv_new = causal_conv(qkv, gdn_conv, p['gdn_conv_w'])
    qkv_c = jax.nn.silu(qkv_c)
    q, k, v = _split(qkv_c, (GDN_KDIM, GDN_KDIM, GDN_VDIM))
    q = l2norm(q.reshape(B, T, GDN_HEADS, GDN_DK))
    k = l2norm(k.reshape(B, T, GDN_HEADS, GDN_DK))
    v = v.reshape(B, T, GDN_HEADS, GDN_DV)
    beta = jax.nn.sigmoid(b_raw.astype(jnp.float32))
    g = -jnp.exp(p['gdn_A_log'].astype(jnp.float32)) * jax.nn.softplus(
        a_raw.astype(jnp.float32) + p['gdn_dt_bias'].astype(jnp.float32))
    o, gdn_S_new = gated_delta_rule(q, k, v, g, beta, gdn_S)
    o = rmsnorm(o, p['gdn_norm_g']) * jax.nn.silu(z.reshape(B, T, GDN_HEADS, GDN_DV).astype(jnp.float32))
    y_b = o.reshape(B, T, GDN_VDIM).astype(h.dtype)
    u = jax.nn.sigmoid(g_a) * (y_a @ p['w_branch_a']) + jax.nn.sigmoid(g_b) * (y_b @ p['w_branch_b'])
    return u @ p['w_out'], (lru_conv_new, lru_h_new, gdn_conv_new, gdn_S_new)


def memory_kv(mem, g_mem, wk, wv):
    B = mem.shape[0]
    m = rmsnorm(mem, g_mem)
    k = (m @ wk).reshape(B, -1, XA_HEADS, XA_HEAD_DIM)
    v = (m @ wv).reshape(B, -1, XA_HEADS, XA_HEAD_DIM)
    return k, v


def cross_attention(h, k_mem, v_mem, wq, wo):
    B, T, _ = h.shape
    q = (h @ wq).reshape(B, T, XA_HEADS, XA_HEAD_DIM).astype(jnp.float32)
    s = jnp.einsum('bthd,bmhd->bhtm', q, k_mem.astype(jnp.float32)) * (XA_HEAD_DIM ** -0.5)
    prob = jax.nn.softmax(s, axis=-1)
    o = jnp.einsum('bhtm,bmhd->bthd', prob, v_mem.astype(jnp.float32)).reshape(B, T, XA_DIM)
    return o.astype(h.dtype) @ wo


def clamped_swiglu(gate, up):
    gate = jnp.minimum(gate, SWIGLU_LIMIT)
    up = jnp.clip(up, -SWIGLU_LIMIT, SWIGLU_LIMIT)
    return gate * jax.nn.sigmoid(SWIGLU_ALPHA * gate) * (up + 1.0)


def moe_ffn(h, p):
    B, T, D = h.shape
    N = B * T
    M = N * TOP_K
    xf = h.reshape(N, D)
    logits = (xf @ p['router_w'] + p['router_b']).astype(jnp.float32)
    top_val, top_idx = lax.top_k(logits, TOP_K)
    gates = jax.nn.softmax(top_val, axis=-1)
    e_flat = top_idx.reshape(M).astype(jnp.int32)
    tok_flat = jnp.arange(M, dtype=jnp.int32) // TOP_K
    g_flat = gates.reshape(M)
    order = jnp.argsort(e_flat)
    e_sorted = e_flat[order]
    counts = jnp.zeros((N_EXPERTS,), jnp.int32).at[e_flat].add(1)
    padded = (counts + MOE_BLOCK - 1) // MOE_BLOCK * MOE_BLOCK
    start = jnp.cumsum(counts) - counts
    pend = jnp.cumsum(padded)
    pstart = pend - padded
    dest = pstart[e_sorted] + jnp.arange(M, dtype=jnp.int32) - start[e_sorted]
    n_blocks = -(-M // MOE_BLOCK) + N_EXPERTS
    P = n_blocks * MOE_BLOCK
    tok_buf = jnp.full((P,), N, jnp.int32).at[dest].set(tok_flat[order])
    gate_buf = jnp.zeros((P,), jnp.float32).at[dest].set(g_flat[order])
    block_e = jnp.minimum(jnp.searchsorted(pend, jnp.arange(n_blocks, dtype=jnp.int32) * MOE_BLOCK, side='right'),
                          N_EXPERTS - 1)
    x_pad = jnp.concatenate([xf, jnp.zeros((1, D), xf.dtype)], axis=0)

    def expert_block(args):
        toks, e = args
        xb = x_pad[toks]
        act = clamped_swiglu(xb @ p['moe_w_gate'][e] + p['moe_b_gate'][e],
                             xb @ p['moe_w_up'][e] + p['moe_b_up'][e])
        return act @ p['moe_w_down'][e] + p['moe_b_down'][e]

    y_buf = lax.map(expert_block, (tok_buf.reshape(n_blocks, MOE_BLOCK), block_e))
    y = jax.ops.segment_sum(y_buf.reshape(P, D).astype(jnp.float32) * gate_buf[:, None], tok_buf,
                            num_segments=N + 1)
    return y[:N].astype(h.dtype).reshape(B, T, D)


def decoder_layer(x, k_mem, v_mem, lru_conv, lru_h, gdn_conv, gdn_S, pos0, p):
    mix, new_state = hybrid_mixer(rmsnorm(x, p['norm_mix_g']), lru_conv, lru_h, gdn_conv, gdn_S, pos0, p)
    x = x + mix
    x = x + cross_attention(rmsnorm(x, p['norm_xa_g']), k_mem, v_mem, p['xa_wq'], p['xa_wo'])
    x = x + moe_ffn(rmsnorm(x, p['norm_moe_g']), p)
    return x, new_state


def setup_inputs(seed: int = 0) -> dict:
    key = jax.random.key(seed)
    ks = iter(list(jax.random.split(key, 64)))
    f32 = jnp.float32
    L = DEPTH

    def nrm(shape, scale):
        return jax.random.normal(next(ks), shape, f32) * scale

    def gain(shape):
        return 1.0 + 0.02 * jax.random.normal(next(ks), shape, f32)

    def unif(shape, lo, hi):
        return jax.random.uniform(next(ks), shape, f32, minval=lo, maxval=hi)

    a0 = unif((L, LRU_WIDTH), 0.9, 0.999)
    dt = jnp.exp(unif((L, GDN_HEADS), float(np.log(1e-3)), float(np.log(1e-1))))
    return {
        'x_prompt': nrm((BATCH, SEQ, D_MODEL), 1.0),
        'x_sample': nrm((DEC_BATCH, DEC_SEQ, D_MODEL), 1.0),
        'mem_prompt': nrm((BATCH, MEM_LEN, D_MODEL), 1.0),
        'state_lru_conv': nrm((L, DEC_BATCH, CONV_W - 1, LRU_WIDTH), 1.0),
        'state_lru_h': nrm((L, DEC_BATCH, LRU_WIDTH), 0.5),
        'state_gdn_conv': nrm((L, DEC_BATCH, CONV_W - 1, GDN_CONV_DIM), 1.0),
        'state_gdn_S': nrm((L, DEC_BATCH, GDN_HEADS, GDN_DK, GDN_DV), 0.1),
        'cache_mem_k': nrm((L, DEC_BATCH, MEM_LEN, XA_HEADS, XA_HEAD_DIM), 1.0),
        'cache_mem_v': nrm((L, DEC_BATCH, MEM_LEN, XA_HEADS, XA_HEAD_DIM), 1.0),
        'norm_mix_g': gain((L, D_MODEL)),
        'w_in': nrm((L, D_MODEL, IN_COLS), D_MODEL ** -0.5),
        'lru_conv_w': nrm((L, CONV_W, LRU_WIDTH), CONV_W ** -0.5),
        'lru_conv_b': nrm((L, LRU_WIDTH), 0.01),
        'lru_wx': nrm((L, LRU_BLOCKS, LRU_BLOCK_W, LRU_BLOCK_W), LRU_BLOCK_W ** -0.5),
        'lru_bx': nrm((L, LRU_WIDTH), 0.01),
        'lru_wa': nrm((L, LRU_BLOCKS, LRU_BLOCK_W, LRU_BLOCK_W), LRU_BLOCK_W ** -0.5),
        'lru_ba': nrm((L, LRU_WIDTH), 0.01),
        'lru_a_param': jnp.log(jnp.expm1(-jnp.log(a0))),
        'gdn_conv_w': nrm((L, CONV_W, GDN_CONV_DIM), CONV_W ** -0.5),
        'gdn_A_log': jnp.log(unif((L, GDN_HEADS), 1.0, 16.0)),
        'gdn_dt_bias': dt + jnp.log(-jnp.expm1(-dt)),
        'gdn_norm_g': gain((L, GDN_DV)),
        'w_branch_a': nrm((L, LRU_WIDTH, D_MODEL), LRU_WIDTH ** -0.5),
        'w_branch_b': nrm((L, GDN_VDIM, D_MODEL), GDN_VDIM ** -0.5),
        'w_out': nrm((L, D_MODEL, D_MODEL), D_MODEL ** -0.5),
        'norm_xa_g': gain((L, D_MODEL)),
        'norm_mem_g': gain((L, D_MODEL)),
        'xa_wq': nrm((L, D_MODEL, XA_DIM), D_MODEL ** -0.5),
        'xa_wk': nrm((L, D_MODEL, XA_DIM), D_MODEL ** -0.5),
        'xa_wv': nrm((L, D_MODEL, XA_DIM), D_MODEL ** -0.5),
        'xa_wo': nrm((L, XA_DIM, D_MODEL), XA_DIM ** -0.5),
        'norm_moe_g': gain((L, D_MODEL)),
        'router_w': nrm((L, D_MODEL, N_EXPERTS), D_MODEL ** -0.5),
        'router_b': nrm((L, N_EXPERTS), 0.01),
        'moe_w_gate': nrm((L, N_EXPERTS, D_MODEL, D_FF), D_MODEL ** -0.5),
        'moe_b_gate': nrm((L, N_EXPERTS, D_FF), 0.01),
        'moe_w_up': nrm((L, N_EXPERTS, D_MODEL, D_FF), D_MODEL ** -0.5),
        'moe_b_up': nrm((L, N_EXPERTS, D_FF), 0.01),
        'moe_w_down': nrm((L, N_EXPERTS, D_FF, D_MODEL), D_FF ** -0.5),
        'moe_b_down': nrm((L, N_EXPERTS, D_MODEL), 0.01),
        'norm_final_g': gain((D_MODEL,)),
    }


def reference(x_prompt, x_sample, mem_prompt, state_lru_conv, state_lru_h, state_gdn_conv, state_gdn_S,
              cache_mem_k, cache_mem_v, norm_mix_g, w_in, lru_conv_w, lru_conv_b, lru_wx, lru_bx, lru_wa, lru_ba,
              lru_a_param, gdn_conv_w, gdn_A_log, gdn_dt_bias, gdn_norm_g, w_branch_a, w_branch_b, w_out,
              norm_xa_g, norm_mem_g, xa_wq, xa_wk, xa_wv, xa_wo, norm_moe_g, router_w, router_b,
              moe_w_gate, moe_b_gate, moe_w_up, moe_b_up, moe_w_down, moe_b_down, norm_final_g):
    B = x_prompt.shape[0]
    dt = x_prompt.dtype
    xp, xs = x_prompt, x_sample
    p_lc, p_lh, p_gc, p_gS, p_mk, p_mv = [], [], [], [], [], []
    s_lc, s_lh, s_gc, s_gS = [], [], [], []
    for l in range(DEPTH):
        p = {
            'norm_mix_g': norm_mix_g[l], 'w_in': w_in[l], 'lru_conv_w': lru_conv_w[l], 'lru_conv_b': lru_conv_b[l],
            'lru_wx': lru_wx[l], 'lru_bx': lru_bx[l], 'lru_wa': lru_wa[l], 'lru_ba': lru_ba[l],
            'lru_a_param': lru_a_param[l], 'gdn_conv_w': gdn_conv_w[l], 'gdn_A_log': gdn_A_log[l],
            'gdn_dt_bias': gdn_dt_bias[l], 'gdn_norm_g': gdn_norm_g[l], 'w_branch_a': w_branch_a[l],
            'w_branch_b': w_branch_b[l], 'w_out': w_out[l], 'norm_xa_g': norm_xa_g[l], 'xa_wq': xa_wq[l],
            'xa_wo': xa_wo[l], 'norm_moe_g': norm_moe_g[l], 'router_w': router_w[l], 'router_b': router_b[l],
            'moe_w_gate': moe_w_gate[l], 'moe_b_gate': moe_b_gate[l], 'moe_w_up': moe_w_up[l],
            'moe_b_up': moe_b_up[l], 'moe_w_down': moe_w_down[l], 'moe_b_down': moe_b_down[l],
        }
        k_mem_p, v_mem_p = memory_kv(mem_prompt, norm_mem_g[l], xa_wk[l], xa_wv[l])
        xp, (lc, lh, gc, gS) = decoder_layer(
            xp, k_mem_p, v_mem_p,
            jnp.zeros((B, CONV_W - 1, LRU_WIDTH), dt), jnp.zeros((B, LRU_WIDTH), jnp.float32),
            jnp.zeros((B, CONV_W - 1, GDN_CONV_DIM), dt), jnp.zeros((B, GDN_HEADS, GDN_DK, GDN_DV), jnp.float32),
            0, p)
        p_lc.append(lc); p_lh.append(lh); p_gc.append(gc); p_gS.append(gS); p_mk.append(k_mem_p); p_mv.append(v_mem_p)
        xs, (lc, lh, gc, gS) = decoder_layer(
            xs, cache_mem_k[l], cache_mem_v[l], state_lru_conv[l], state_lru_h[l], state_gdn_conv[l],
            state_gdn_S[l], PAST_LEN, p)
        s_lc.append(lc); s_lh.append(lh); s_gc.append(gc); s_gS.append(gS)
    y_prompt = rmsnorm(xp, norm_final_g)
    y_sample = rmsnorm(xs, norm_final_g)
    return (y_prompt, y_sample, jnp.stack(p_lc), jnp.stack(p_lh), jnp.stack(p_gc), jnp.stack(p_gS),
            jnp.stack(p_mk), jnp.stack(p_mv), jnp.stack(s_lc), jnp.stack(s_lh), jnp.stack(s_gc), jnp.stack(s_gS))
```

```python
import functools
import math

import jax
import jax.numpy as jnp
from jax import lax
from jax.experimental import pallas as pl
from jax.experimental.pallas import tpu as pltpu

F32 = jnp.float32
BF16 = jnp.bfloat16
NORM_EPS = 1e-6
LRU_C = 8.0
CONV_W = 4
SWIGLU_ALPHA = 1.702
SWIGLU_LIMIT = 7.0
TOP_K = 4
VMEM_LIMIT_BYTES = 56 * 1024 * 1024
SUBLANES = 8
LANES = 128


def _params(*sem):
    return pltpu.CompilerParams(dimension_semantics=sem, vmem_limit_bytes=VMEM_LIMIT_BYTES)


def _bdot(a, b):
    return jnp.dot(a.astype(BF16), b.astype(BF16), preferred_element_type=F32)


def _bdot_nt(a, b):
    return lax.dot_general(a.astype(BF16), b.astype(BF16), (((1,), (1,)), ((), ())), preferred_element_type=F32)


def _bdot_tn(a, b):
    return lax.dot_general(a.astype(BF16), b.astype(BF16), (((0,), (0,)), ((), ())), preferred_element_type=F32)


def _split2(x):
    hi = x.astype(BF16)
    lo = (x - hi.astype(F32)).astype(BF16)
    return hi, lo


def _split3(x):
    p1 = x.astype(BF16)
    r1 = x - p1.astype(F32)
    p2 = r1.astype(BF16)
    p3 = (r1 - p2.astype(F32)).astype(BF16)
    return p1, p2, p3


def _dot3(a, b):
    ah, al = _split2(a)
    bh, bl = _split2(b)
    d = functools.partial(jnp.dot, preferred_element_type=F32)
    return d(ah, bh) + (d(ah, bl) + d(al, bh))


def _dot_exact_lhs(m_bf16, x):
    d = functools.partial(jnp.dot, preferred_element_type=F32)
    p1, p2, p3 = _split3(x)
    return d(m_bf16, p1) + (d(m_bf16, p2) + d(m_bf16, p3))


def _dot_exact_rhs(x, m_bf16):
    d = functools.partial(jnp.dot, preferred_element_type=F32)
    p1, p2, p3 = _split3(x)
    return d(p1, m_bf16) + (d(p2, m_bf16) + d(p3, m_bf16))


def _sigmoid(x):
    return jax.nn.sigmoid(x)


def _tile(n, pref):
    if n <= pref:
        return n
    t = pref - pref % SUBLANES
    while n % t:
        t -= SUBLANES
    return t


def _softplus(x):
    return jnp.maximum(x, 0.0) + jnp.log1p(jnp.exp(-jnp.abs(x)))


def _gelu_tanh(x):
    return 0.5 * x * (1.0 + jnp.tanh(0.7978845608028654 * (x + 0.044715 * (x * x * x))))


def _rms(x, g):
    r = lax.rsqrt(jnp.mean(x * x, axis=-1, keepdims=True) + NORM_EPS)
    return x * r * g


def _norm_ba_kernel(x_ref, g_ref, wba_ref, h_ref, ba_ref):
    h = _rms(x_ref[...], g_ref[...]).astype(BF16)
    h_ref[...] = h
    ba_ref[...] = jnp.dot(h, wba_ref[...], preferred_element_type=F32)


def _norm_ba(x, g, w_ba, tm):
    m, d = x.shape
    nb = w_ba.shape[1]
    return pl.pallas_call(
        _norm_ba_kernel,
        grid=(m // tm,),
        in_specs=[pl.BlockSpec((tm, d), lambda i: (i, 0)),
                  pl.BlockSpec((1, d), lambda i: (0, 0)),
                  pl.BlockSpec((d, nb), lambda i: (0, 0))],
        out_specs=[pl.BlockSpec((tm, d), lambda i: (i, 0)),
                   pl.BlockSpec((tm, nb), lambda i: (i, 0))],
        out_shape=[jax.ShapeDtypeStruct((m, d), BF16), jax.ShapeDtypeStruct((m, nb), F32)],
        compiler_params=_params("parallel"),
        name="norm_ba",
    )(x, g, w_ba)


def _mm_kernel(a_ref, w_ref, o_ref):
    o_ref[...] = jnp.dot(a_ref[...], w_ref[...], preferred_element_type=F32).astype(o_ref.dtype)


def _matmul(a, w, n_cols, tm, tn, out_dtype, name):
    m, k = a.shape
    return pl.pallas_call(
        _mm_kernel,
        grid=(m // tm, n_cols // tn),
        in_specs=[pl.BlockSpec((tm, k), lambda i, j: (i, 0)),
                  pl.BlockSpec((k, tn), lambda i, j: (0, j))],
        out_specs=pl.BlockSpec((tm, tn), lambda i, j: (i, j)),
        out_shape=jax.ShapeDtypeStruct((m, n_cols), out_dtype),
        compiler_params=_params("parallel", "parallel"),
        name=name,
    )(a, w)


def _conv4_into(xe_ref, rows, cw):
    y = xe_ref[SUBLANES:SUBLANES + rows, :] * cw[CONV_W - 1:CONV_W]
    for s in range(1, CONV_W):
        y = y + xe_ref[SUBLANES - s:SUBLANES - s + rows, :] * cw[CONV_W - 1 - s:CONV_W - s]
    return y


def _scan8(a3, b3):
    row = lax.broadcasted_iota(jnp.int32, a3.shape, 1)
    for s in (1, 2, 4):
        a_sh = pltpu.roll(a3, s, axis=1)
        b_sh = pltpu.roll(b3, s, axis=1)
        m = row >= s
        b3 = jnp.where(m, a3 * b_sh + b3, b3)
        a3 = jnp.where(m, a3 * a_sh, a3)
    return a3, b3


def _lru_gates(xc, wg_ref, bx, bga, ap, first_row_is_pos0):
    nblk = wg_ref.shape[0]
    bw = wg_ref.shape[1]
    xcb = xc.astype(BF16)
    gx, ga = [], []
    for n in range(nblk):
        r = jnp.dot(xcb[:, n * bw:(n + 1) * bw], wg_ref[n], preferred_element_type=F32)
        gx.append(r[:, :bw])
        ga.append(r[:, bw:])
    gate_x = _sigmoid(jnp.concatenate(gx, axis=1) + bx)
    gate_a = _sigmoid(jnp.concatenate(ga, axis=1) + bga)
    log_a = (-LRU_C) * gate_a * _softplus(ap)
    a = jnp.exp(log_a)
    mult = jnp.sqrt(-jnp.tanh(log_a) * (a * a + 1.0))
    if first_row_is_pos0 is not None:
        mult = jnp.where(first_row_is_pos0, 1.0, mult)
    return a, xc * gate_x * mult


def _lru_seq_kernel(xa_ref, ya_ref, tail0_ref, h0_ref, cw_ref, cb_ref, wg_ref, bx_ref, bga_ref, ap_ref,
                    y_ref, hl_ref, xe_ref, a_ref, b_ref, h_ref, *, tt, pos0):
    t = pl.program_id(1)
    w = xa_ref.shape[1]

    @pl.when(t == 0)
    def _():
        xe_ref[0:SUBLANES, :] = tail0_ref[0]
        h_ref[...] = h0_ref[0]

    xe_ref[SUBLANES:SUBLANES + tt, :] = xa_ref[...]
    xc = _conv4_into(xe_ref, tt, cw_ref[...]) + cb_ref[...]
    xe_ref[0:SUBLANES, :] = xe_ref[tt:tt + SUBLANES, :]
    first = None
    if pos0 == 0:
        first = (lax.broadcasted_iota(jnp.int32, (tt, 1), 0) + t * tt) == 0
    a, b = _lru_gates(xc, wg_ref, bx_ref[...], bga_ref[...], ap_ref[...], first)
    a3, b3 = _scan8(a.reshape(tt // SUBLANES, SUBLANES, w), b.reshape(tt // SUBLANES, SUBLANES, w))
    a_ref[...] = a3
    b_ref[...] = b3

    def body(g, h):
        hg = a_ref[g] * h + b_ref[g]
        b_ref[g] = hg
        return hg[SUBLANES - 1:SUBLANES, :]

    h = lax.fori_loop(0, tt // SUBLANES, body, h_ref[...])
    h_ref[...] = h
    hl_ref[0] = h
    hs = b_ref[...].reshape(tt, w)
    y_ref[...] = hs * _gelu_tanh(ya_ref[...])


def _lru_seq(proj, row0, batch, seq, tt, tail0, h0, cw, cb, wg, bx, bga, ap, pos0):
    w = cw.shape[1]
    nt = seq // tt
    rb0 = row0 // tt
    vec = lambda: pl.BlockSpec((1, w), lambda b, t: (0, 0))
    return pl.pallas_call(
        functools.partial(_lru_seq_kernel, tt=tt, pos0=pos0),
        grid=(batch, nt),
        in_specs=[pl.BlockSpec((tt, w), lambda b, t: (rb0 + b * nt + t, 0)),
                  pl.BlockSpec((tt, w), lambda b, t: (rb0 + b * nt + t, 1)),
                  pl.BlockSpec((1, SUBLANES, w), lambda b, t: (b, 0, 0)),
                  pl.BlockSpec((1, 1, w), lambda b, t: (b, 0, 0)),
                  pl.BlockSpec((CONV_W, w), lambda b, t: (0, 0)),
                  vec(),
                  pl.BlockSpec(wg.shape, lambda b, t: (0, 0, 0)),
                  vec(), vec(), vec()],
        out_specs=[pl.BlockSpec((tt, w), lambda b, t: (b * nt + t, 0)),
                   pl.BlockSpec((1, 1, w), lambda b, t: (b, 0, 0))],
        out_shape=[jax.ShapeDtypeStruct((batch * seq, w), F32),
                   jax.ShapeDtypeStruct((batch, 1, w), F32)],
        scratch_shapes=[pltpu.VMEM((SUBLANES + tt, w), F32),
                        pltpu.VMEM((tt // SUBLANES, SUBLANES, w), F32),
                        pltpu.VMEM((tt // SUBLANES, SUBLANES, w), F32),
                        pltpu.VMEM((1, w), F32)],
        compiler_params=_params("parallel", "arbitrary"),
        name="lru_seq",
    )(proj, proj, tail0, h0, cw, cb, wg, bx, bga, ap)


def _nilpotent_inverse(nm, c, eye):
    p = eye + nm
    npow = nm
    k = 1
    while 2 * k < c:
        npow = _dot3(npow, npow)
        p = p + _dot3(p, npow)
        k *= 2
    return p


def _gdn_kernel(q_ref, k_ref, v_ref, z_ref, ba_ref, bat_ref, tail0_ref, s0_ref, cw_ref, alog_ref, dtb_ref,
                alogt_ref, dtbt_ref, ng_ref, y_ref, sout_ref, xe_ref, s_ref, *, c, heads, dk, dv):
    ci = pl.program_id(1)
    kdim = heads * dk

    @pl.when(ci == 0)
    def _():
        xe_ref[0:SUBLANES, :] = tail0_ref[0]
        s_ref[...] = s0_ref[0]

    xe_ref[SUBLANES:SUBLANES + c, 0:kdim] = q_ref[...]
    xe_ref[SUBLANES:SUBLANES + c, kdim:2 * kdim] = k_ref[...]
    xe_ref[SUBLANES:SUBLANES + c, 2 * kdim:] = v_ref[...]
    pre = _conv4_into(xe_ref, c, cw_ref[...])
    xe_ref[0:SUBLANES, :] = xe_ref[c:c + SUBLANES, :]
    qkv = pre * _sigmoid(pre)

    ba = ba_ref[...]
    bat = bat_ref[0, 0]
    beta = _sigmoid(ba[:, 0:heads])
    g = -jnp.exp(alog_ref[...]) * _softplus(ba[:, heads:2 * heads] + dtb_ref[...])
    gt = -jnp.exp(alogt_ref[...]) * _softplus(bat[heads:2 * heads, :] + dtbt_ref[...])
    ii = lax.broadcasted_iota(jnp.int32, (c, c), 0)
    jj = lax.broadcasted_iota(jnp.int32, (c, c), 1)
    incl = ii >= jj
    strict = ii > jj
    eye = (ii == jj).astype(F32)
    gc = _dot_exact_lhs(incl.astype(BF16), g)
    gct = _dot_exact_rhs(gt, (ii <= jj).astype(BF16))
    egc = jnp.exp(gc)
    gc_last = gc[c - 1:c, :]
    g_last = jnp.exp(gc_last)
    ekd = jnp.exp(gc_last - gc)
    z = z_ref[...]
    ng = ng_ref[...]
    scale = dk ** -0.5

    ys = []
    for h in range(heads):
        qh = qkv[:, h * dk:(h + 1) * dk]
        kh = qkv[:, kdim + h * dk:kdim + (h + 1) * dk]
        vh = qkv[:, 2 * kdim + h * dv:2 * kdim + (h + 1) * dv]
        qh = qh * lax.rsqrt(jnp.sum(qh * qh, axis=-1, keepdims=True) + NORM_EPS)
        kh = kh * lax.rsqrt(jnp.sum(kh * kh, axis=-1, keepdims=True) + NORM_EPS)
        bh = beta[:, h:h + 1]
        diff = gc[:, h:h + 1] - gct[h:h + 1, :]
        dec = jnp.where(incl, jnp.exp(jnp.where(incl, diff, 0.0)), 0.0)
        kb = kh * bh
        nm = jnp.where(strict, -(_bdot_nt(kb, kh) * dec), 0.0)
        tinv = _nilpotent_inverse(nm, c, eye)
        rhs = jnp.concatenate([vh * bh, kb * egc[:, h:h + 1]], axis=1)
        sol = _dot3(tinv, rhs)
        u = sol[:, :dv]
        w = sol[:, dv:]
        qs = qh * scale
        qk = _bdot_nt(qs, kh) * dec
        qd = qs * egc[:, h:h + 1]
        kd = kh * ekd[:, h:h + 1]
        s = s_ref[h]
        vnew = u - _bdot(w, s)
        o = _bdot(qd, s) + _bdot(qk, vnew)
        s_ref[h] = s * g_last[:, h:h + 1] + _bdot_tn(kd, vnew)
        zz = z[:, h * dv:(h + 1) * dv]
        ys.append(_rms(o, ng) * (zz * _sigmoid(zz)))
    y_ref[...] = jnp.concatenate(ys, axis=1)
    sout_ref[0] = s_ref[...]


def _gdn(proj, col0, row0, batch, seq, c, ba, bat, tail0, s0, cw, alog, dtb, ng, heads, dk, dv):
    kdim = heads * dk
    nc = seq // c
    rb0 = row0 // c
    cb0 = col0 // kdim
    rows = lambda b, ci: rb0 + b * nc + ci
    small = lambda shape: pl.BlockSpec(shape, lambda b, ci: (0,) * len(shape))
    return pl.pallas_call(
        functools.partial(_gdn_kernel, c=c, heads=heads, dk=dk, dv=dv),
        grid=(batch, nc),
        in_specs=[pl.BlockSpec((c, kdim), lambda b, ci: (rows(b, ci), cb0)),
                  pl.BlockSpec((c, kdim), lambda b, ci: (rows(b, ci), cb0 + 1)),
                  pl.BlockSpec((c, kdim), lambda b, ci: (rows(b, ci), cb0 + 2)),
                  pl.BlockSpec((c, kdim), lambda b, ci: (rows(b, ci), cb0 + 3)),
                  pl.BlockSpec((c, 2 * heads), lambda b, ci: (rows(b, ci), 0)),
                  pl.BlockSpec((1, 1, 2 * heads, c), lambda b, ci: (b, ci, 0, 0)),
                  pl.BlockSpec((1, SUBLANES, 3 * kdim), lambda b, ci: (b, 0, 0)),
                  pl.BlockSpec((1, heads, dk, dv), lambda b, ci: (b, 0, 0, 0)),
                  small((CONV_W, 3 * kdim)),
                  small((1, heads)), small((1, heads)), small((heads, 1)), small((heads, 1)),
                  small((1, dv))],
        out_specs=[pl.BlockSpec((c, heads * dv), lambda b, ci: (b * nc + ci, 0)),
                   pl.BlockSpec((1, heads, dk, dv), lambda b, ci: (b, 0, 0, 0))],
        out_shape=[jax.ShapeDtypeStruct((batch * seq, heads * dv), F32),
                   jax.ShapeDtypeStruct((batch, heads, dk, dv), F32)],
        scratch_shapes=[pltpu.VMEM((SUBLANES + c, 3 * kdim), F32),
                        pltpu.VMEM((heads, dk, dv), F32)],
        compiler_params=_params("parallel", "arbitrary"),
        name="gdn",
    )(proj, proj, proj, proj, ba, bat, tail0, s0, cw, alog, dtb, alog.T, dtb.T, ng)


def _merge_kernel(h_ref, ya_ref, yb_ref, wga_ref, wgb_ref, wa_ref, wb_ref, u_ref):
    d = functools.partial(jnp.dot, preferred_element_type=F32)
    h = h_ref[...]
    u = _sigmoid(d(h, wga_ref[...])) * d(ya_ref[...].astype(BF16), wa_ref[...])
    u = u + _sigmoid(d(h, wgb_ref[...])) * d(yb_ref[...].astype(BF16), wb_ref[...])
    u_ref[...] = u.astype(BF16)


def _merge(h, ya, yb, w_g, w_a, w_b, tm, tn):
    m, d = h.shape
    n = w_a.shape[1]
    nj = n // tn
    return pl.pallas_call(
        _merge_kernel,
        grid=(m // tm, nj),
        in_specs=[pl.BlockSpec((tm, d), lambda i, j: (i, 0)),
                  pl.BlockSpec((tm, ya.shape[1]), lambda i, j: (i, 0)),
                  pl.BlockSpec((tm, yb.shape[1]), lambda i, j: (i, 0)),
                  pl.BlockSpec((d, tn), lambda i, j: (0, j)),
                  pl.BlockSpec((d, tn), lambda i, j: (0, nj + j)),
                  pl.BlockSpec((w_a.shape[0], tn), lambda i, j: (0, j)),
                  pl.BlockSpec((w_b.shape[0], tn), lambda i, j: (0, j))],
        out_specs=pl.BlockSpec((tm, tn), lambda i, j: (i, j)),
        out_shape=jax.ShapeDtypeStruct((m, n), BF16),
        compiler_params=_params("parallel", "parallel"),
        name="merge",
    )(h, ya, yb, w_g, w_g, w_a, w_b)


def _out_norm_kernel(u_ref, w_ref, x_ref, g_ref, x1_ref, h_ref):
    x1 = x_ref[...] + jnp.dot(u_ref[...], w_ref[...], preferred_element_type=F32)
    x1_ref[...] = x1
    h_ref[...] = _rms(x1, g_ref[...]).astype(BF16)


def _out_norm(u, w, x, g, tm):
    m, d = x.shape
    k = u.shape[1]
    return pl.pallas_call(
        _out_norm_kernel,
        grid=(m // tm,),
        in_specs=[pl.BlockSpec((tm, k), lambda i: (i, 0)),
                  pl.BlockSpec((k, d), lambda i: (0, 0)),
                  pl.BlockSpec((tm, d), lambda i: (i, 0)),
                  pl.BlockSpec((1, d), lambda i: (0, 0))],
        out_specs=[pl.BlockSpec((tm, d), lambda i: (i, 0)),
                   pl.BlockSpec((tm, d), lambda i: (i, 0))],
        out_shape=[jax.ShapeDtypeStruct((m, d), F32), jax.ShapeDtypeStruct((m, d), BF16)],
        compiler_params=_params("parallel"),
        name="out_norm",
    )(u, w, x, g)


def _memkv_kernel(m_ref, g_ref, wk_ref, wv_ref, k_ref, v_ref):
    mn = _rms(m_ref[...], g_ref[...]).astype(BF16)
    k_ref[...] = jnp.dot(mn, wk_ref[...], preferred_element_type=F32)
    v_ref[...] = jnp.dot(mn, wv_ref[...], preferred_element_type=F32)


def _memkv(mem, g, wk, wv, tm):
    m, d = mem.shape
    n = wk.shape[1]
    return pl.pallas_call(
        _memkv_kernel,
        grid=(m // tm,),
        in_specs=[pl.BlockSpec((tm, d), lambda i: (i, 0)),
                  pl.BlockSpec((1, d), lambda i: (0, 0)),
                  pl.BlockSpec((d, n), lambda i: (0, 0)),
                  pl.BlockSpec((d, n), lambda i: (0, 0))],
        out_specs=[pl.BlockSpec((tm, n), lambda i: (i, 0)), pl.BlockSpec((tm, n), lambda i: (i, 0))],
        out_shape=[jax.ShapeDtypeStruct((m, n), F32), jax.ShapeDtypeStruct((m, n), F32)],
        compiler_params=_params("parallel"),
        name="memkv",
    )(mem, g, wk, wv)


def _attn_kernel(q_ref, k_ref, v_ref, o_ref, *, heads, hd):
    q = q_ref[...].astype(BF16)
    k = k_ref[0].astype(BF16)
    v = v_ref[0].astype(BF16)
    scale = hd ** -0.5
    outs = []
    for h in range(heads):
        sl = slice(h * hd, (h + 1) * hd)
        s = lax.dot_general(q[:, sl], k[:, sl], (((1,), (1,)), ((), ())), preferred_element_type=F32) * scale
        p = jnp.exp(s - jnp.max(s, axis=-1, keepdims=True))
        l = jnp.sum(p, axis=-1, keepdims=True)
        outs.append(jnp.dot(p.astype(BF16), v[:, sl], preferred_element_type=F32) / l)
    o_ref[...] = jnp.concatenate(outs, axis=1)


def _attn(q, row0, batch, seq, tq, k, v, heads, hd):
    nt = seq // tq
    rb0 = row0 // tq
    n = heads * hd
    mem = k.shape[1]
    return pl.pallas_call(
        functools.partial(_attn_kernel, heads=heads, hd=hd),
        grid=(batch, nt),
        in_specs=[pl.BlockSpec((tq, n), lambda b, t: (rb0 + b * nt + t, 0)),
                  pl.BlockSpec((1, mem, n), lambda b, t: (b, 0, 0)),
                  pl.BlockSpec((1, mem, n), lambda b, t: (b, 0, 0))],
        out_specs=pl.BlockSpec((tq, n), lambda b, t: (b * nt + t, 0)),
        out_shape=jax.ShapeDtypeStruct((batch * seq, n), F32),
        compiler_params=_params("parallel", "parallel"),
        name="attn",
    )(q, k, v)


def _xo_router_kernel(o_ref, wo_ref, x1_ref, g_ref, rw_ref, rb_ref, x2_ref, h_ref, lg_ref):
    x2 = x1_ref[...] + jnp.dot(o_ref[...].astype(BF16), wo_ref[...], preferred_element_type=F32)
    x2_ref[...] = x2
    h = _rms(x2, g_ref[...])
    h_ref[...] = h
    lg_ref[...] = _dot3(h, rw_ref[...]) + rb_ref[...]


def _xo_router(o, wo, x1, g, rw, rb, tm):
    m, d = x1.shape
    k = o.shape[1]
    ne = rw.shape[1]
    row = lambda w: pl.BlockSpec((tm, w), lambda i: (i, 0))
    full = lambda a: pl.BlockSpec(a.shape, lambda i: (0, 0))
    return pl.pallas_call(
        _xo_router_kernel,
        grid=(m // tm,),
        in_specs=[row(k), full(wo), row(d), full(g), full(rw), full(rb)],
        out_specs=[row(d), row(d), row(ne)],
        out_shape=[jax.ShapeDtypeStruct((m, d), F32), jax.ShapeDtypeStruct((m, d), F32),
                   jax.ShapeDtypeStruct((m, ne), F32)],
        compiler_params=_params("parallel"),
        name="xo_router",
    )(o, wo, x1, g, rw, rb)


def _moe_kernel(e_ref, r0_ref, n_ref, code_ref, h_hbm, wg_ref, wu_ref, wd_ref, bg_ref, bu_ref, bd_ref,
                slots_hbm, xbuf, acc, wgb, wub, wdb, gsem, ssem, *, sub, nf):
    s = pl.program_id(0)
    f = pl.program_id(1)
    n = n_ref[s]
    r0 = r0_ref[s]

    @pl.when(jnp.logical_and(s == 0, f == 0))
    def _():
        xbuf[...] = jnp.zeros_like(xbuf)

    def row_in(i):
        tok = code_ref[r0 + i] // TOP_K
        return pltpu.make_async_copy(h_hbm.at[pl.ds(tok, 1), :], xbuf.at[pl.ds(i, 1), :], gsem)

    def row_out(i):
        return pltpu.make_async_copy(acc.at[pl.ds(i, 1), :], slots_hbm.at[pl.ds(code_ref[r0 + i], 1), :], ssem)

    @pl.when(n > 0)
    def _active():
        @pl.when(f == 0)
        def _gather():
            lax.fori_loop(0, n, lambda i, c: (row_in(i).start(), c)[1], 0)
            lax.fori_loop(0, n, lambda i, c: (row_in(i).wait(), c)[1], 0)

        wgb[...] = wg_ref[0].astype(BF16)
        wub[...] = wu_ref[0].astype(BF16)
        wdb[...] = wd_ref[0].astype(BF16)
        bg = bg_ref[0]
        bu = bu_ref[0]
        bd = bd_ref[0]

        def sub_block(j, c):
            off = pl.multiple_of(j * sub, sub)
            x = xbuf[pl.ds(off, sub), :].astype(BF16)
            gate = jnp.dot(x, wgb[...], preferred_element_type=F32) + bg
            up = jnp.dot(x, wub[...], preferred_element_type=F32) + bu
            gate = jnp.minimum(gate, SWIGLU_LIMIT)
            up = jnp.clip(up, -SWIGLU_LIMIT, SWIGLU_LIMIT)
            act = gate * _sigmoid(SWIGLU_ALPHA * gate) * (up + 1.0)
            part = jnp.dot(act.astype(BF16), wdb[...], preferred_element_type=F32)

            @pl.when(f == 0)
            def _():
                acc[pl.ds(off, sub), :] = part + bd

            @pl.when(f > 0)
            def _():
                acc[pl.ds(off, sub), :] += part

            return c

        lax.fori_loop(0, (n + sub - 1) // sub, sub_block, 0)

        @pl.when(f == nf - 1)
        def _scatter():
            lax.fori_loop(0, n, lambda i, c: (row_out(i).start(), c)[1], 0)
            lax.fori_loop(0, n, lambda i, c: (row_out(i).wait(), c)[1], 0)


def _moe(h, sb_e, sb_r0, sb_n, codes, wg, wu, wd, bg, bu, bd, cap, sub, tf):
    n_tok, d = h.shape
    ne, _, dff = wg.shape
    nf = dff // tf
    g = sb_e.shape[0]
    fi = lambda s, f, n_ref: jnp.where(n_ref[s] > 0, f, nf - 1)
    grid_spec = pltpu.PrefetchScalarGridSpec(
        num_scalar_prefetch=4,
        grid=(g, nf),
        in_specs=[pl.BlockSpec(memory_space=pl.ANY),
                  pl.BlockSpec((1, d, tf), lambda s, f, e, r, n, c: (e[s], 0, fi(s, f, n))),
                  pl.BlockSpec((1, d, tf), lambda s, f, e, r, n, c: (e[s], 0, fi(s, f, n))),
                  pl.BlockSpec((1, tf, d), lambda s, f, e, r, n, c: (e[s], fi(s, f, n), 0)),
                  pl.BlockSpec((1, 1, tf), lambda s, f, e, r, n, c: (e[s], 0, fi(s, f, n))),
                  pl.BlockSpec((1, 1, tf), lambda s, f, e, r, n, c: (e[s], 0, fi(s, f, n))),
                  pl.BlockSpec((1, 1, d), lambda s, f, e, r, n, c: (e[s], 0, 0))],
        out_specs=pl.BlockSpec(memory_space=pl.ANY),
        scratch_shapes=[pltpu.VMEM((cap, d), F32), pltpu.VMEM((cap, d), F32),
                        pltpu.VMEM((d, tf), BF16), pltpu.VMEM((d, tf), BF16), pltpu.VMEM((tf, d), BF16),
                        pltpu.SemaphoreType.DMA(()), pltpu.SemaphoreType.DMA(())],
    )
    return pl.pallas_call(
        functools.partial(_moe_kernel, sub=sub, nf=nf),
        grid_spec=grid_spec,
        out_shape=jax.ShapeDtypeStruct((n_tok * TOP_K, d), F32),
        compiler_params=_params("arbitrary", "arbitrary"),
        name="moe",
    )(sb_e, sb_r0, sb_n, codes, h, wg, wu, wd, bg.reshape(ne, 1, dff), bu.reshape(ne, 1, dff),
      bd.reshape(ne, 1, d))


def _route(logits, cap, n_sb_max):
    n_tok, ne = logits.shape
    top_val, top_idx = lax.top_k(logits, TOP_K)
    gates = jax.nn.softmax(top_val, axis=-1)
    e_flat = top_idx.reshape(-1).astype(jnp.int32)
    codes = jnp.argsort(e_flat, stable=True).astype(jnp.int32)
    counts = jnp.zeros((ne,), jnp.int32).at[e_flat].add(1)
    start = jnp.cumsum(counts) - counts
    n_sb = (counts + cap - 1) // cap
    sb_end = jnp.cumsum(n_sb)
    sb_start = sb_end - n_sb
    total = sb_end[-1]
    s = jnp.arange(n_sb_max, dtype=jnp.int32)
    e_s = jnp.minimum(jnp.searchsorted(sb_end, s, side='right'), ne - 1).astype(jnp.int32)
    e_last = e_s[jnp.maximum(total - 1, 0)]
    active = s < total
    e_s = jnp.where(active, e_s, e_last)
    within = s - sb_start[e_s]
    r0 = jnp.where(active, start[e_s] + within * cap, 0).astype(jnp.int32)
    n = jnp.where(active, jnp.clip(counts[e_s] - within * cap, 0, cap), 0).astype(jnp.int32)
    return gates, codes, e_s, r0, n


def _final_kernel(x2_ref, slots_ref, gates_ref, g_ref, y_ref, *, d):
    x = x2_ref[...]
    gt = gates_ref[...]
    for k in range(TOP_K):
        x = x + slots_ref[:, k * d:(k + 1) * d] * gt[:, k:k + 1]
    y_ref[...] = _rms(x, g_ref[...])


def _final(x2, slots, gates, g, row0, rows, tm):
    d = x2.shape[1]
    rb0 = row0 // tm
    return pl.pallas_call(
        functools.partial(_final_kernel, d=d),
        grid=(rows // tm,),
        in_specs=[pl.BlockSpec((tm, d), lambda i: (rb0 + i, 0)),
                  pl.BlockSpec((tm, TOP_K * d), lambda i: (rb0 + i, 0)),
                  pl.BlockSpec((tm, TOP_K), lambda i: (rb0 + i, 0)),
                  pl.BlockSpec((1, d), lambda i: (0, 0))],
        out_specs=pl.BlockSpec((tm, d), lambda i: (i, 0)),
        out_shape=jax.ShapeDtypeStruct((rows, d), F32),
        compiler_params=_params("parallel"),
        name="final",
    )(x2, slots, gates, g)


def _pad_tail(buf):
    return jnp.pad(buf, ((0, 0), (SUBLANES - (CONV_W - 1), 0), (0, 0)))


def _chunked_t(ba, row0, batch, seq, c):
    x = ba[row0:row0 + batch * seq].reshape(batch, seq // c, c, ba.shape[1])
    return jnp.swapaxes(x, 2, 3)


def kernel(x_prompt, x_sample, mem_prompt, state_lru_conv, state_lru_h, state_gdn_conv, state_gdn_S, cache_mem_k, cache_mem_v, norm_mix_g, w_in, lru_conv_w, lru_conv_b, lru_wx, lru_bx, lru_wa, lru_ba, lru_a_param, gdn_conv_w, gdn_A_log, gdn_dt_bias, gdn_norm_g, w_branch_a, w_branch_b, w_out, norm_xa_g, norm_mem_g, xa_wq, xa_wk, xa_wv, xa_wo, norm_moe_g, router_w, router_b, moe_w_gate, moe_b_gate, moe_w_up, moe_b_up, moe_w_down, moe_b_down, norm_final_g):
    depth = w_in.shape[0]
    assert depth == 1
    bp, tp, d = x_prompt.shape
    bs, ts, _ = x_sample.shape
    np_, ns = bp * tp, bs * ts
    n_tok = np_ + ns
    lw = lru_conv_w.shape[2]
    heads, dk, dv = state_gdn_S.shape[2:]
    kdim = heads * dk
    mem_len = mem_prompt.shape[1]
    xa_heads, xa_hd = cache_mem_k.shape[3:]
    xa_dim = xa_heads * xa_hd
    ne = router_w.shape[2]
    past_len = 16384
    gdn_chunk = 64

    row = lambda v: v.reshape(1, -1)
    x = jnp.concatenate([x_prompt.reshape(np_, d), x_sample.reshape(ns, d)], axis=0)

    c_main = 2 * lw + 3 * kdim + heads * dv
    w_in0 = w_in[0]
    w_main = w_in0[:, :c_main].astype(BF16)
    w_ba = w_in0[:, c_main:c_main + 2 * heads].astype(BF16)
    w_g = w_in0[:, c_main + 2 * heads:].astype(BF16)
    w_gates = jnp.concatenate([lru_wx[0], lru_wa[0]], axis=-1).astype(BF16)

    rows_gcd = math.gcd(np_, ns)
    tm_s = _tile(rows_gcd, 256)
    tm_m = _tile(rows_gcd, 512)
    tm_l = _tile(rows_gcd, 1024)
    h1, ba = _norm_ba(x, row(norm_mix_g[0]), w_ba, tm_m)
    proj = _matmul(h1, w_main, c_main, tm_l, _tile(c_main, 1024), F32, "in_proj")

    lru_args = (lru_conv_w[0], row(lru_conv_b[0]), w_gates, row(lru_bx[0]), row(lru_ba[0]), row(lru_a_param[0]))
    ya_p, lru_h_p = _lru_seq(proj, 0, bp, tp, _tile(tp, 256), jnp.zeros((bp, SUBLANES, lw), F32),
                             jnp.zeros((bp, 1, lw), F32), *lru_args, pos0=0)
    ya_s, lru_h_s = _lru_seq(proj, np_, bs, ts, ts, _pad_tail(state_lru_conv[0]), state_lru_h[0][:, None, :],
                             *lru_args, pos0=past_len)

    gdn_args = (gdn_conv_w[0], row(gdn_A_log[0]), row(gdn_dt_bias[0]), row(gdn_norm_g[0]), heads, dk, dv)
    cp = min(gdn_chunk, tp)
    cs = min(gdn_chunk, ts)
    yb_p, gdn_s_p = _gdn(proj, 2 * lw, 0, bp, tp, cp, ba, _chunked_t(ba, 0, bp, tp, cp),
                         jnp.zeros((bp, SUBLANES, 3 * kdim), F32), jnp.zeros((bp, heads, dk, dv), F32), *gdn_args)
    yb_s, gdn_s_s = _gdn(proj, 2 * lw, np_, bs, ts, cs, ba, _chunked_t(ba, np_, bs, ts, cs),
                         _pad_tail(state_gdn_conv[0]), state_gdn_S[0], *gdn_args)

    ya = jnp.concatenate([ya_p, ya_s], axis=0)
    yb = jnp.concatenate([yb_p, yb_s], axis=0)
    u = _merge(h1, ya, yb, w_g, w_branch_a[0].astype(BF16), w_branch_b[0].astype(BF16), tm_l, _tile(d, 512))
    x1, h2 = _out_norm(u, w_out[0].astype(BF16), x, row(norm_xa_g[0]), tm_m)

    k_p, v_p = _memkv(mem_prompt.reshape(bp * mem_len, d), row(norm_mem_g[0]), xa_wk[0].astype(BF16),
                      xa_wv[0].astype(BF16), _tile(bp * mem_len, 256))
    q = _matmul(h2, xa_wq[0].astype(BF16), xa_dim, tm_l, xa_dim, F32, "xa_q")
    o_p = _attn(q, 0, bp, tp, _tile(tp, 512), k_p.reshape(bp, mem_len, xa_dim), v_p.reshape(bp, mem_len, xa_dim),
                xa_heads, xa_hd)
    o_s = _attn(q, np_, bs, ts, ts, cache_mem_k[0].reshape(bs, mem_len, xa_dim),
                cache_mem_v[0].reshape(bs, mem_len, xa_dim), xa_heads, xa_hd)
    o = jnp.concatenate([o_p, o_s], axis=0)
    x2, h3, logits = _xo_router(o, xa_wo[0].astype(BF16), x1, row(norm_moe_g[0]), router_w[0], row(router_b[0]),
                                tm_m)

    cap, sub, tf = 1536, 256, _tile(moe_w_gate.shape[3], 256)
    n_sb_max = (n_tok * TOP_K + ne * (cap - 1)) // cap
    gates, codes, sb_e, sb_r0, sb_n = _route(logits, cap, n_sb_max)
    slots = _moe(h3, sb_e, sb_r0, sb_n, codes, moe_w_gate[0], moe_w_up[0], moe_w_down[0],
                 moe_b_gate[0], moe_b_up[0], moe_b_down[0], cap, sub, tf)
    slots = slots.reshape(n_tok, TOP_K * d)
    y_p = _final(x2, slots, gates, row(norm_final_g), 0, np_, tm_s)
    y_s = _final(x2, slots, gates, row(norm_final_g), np_, ns, tm_s)

    pp = proj[:np_].reshape(bp, tp, c_main)
    ps = proj[np_:].reshape(bs, ts, c_main)
    keep = CONV_W - 1
    return (y_p.reshape(bp, tp, d), y_s.reshape(bs, ts, d),
            pp[:, tp - keep:, :lw][None], lru_h_p.reshape(1, bp, lw),
            pp[:, tp - keep:, 2 * lw:2 * lw + 3 * kdim][None], gdn_s_p[None],
            k_p.reshape(1, bp, mem_len, xa_heads, xa_hd), v_p.reshape(1, bp, mem_len, xa_heads, xa_hd),
            ps[:, ts - keep:, :lw][None], lru_h_s.reshape(1, bs, lw),
            ps[:, ts - keep:, 2 * lw:2 * lw + 3 * kdim][None], gdn_s_s[None])
```

```python
import functools
import math

import jax
import jax.numpy as jnp
from jax import lax
from jax.experimental import pallas as pl
from jax.experimental.pallas import tpu as pltpu

F32 = jnp.float32
BF16 = jnp.bfloat16
NORM_EPS = 1e-6
LRU_C = 8.0
CONV_W = 4
SWIGLU_ALPHA = 1.702
SWIGLU_LIMIT = 7.0
TOP_K = 4
TOP_K_SHIFT = 2
DMA_UNROLL = 8
VMEM_LIMIT_BYTES = 56 * 1024 * 1024
SUBLANES = 8
LANES = 128


def _params(*sem):
    return pltpu.CompilerParams(dimension_semantics=sem, vmem_limit_bytes=VMEM_LIMIT_BYTES)


def _bdot(a, b):
    return jnp.dot(a.astype(BF16), b.astype(BF16), preferred_element_type=F32)


def _bdot_nt(a, b):
    return lax.dot_general(a.astype(BF16), b.astype(BF16), (((1,), (1,)), ((), ())), preferred_element_type=F32)


def _bdot_tn(a, b):
    return lax.dot_general(a.astype(BF16), b.astype(BF16), (((0,), (0,)), ((), ())), preferred_element_type=F32)


def _split2(x):
    hi = x.astype(BF16)
    lo = (x - hi.astype(F32)).astype(BF16)
    return hi, lo


def _split3(x):
    p1 = x.astype(BF16)
    r1 = x - p1.astype(F32)
    p2 = r1.astype(BF16)
    p3 = (r1 - p2.astype(F32)).astype(BF16)
    return p1, p2, p3


def _dot3(a, b):
    ah, al = _split2(a)
    bh, bl = _split2(b)
    d = functools.partial(jnp.dot, preferred_element_type=F32)
    return d(ah, bh) + (d(ah, bl) + d(al, bh))


def _dot_exact_lhs(m_bf16, x):
    d = functools.partial(jnp.dot, preferred_element_type=F32)
    p1, p2, p3 = _split3(x)
    return d(m_bf16, p1) + (d(m_bf16, p2) + d(m_bf16, p3))


def _dot_exact_rhs(x, m_bf16):
    d = functools.partial(jnp.dot, preferred_element_type=F32)
    p1, p2, p3 = _split3(x)
    return d(p1, m_bf16) + (d(p2, m_bf16) + d(p3, m_bf16))


def _sigmoid(x):
    return jax.nn.sigmoid(x)


def _tile(n, pref):
    if n <= pref:
        return n
    t = pref - pref % SUBLANES
    while n % t:
        t -= SUBLANES
    return t


def _softplus(x):
    return jnp.maximum(x, 0.0) + jnp.log1p(jnp.exp(-jnp.abs(x)))


def _gelu_tanh(x):
    return 0.5 * x * (1.0 + jnp.tanh(0.7978845608028654 * (x + 0.044715 * (x * x * x))))


def _rms(x, g):
    r = lax.rsqrt(jnp.mean(x * x, axis=-1, keepdims=True) + NORM_EPS)
    return x * r * g


def _split_rows(a, b, tm):
    na = a.shape[0] // tm
    w = a.shape[1]
    return (pl.BlockSpec((tm, w), lambda i, *_: (jnp.minimum(i, na - 1), 0)),
            pl.BlockSpec((tm, w), lambda i, *_: (jnp.maximum(i - na, 0), 0)))


def _norm_ba_kernel(xa_ref, xb_ref, g_ref, wba_ref, h_ref, ba_ref, *, na):
    x = jnp.where(pl.program_id(0) < na, xa_ref[...], xb_ref[...])
    h = _rms(x, g_ref[...]).astype(BF16)
    h_ref[...] = h
    ba_ref[...] = jnp.dot(h, wba_ref[...], preferred_element_type=F32)


def _norm_ba(xa, xb, g, w_ba, tm):
    d = xa.shape[1]
    m = xa.shape[0] + xb.shape[0]
    nb = w_ba.shape[1]
    return pl.pallas_call(
        functools.partial(_norm_ba_kernel, na=xa.shape[0] // tm),
        grid=(m // tm,),
        in_specs=[*_split_rows(xa, xb, tm),
                  pl.BlockSpec((1, d), lambda i: (0, 0)),
                  pl.BlockSpec((d, nb), lambda i: (0, 0))],
        out_specs=[pl.BlockSpec((tm, d), lambda i: (i, 0)),
                   pl.BlockSpec((tm, nb), lambda i: (i, 0))],
        out_shape=[jax.ShapeDtypeStruct((m, d), BF16), jax.ShapeDtypeStruct((m, nb), F32)],
        compiler_params=_params("parallel"),
        name="norm_ba",
    )(xa, xb, g, w_ba)


def _mm_kernel(a_ref, w_ref, o_ref):
    o_ref[...] = jnp.dot(a_ref[...], w_ref[...], preferred_element_type=F32).astype(o_ref.dtype)


def _matmul(a, w, n_cols, tm, tn, out_dtype, name):
    m, k = a.shape
    return pl.pallas_call(
        _mm_kernel,
        grid=(m // tm, n_cols // tn),
        in_specs=[pl.BlockSpec((tm, k), lambda i, j: (i, 0)),
                  pl.BlockSpec((k, tn), lambda i, j: (0, j))],
        out_specs=pl.BlockSpec((tm, tn), lambda i, j: (i, j)),
        out_shape=jax.ShapeDtypeStruct((m, n_cols), out_dtype),
        compiler_params=_params("parallel", "parallel"),
        name=name,
    )(a, w)


def _conv4_into(xe_ref, rows, cw):
    y = xe_ref[SUBLANES:SUBLANES + rows, :] * cw[CONV_W - 1:CONV_W]
    for s in range(1, CONV_W):
        y = y + xe_ref[SUBLANES - s:SUBLANES - s + rows, :] * cw[CONV_W - 1 - s:CONV_W - s]
    return y


def _scan8(a3, b3):
    row = lax.broadcasted_iota(jnp.int32, a3.shape, 1)
    for s in (1, 2, 4):
        a_sh = pltpu.roll(a3, s, axis=1)
        b_sh = pltpu.roll(b3, s, axis=1)
        m = row >= s
        b3 = jnp.where(m, a3 * b_sh + b3, b3)
        a3 = jnp.where(m, a3 * a_sh, a3)
    return a3, b3


def _lru_gates(xc, wg_ref, bx, bga, ap, first_row_is_pos0):
    nblk = wg_ref.shape[0]
    bw = wg_ref.shape[1]
    xcb = xc.astype(BF16)
    gx, ga = [], []
    for n in range(nblk):
        r = jnp.dot(xcb[:, n * bw:(n + 1) * bw], wg_ref[n], preferred_element_type=F32)
        gx.append(r[:, :bw])
        ga.append(r[:, bw:])
    gate_x = _sigmoid(jnp.concatenate(gx, axis=1) + bx)
    gate_a = _sigmoid(jnp.concatenate(ga, axis=1) + bga)
    log_a = (-LRU_C) * gate_a * _softplus(ap)
    a = jnp.exp(log_a)
    mult = jnp.sqrt(-jnp.tanh(log_a) * (a * a + 1.0))
    if first_row_is_pos0 is not None:
        mult = jnp.where(first_row_is_pos0, 1.0, mult)
    return a, xc * gate_x * mult


def _lru_seq_kernel(xa_ref, ya_ref, tail0_ref, h0_ref, cw_ref, cb_ref, wg_ref, bx_ref, bga_ref, ap_ref,
                    y_ref, hl_ref, xe_ref, a_ref, b_ref, h_ref, *, tt, pos0):
    t = pl.program_id(1)
    w = xa_ref.shape[1]

    @pl.when(t == 0)
    def _():
        xe_ref[0:SUBLANES, :] = tail0_ref[0]
        h_ref[...] = h0_ref[0]

    xe_ref[SUBLANES:SUBLANES + tt, :] = xa_ref[...]
    xc = _conv4_into(xe_ref, tt, cw_ref[...]) + cb_ref[...]
    xe_ref[0:SUBLANES, :] = xe_ref[tt:tt + SUBLANES, :]
    first = None
    if pos0 == 0:
        first = (lax.broadcasted_iota(jnp.int32, (tt, 1), 0) + t * tt) == 0
    a, b = _lru_gates(xc, wg_ref, bx_ref[...], bga_ref[...], ap_ref[...], first)
    a3, b3 = _scan8(a.reshape(tt // SUBLANES, SUBLANES, w), b.reshape(tt // SUBLANES, SUBLANES, w))
    a_ref[...] = a3
    b_ref[...] = b3

    def body(g, h):
        hg = a_ref[g] * h + b_ref[g]
        b_ref[g] = hg
        return hg[SUBLANES - 1:SUBLANES, :]

    h = lax.fori_loop(0, tt // SUBLANES, body, h_ref[...])
    h_ref[...] = h
    hl_ref[0] = h
    hs = b_ref[...].reshape(tt, w)
    y_ref[...] = hs * _gelu_tanh(ya_ref[...])


def _lru_seq(proj, row0, batch, seq, tt, tail0, h0, cw, cb, wg, bx, bga, ap, pos0):
    w = cw.shape[1]
    nt = seq // tt
    rb0 = row0 // tt
    vec = lambda: pl.BlockSpec((1, w), lambda b, t: (0, 0))
    return pl.pallas_call(
        functools.partial(_lru_seq_kernel, tt=tt, pos0=pos0),
        grid=(batch, nt),
        in_specs=[pl.BlockSpec((tt, w), lambda b, t: (rb0 + b * nt + t, 0)),
                  pl.BlockSpec((tt, w), lambda b, t: (rb0 + b * nt + t, 1)),
                  pl.BlockSpec((1, SUBLANES, w), lambda b, t: (b, 0, 0)),
                  pl.BlockSpec((1, 1, w), lambda b, t: (b, 0, 0)),
                  pl.BlockSpec((CONV_W, w), lambda b, t: (0, 0)),
                  vec(),
                  pl.BlockSpec(wg.shape, lambda b, t: (0, 0, 0)),
                  vec(), vec(), vec()],
        out_specs=[pl.BlockSpec((tt, w), lambda b, t: (b * nt + t, 0)),
                   pl.BlockSpec((1, 1, w), lambda b, t: (b, 0, 0))],
        out_shape=[jax.ShapeDtypeStruct((batch * seq, w), F32),
                   jax.ShapeDtypeStruct((batch, 1, w), F32)],
        scratch_shapes=[pltpu.VMEM((SUBLANES + tt, w), F32),
                        pltpu.VMEM((tt // SUBLANES, SUBLANES, w), F32),
                        pltpu.VMEM((tt // SUBLANES, SUBLANES, w), F32),
                        pltpu.VMEM((1, w), F32)],
        compiler_params=_params("parallel", "arbitrary"),
        name="lru_seq",
    )(proj, proj, tail0, h0, cw, cb, wg, bx, bga, ap)


def _gdn_kernel(q_ref, k_ref, v_ref, z_ref, ba_ref, bat_ref, tail0_ref, s0_ref, cw_ref, alog_ref, dtb_ref,
                alogt_ref, dtbt_ref, ng_ref, y_ref, sout_ref, xe_ref, s_ref, *, groups, gr, c, carry, heads, dk, dv):
    ci = pl.program_id(1)
    kdim = heads * dk
    rows = groups * gr
    nseg = gr // c
    nconv, crow = xe_ref.shape[0], xe_ref.shape[1] - SUBLANES
    hs = range(heads)
    units = [(gi, h) for gi in range(groups) for h in hs]

    @pl.when(ci == 0)
    def _():
        xe_ref[:, 0:SUBLANES, :] = tail0_ref[...]
        if carry:
            s_ref[...] = s0_ref[0]

    xe_ref[:, SUBLANES:, 0:kdim] = q_ref[...].reshape(nconv, crow, kdim)
    xe_ref[:, SUBLANES:, kdim:2 * kdim] = k_ref[...].reshape(nconv, crow, kdim)
    xe_ref[:, SUBLANES:, 2 * kdim:] = v_ref[...].reshape(nconv, crow, kdim)
    cw = cw_ref[...]
    pre = xe_ref[:, SUBLANES:, :] * cw[CONV_W - 1:CONV_W]
    for s in range(1, CONV_W):
        pre = pre + xe_ref[:, SUBLANES - s:SUBLANES - s + crow, :] * cw[CONV_W - 1 - s:CONV_W - s]
    xe_ref[:, 0:SUBLANES, :] = xe_ref[:, crow:crow + SUBLANES, :]
    pre = pre.reshape(rows, 3 * kdim)
    qkv = pre * _sigmoid(pre)

    ba = ba_ref[...]
    beta = _sigmoid(ba[:, 0:heads])
    g = -jnp.exp(alog_ref[...]) * _softplus(ba[:, heads:2 * heads] + dtb_ref[...])
    ii = lax.broadcasted_iota(jnp.int32, (gr, gr), 0)
    jj = lax.broadcasted_iota(jnp.int32, (gr, gr), 1)
    same = (ii // c) == (jj // c)
    incl = jnp.logical_and(same, ii >= jj)
    strict = jnp.logical_and(same, ii > jj)
    eye = (ii == jj).astype(F32)
    incl_b = incl.astype(BF16)
    inclt_b = jnp.logical_and(same, ii <= jj).astype(BF16)
    same_b = same.astype(BF16)
    z = z_ref[...]
    ng = ng_ref[...]
    scale = dk ** -0.5

    gcs, gcts, egcs, ekds, gtots = [], [], [], [], []
    for gi in range(groups):
        gg = g[gi * gr:(gi + 1) * gr]
        gt = -jnp.exp(alogt_ref[...]) * _softplus(bat_ref[0, 0, gi][heads:2 * heads, :] + dtbt_ref[...])
        gc = _dot_exact_lhs(incl_b, gg)
        gtot = _dot_exact_lhs(same_b, gg)
        gcs.append(gc)
        gcts.append(_dot_exact_rhs(gt, inclt_b))
        egcs.append(jnp.exp(gc))
        ekds.append(jnp.exp(gtot - gc))
        gtots.append(gtot)

    k_, kb_, dec_, rhs_, qs_ = {}, {}, {}, {}, {}
    for (gi, h) in units:
        r0 = gi * gr
        qh = qkv[r0:r0 + gr, h * dk:(h + 1) * dk]
        kh = qkv[r0:r0 + gr, kdim + h * dk:kdim + (h + 1) * dk]
        vh = qkv[r0:r0 + gr, 2 * kdim + h * dv:2 * kdim + (h + 1) * dv]
        qh = qh * lax.rsqrt(jnp.sum(qh * qh, axis=-1, keepdims=True) + NORM_EPS)
        kh = kh * lax.rsqrt(jnp.sum(kh * kh, axis=-1, keepdims=True) + NORM_EPS)
        bh = beta[r0:r0 + gr, h:h + 1]
        diff = gcs[gi][:, h:h + 1] - gcts[gi][h:h + 1, :]
        u = (gi, h)
        dec_[u] = jnp.where(incl, jnp.exp(jnp.where(incl, diff, 0.0)), 0.0)
        k_[u] = kh
        kb_[u] = kh * bh
        qs_[u] = qh * scale
        rhs_[u] = jnp.concatenate([vh * bh, kb_[u] * egcs[gi][:, h:h + 1]], axis=1)
    npow = {u: jnp.where(strict, -(_bdot_nt(kb_[u], k_[u]) * dec_[u]), 0.0) for u in units}
    qk_ = {u: _bdot_nt(qs_[u], k_[u]) * dec_[u] for u in units}
    p = {u: eye + npow[u] for u in units}
    lvl = 1
    while 2 * lvl < c:
        npow = {u: _dot3(npow[u], npow[u]) for u in units}
        p = {u: p[u] + _dot3(p[u], npow[u]) for u in units}
        lvl *= 2
    sol = {u: _dot3(p[u], rhs_[u]) for u in units}

    o_ = {}
    if carry:
        assert nseg == 1
        cur = {h: s_ref[h] for h in hs}
        for gi in range(groups):
            wq = {}
            for h in hs:
                u = (gi, h)
                lhs = jnp.concatenate([sol[u][:, dv:], qs_[u] * egcs[gi][:, h:h + 1]], axis=0)
                wq[h] = _bdot(lhs, cur[h])
            for h in hs:
                u = (gi, h)
                vnew = sol[u][:, :dv] - wq[h][:gr]
                o_[u] = wq[h][gr:] + _bdot(qk_[u], vnew)
                kd = k_[u] * ekds[gi][:, h:h + 1]
                glast = jnp.exp(gtots[gi][0:1, h:h + 1])
                cur[h] = cur[h] * glast + _bdot_tn(kd, vnew)
        for h in hs:
            s_ref[h] = cur[h]
        sout_ref[0] = s_ref[...]
    else:
        segs = [(gi, j, h) for gi in range(groups) for j in range(nseg) for h in hs]
        wq = {}
        for (gi, j, h) in segs:
            u = (gi, h)
            sl = slice(j * c, (j + 1) * c)
            lhs = jnp.concatenate([sol[u][sl, dv:], qs_[u][sl] * egcs[gi][sl, h:h + 1]], axis=0)
            wq[(gi, j, h)] = _bdot(lhs, s0_ref[gi * nseg + j, h])
        vn = {}
        for (gi, j, h) in segs:
            u = (gi, h)
            sl = slice(j * c, (j + 1) * c)
            vn[(gi, j, h)] = sol[u][sl, :dv] - wq[(gi, j, h)][:c]
            kd = k_[u][sl] * ekds[gi][sl, h:h + 1]
            glast = jnp.exp(gtots[gi][j * c:j * c + 1, h:h + 1])
            sout_ref[gi * nseg + j, h] = s0_ref[gi * nseg + j, h] * glast + _bdot_tn(kd, vn[(gi, j, h)])
        for u in units:
            gi, h = u
            vnew = jnp.concatenate([vn[(gi, j, h)] for j in range(nseg)], axis=0)
            os_ = jnp.concatenate([wq[(gi, j, h)][c:] for j in range(nseg)], axis=0)
            o_[u] = os_ + _bdot(qk_[u], vnew)

    ys = []
    for gi in range(groups):
        row = []
        for h in hs:
            zz = z[gi * gr:(gi + 1) * gr, h * dv:(h + 1) * dv]
            row.append(_rms(o_[(gi, h)], ng) * (zz * _sigmoid(zz)))
        ys.append(jnp.concatenate(row, axis=1))
    y_ref[...] = jnp.concatenate(ys, axis=0)


def _gdn(proj, col0, row0, batch, seq, c, groups, gr, carry, ba, tail0, s0, cw, alog, dtb, ng, heads, dk, dv):
    kdim = heads * dk
    rows = groups * gr
    if carry:
        nb, nsteps, sb = batch, seq // rows, 1
    else:
        sb = rows // seq
        nb, nsteps = batch // sb, 1
    crow = rows // sb
    rb0 = row0 // rows
    cb0 = col0 // kdim
    rblk = lambda b, ci: rb0 + b * nsteps + ci
    small = lambda shape: pl.BlockSpec(shape, lambda b, ci: (0,) * len(shape))
    bat = ba[row0:row0 + batch * seq].reshape(nb, nsteps, groups, gr, 2 * heads)
    bat = jnp.swapaxes(bat, 3, 4)
    return pl.pallas_call(
        functools.partial(_gdn_kernel, groups=groups, gr=gr, c=c, carry=carry, heads=heads, dk=dk, dv=dv),
        grid=(nb, nsteps),
        in_specs=[pl.BlockSpec((rows, kdim), lambda b, ci: (rblk(b, ci), cb0)),
                  pl.BlockSpec((rows, kdim), lambda b, ci: (rblk(b, ci), cb0 + 1)),
                  pl.BlockSpec((rows, kdim), lambda b, ci: (rblk(b, ci), cb0 + 2)),
                  pl.BlockSpec((rows, kdim), lambda b, ci: (rblk(b, ci), cb0 + 3)),
                  pl.BlockSpec((rows, 2 * heads), lambda b, ci: (rblk(b, ci), 0)),
                  pl.BlockSpec((1, 1, groups, 2 * heads, gr), lambda b, ci: (b, ci, 0, 0, 0)),
                  pl.BlockSpec((sb, SUBLANES, 3 * kdim), lambda b, ci: (b, 0, 0)),
                  pl.BlockSpec((sb, heads, dk, dv), lambda b, ci: (b, 0, 0, 0)),
                  small((CONV_W, 3 * kdim)),
                  small((1, heads)), small((1, heads)), small((heads, 1)), small((heads, 1)),
                  small((1, dv))],
        out_specs=[pl.BlockSpec((rows, heads * dv), lambda b, ci: (b * nsteps + ci, 0)),
                   pl.BlockSpec((sb, heads, dk, dv), lambda b, ci: (b, 0, 0, 0))],
        out_shape=[jax.ShapeDtypeStruct((batch * seq, heads * dv), F32),
                   jax.ShapeDtypeStruct((batch, heads, dk, dv), F32)],
        scratch_shapes=[pltpu.VMEM((sb, SUBLANES + crow, 3 * kdim), F32),
                        pltpu.VMEM((heads, dk, dv), F32)],
        compiler_params=_params("parallel", "arbitrary"),
        name="gdn",
    )(proj, proj, proj, proj, ba, bat, tail0, s0, cw, alog, dtb, alog.T, dtb.T, ng)


def _merge_kernel(h_ref, ya1_ref, ya2_ref, yb1_ref, yb2_ref, wga_ref, wgb_ref, wa_ref, wb_ref, u_ref, *, na):
    d = functools.partial(jnp.dot, preferred_element_type=F32)
    first = pl.program_id(0) < na
    h = h_ref[...]
    ya = jnp.where(first, ya1_ref[...], ya2_ref[...]).astype(BF16)
    yb = jnp.where(first, yb1_ref[...], yb2_ref[...]).astype(BF16)
    u = _sigmoid(d(h, wga_ref[...])) * d(ya, wa_ref[...])
    u = u + _sigmoid(d(h, wgb_ref[...])) * d(yb, wb_ref[...])
    u_ref[...] = u.astype(BF16)


def _merge(h, ya1, ya2, yb1, yb2, w_g, w_a, w_b, tm, tn):
    m, d = h.shape
    n = w_a.shape[1]
    nj = n // tn
    return pl.pallas_call(
        functools.partial(_merge_kernel, na=ya1.shape[0] // tm),
        grid=(m // tm, nj),
        in_specs=[pl.BlockSpec((tm, d), lambda i, j: (i, 0)),
                  *_split_rows(ya1, ya2, tm),
                  *_split_rows(yb1, yb2, tm),
                  pl.BlockSpec((d, tn), lambda i, j: (0, j)),
                  pl.BlockSpec((d, tn), lambda i, j: (0, nj + j)),
                  pl.BlockSpec((w_a.shape[0], tn), lambda i, j: (0, j)),
                  pl.BlockSpec((w_b.shape[0], tn), lambda i, j: (0, j))],
        out_specs=pl.BlockSpec((tm, tn), lambda i, j: (i, j)),
        out_shape=jax.ShapeDtypeStruct((m, n), BF16),
        compiler_params=_params("parallel", "parallel"),
        name="merge",
    )(h, ya1, ya2, yb1, yb2, w_g, w_g, w_a, w_b)


def _out_norm_kernel(u_ref, w_ref, xa_ref, xb_ref, g_ref, x1_ref, h_ref, *, na):
    x = jnp.where(pl.program_id(0) < na, xa_ref[...], xb_ref[...])
    x1 = x + jnp.dot(u_ref[...], w_ref[...], preferred_element_type=F32)
    x1_ref[...] = x1
    h_ref[...] = _rms(x1, g_ref[...]).astype(BF16)


def _out_norm(u, w, xa, xb, g, tm):
    m, k = u.shape
    d = xa.shape[1]
    return pl.pallas_call(
        functools.partial(_out_norm_kernel, na=xa.shape[0] // tm),
        grid=(m // tm,),
        in_specs=[pl.BlockSpec((tm, k), lambda i: (i, 0)),
                  pl.BlockSpec((k, d), lambda i: (0, 0)),
                  *_split_rows(xa, xb, tm),
                  pl.BlockSpec((1, d), lambda i: (0, 0))],
        out_specs=[pl.BlockSpec((tm, d), lambda i: (i, 0)),
                   pl.BlockSpec((tm, d), lambda i: (i, 0))],
        out_shape=[jax.ShapeDtypeStruct((m, d), F32), jax.ShapeDtypeStruct((m, d), BF16)],
        compiler_params=_params("parallel"),
        name="out_norm",
    )(u, w, xa, xb, g)


def _memkv_kernel(m_ref, g_ref, wk_ref, wv_ref, k_ref, v_ref):
    mn = _rms(m_ref[...], g_ref[...]).astype(BF16)
    k_ref[...] = jnp.dot(mn, wk_ref[...], preferred_element_type=F32)
    v_ref[...] = jnp.dot(mn, wv_ref[...], preferred_element_type=F32)


def _memkv(mem, g, wk, wv, tm):
    m, d = mem.shape
    n = wk.shape[1]
    return pl.pallas_call(
        _memkv_kernel,
        grid=(m // tm,),
        in_specs=[pl.BlockSpec((tm, d), lambda i: (i, 0)),
                  pl.BlockSpec((1, d), lambda i: (0, 0)),
                  pl.BlockSpec((d, n), lambda i: (0, 0)),
                  pl.BlockSpec((d, n), lambda i: (0, 0))],
        out_specs=[pl.BlockSpec((tm, n), lambda i: (i, 0)), pl.BlockSpec((tm, n), lambda i: (i, 0))],
        out_shape=[jax.ShapeDtypeStruct((m, n), F32), jax.ShapeDtypeStruct((m, n), F32)],
        compiler_params=_params("parallel"),
        name="memkv",
    )(mem, g, wk, wv)


def _attn_kernel(q_ref, k_ref, v_ref, o_ref, *, nb, tq, heads, hd, interleaved):
    mem = k_ref.shape[1] // heads if interleaved else k_ref.shape[1]
    scale = hd ** -0.5
    units = [(b, h) for b in range(nb) for h in range(heads)]

    def head(ref, b, h):
        if interleaved:
            return ref[b, pl.ds(h, mem, stride=heads), :].astype(BF16)
        return ref[b, :, h * hd:(h + 1) * hd].astype(BF16)

    q = q_ref[...].astype(BF16)
    s = {(b, h): lax.dot_general(q[b * tq:(b + 1) * tq, h * hd:(h + 1) * hd], head(k_ref, b, h),
                                 (((1,), (1,)), ((), ())), preferred_element_type=F32) * scale for (b, h) in units}
    p = {u: jnp.exp(s[u] - jnp.max(s[u], axis=-1, keepdims=True)) for u in units}
    o = {(b, h): jnp.dot(p[(b, h)].astype(BF16), head(v_ref, b, h), preferred_element_type=F32)
         / jnp.sum(p[(b, h)], axis=-1, keepdims=True) for (b, h) in units}
    o_ref[...] = jnp.concatenate([jnp.concatenate([o[(b, h)] for h in range(heads)], axis=1) for b in range(nb)],
                                 axis=0)


def _attn(q, row0, batch, seq, nb, tq, k, v, heads, hd, interleaved):
    nt = seq // tq
    rows = nb * tq
    rb0 = row0 // rows
    n = heads * hd
    return pl.pallas_call(
        functools.partial(_attn_kernel, nb=nb, tq=tq, heads=heads, hd=hd, interleaved=interleaved),
        grid=(batch // nb, nt),
        in_specs=[pl.BlockSpec((rows, n), lambda b, t: (rb0 + b * nt + t, 0)),
                  pl.BlockSpec((nb,) + k.shape[1:], lambda b, t: (b, 0, 0)),
                  pl.BlockSpec((nb,) + v.shape[1:], lambda b, t: (b, 0, 0))],
        out_specs=pl.BlockSpec((rows, n), lambda b, t: (b * nt + t, 0)),
        out_shape=jax.ShapeDtypeStruct((batch * seq, n), F32),
        compiler_params=_params("parallel", "parallel"),
        name="attn",
    )(q, k, v)


def _xo_router_kernel(oa_ref, ob_ref, wo_ref, x1_ref, g_ref, rw_ref, rb_ref, x2_ref, h_ref, lg_ref, *, na):
    o = jnp.where(pl.program_id(0) < na, oa_ref[...], ob_ref[...]).astype(BF16)
    x2 = x1_ref[...] + jnp.dot(o, wo_ref[...], preferred_element_type=F32)
    x2_ref[...] = x2
    h = _rms(x2, g_ref[...])
    h_ref[...] = h
    lg_ref[...] = _dot3(h, rw_ref[...]) + rb_ref[...]


def _xo_router(oa, ob, wo, x1, g, rw, rb, tm):
    m, d = x1.shape
    ne = rw.shape[1]
    row = lambda w: pl.BlockSpec((tm, w), lambda i: (i, 0))
    full = lambda a: pl.BlockSpec(a.shape, lambda i: (0, 0))
    return pl.pallas_call(
        functools.partial(_xo_router_kernel, na=oa.shape[0] // tm),
        grid=(m // tm,),
        in_specs=[*_split_rows(oa, ob, tm), full(wo), row(d), full(g), full(rw), full(rb)],
        out_specs=[row(d), row(d), row(ne)],
        out_shape=[jax.ShapeDtypeStruct((m, d), F32), jax.ShapeDtypeStruct((m, d), F32),
                   jax.ShapeDtypeStruct((m, ne), F32)],
        compiler_params=_params("parallel"),
        name="xo_router",
    )(oa, ob, wo, x1, g, rw, rb)


def _moe_kernel(e_ref, r0_ref, n_ref, code_ref, h_hbm, wg_ref, wu_ref, wd_ref, bg_ref, bu_ref, bd_ref,
                slots_hbm, xbuf, acc, wgb, wub, wdb, gsem, ssem, *, sub, nf):
    s = pl.program_id(0)
    f = pl.program_id(1)
    n = n_ref[s]
    r0 = r0_ref[s]

    @pl.when(jnp.logical_and(s == 0, f == 0))
    def _():
        xbuf[...] = jnp.zeros_like(xbuf)

    n_tok = h_hbm.shape[0]

    def row_in(i):
        tok = lax.shift_right_logical(code_ref[r0 + i], TOP_K_SHIFT)
        return pltpu.make_async_copy(h_hbm.at[pl.ds(tok, 1), :], xbuf.at[pl.ds(i, 1), :], gsem)

    def row_out(i):
        code = code_ref[r0 + i]
        dst = (code & (TOP_K - 1)) * n_tok + lax.shift_right_logical(code, TOP_K_SHIFT)
        return pltpu.make_async_copy(acc.at[pl.ds(i, 1), :], slots_hbm.at[pl.ds(dst, 1), :], ssem)

    def for_rows(fn):
        nfull = n // DMA_UNROLL

        def chunk(i, c):
            for k in range(DMA_UNROLL):
                fn(i * DMA_UNROLL + k)
            return c

        lax.fori_loop(0, nfull, chunk, 0)
        lax.fori_loop(nfull * DMA_UNROLL, n, lambda i, c: (fn(i), c)[1], 0)

    @pl.when(n > 0)
    def _active():
        @pl.when(f == 0)
        def _gather():
            for_rows(lambda i: row_in(i).start())
            for_rows(lambda i: row_in(i).wait())

        wgb[...] = wg_ref[0].astype(BF16)
        wub[...] = wu_ref[0].astype(BF16)
        wdb[...] = wd_ref[0].astype(BF16)
        bg = bg_ref[0]
        bu = bu_ref[0]
        bd = bd_ref[0]

        def sub_block(j, c):
            off = pl.multiple_of(j * sub, sub)
            x = xbuf[pl.ds(off, sub), :].astype(BF16)
            gate = jnp.dot(x, wgb[...], preferred_element_type=F32) + bg
            up = jnp.dot(x, wub[...], preferred_element_type=F32) + bu
            gate = jnp.minimum(gate, SWIGLU_LIMIT)
            up = jnp.clip(up, -SWIGLU_LIMIT, SWIGLU_LIMIT)
            act = gate * _sigmoid(SWIGLU_ALPHA * gate) * (up + 1.0)
            part = jnp.dot(act.astype(BF16), wdb[...], preferred_element_type=F32)

            @pl.when(f == 0)
            def _():
                acc[pl.ds(off, sub), :] = part + bd

            @pl.when(f > 0)
            def _():
                acc[pl.ds(off, sub), :] += part

            return c

        lax.fori_loop(0, (n + sub - 1) // sub, sub_block, 0)

        @pl.when(f == nf - 1)
        def _scatter():
            for_rows(lambda i: row_out(i).start())
            for_rows(lambda i: row_out(i).wait())


def _moe(h, sb_e, sb_r0, sb_n, codes, wg, wu, wd, bg, bu, bd, cap, sub, tf):
    n_tok, d = h.shape
    ne, _, dff = wg.shape
    nf = dff // tf
    g = sb_e.shape[0]
    fi = lambda s, f, n_ref: jnp.where(n_ref[s] > 0, f, nf - 1)
    grid_spec = pltpu.PrefetchScalarGridSpec(
        num_scalar_prefetch=4,
        grid=(g, nf),
        in_specs=[pl.BlockSpec(memory_space=pl.ANY),
                  pl.BlockSpec((1, d, tf), lambda s, f, e, r, n, c: (e[s], 0, fi(s, f, n))),
                  pl.BlockSpec((1, d, tf), lambda s, f, e, r, n, c: (e[s], 0, fi(s, f, n))),
                  pl.BlockSpec((1, tf, d), lambda s, f, e, r, n, c: (e[s], fi(s, f, n), 0)),
                  pl.BlockSpec((1, 1, tf), lambda s, f, e, r, n, c: (e[s], 0, fi(s, f, n))),
                  pl.BlockSpec((1, 1, tf), lambda s, f, e, r, n, c: (e[s], 0, fi(s, f, n))),
                  pl.BlockSpec((1, 1, d), lambda s, f, e, r, n, c: (e[s], 0, 0))],
        out_specs=pl.BlockSpec(memory_space=pl.ANY),
        scratch_shapes=[pltpu.VMEM((cap, d), F32), pltpu.VMEM((cap, d), F32),
                        pltpu.VMEM((d, tf), BF16), pltpu.VMEM((d, tf), BF16), pltpu.VMEM((tf, d), BF16),
                        pltpu.SemaphoreType.DMA(()), pltpu.SemaphoreType.DMA(())],
    )
    return pl.pallas_call(
        functools.partial(_moe_kernel, sub=sub, nf=nf),
        grid_spec=grid_spec,
        out_shape=jax.ShapeDtypeStruct((n_tok * TOP_K, d), F32),
        compiler_params=_params("arbitrary", "arbitrary"),
        name="moe",
    )(sb_e, sb_r0, sb_n, codes, h, wg, wu, wd, bg.reshape(ne, 1, dff), bu.reshape(ne, 1, dff),
      bd.reshape(ne, 1, d))


def _route(logits, cap, n_sb_max):
    n_tok, ne = logits.shape
    top_val, top_idx = lax.top_k(logits, TOP_K)
    gates = jax.nn.softmax(top_val, axis=-1)
    e_flat = top_idx.reshape(-1).astype(jnp.int32)
    codes = jnp.argsort(e_flat, stable=True).astype(jnp.int32)
    counts = jnp.zeros((ne,), jnp.int32).at[e_flat].add(1)
    start = jnp.cumsum(counts) - counts
    n_sb = (counts + cap - 1) // cap
    sb_end = jnp.cumsum(n_sb)
    sb_start = sb_end - n_sb
    total = sb_end[-1]
    s = jnp.arange(n_sb_max, dtype=jnp.int32)
    e_s = jnp.minimum(jnp.searchsorted(sb_end, s, side='right'), ne - 1).astype(jnp.int32)
    e_last = e_s[jnp.maximum(total - 1, 0)]
    active = s < total
    e_s = jnp.where(active, e_s, e_last)
    within = s - sb_start[e_s]
    r0 = jnp.where(active, start[e_s] + within * cap, 0).astype(jnp.int32)
    n = jnp.where(active, jnp.clip(counts[e_s] - within * cap, 0, cap), 0).astype(jnp.int32)
    return gates, codes, e_s, r0, n


def _final_kernel(x2_ref, s0_ref, s1_ref, s2_ref, s3_ref, gates_ref, g_ref, y_ref):
    x = x2_ref[...]
    gt = gates_ref[...]
    for k, s_ref in enumerate((s0_ref, s1_ref, s2_ref, s3_ref)):
        x = x + s_ref[...] * gt[:, k:k + 1]
    y_ref[...] = _rms(x, g_ref[...])


def _final(x2, slots, gates, g, row0, rows, tm):
    n_tok, d = x2.shape
    rb0 = row0 // tm
    nblk = n_tok // tm
    slot = lambda k: pl.BlockSpec((tm, d), lambda i: (k * nblk + rb0 + i, 0))
    return pl.pallas_call(
        _final_kernel,
        grid=(rows // tm,),
        in_specs=[pl.BlockSpec((tm, d), lambda i: (rb0 + i, 0)),
                  slot(0), slot(1), slot(2), slot(3),
                  pl.BlockSpec((tm, TOP_K), lambda i: (rb0 + i, 0)),
                  pl.BlockSpec((1, d), lambda i: (0, 0))],
        out_specs=pl.BlockSpec((tm, d), lambda i: (i, 0)),
        out_shape=jax.ShapeDtypeStruct((rows, d), F32),
        compiler_params=_params("parallel"),
        name="final",
    )(x2, slots, slots, slots, slots, gates, g)


def _pad_tail(buf):
    return jnp.pad(buf, ((0, 0), (SUBLANES - (CONV_W - 1), 0), (0, 0)))


def kernel(x_prompt, x_sample, mem_prompt, state_lru_conv, state_lru_h, state_gdn_conv, state_gdn_S, cache_mem_k, cache_mem_v, norm_mix_g, w_in, lru_conv_w, lru_conv_b, lru_wx, lru_bx, lru_wa, lru_ba, lru_a_param, gdn_conv_w, gdn_A_log, gdn_dt_bias, gdn_norm_g, w_branch_a, w_branch_b, w_out, norm_xa_g, norm_mem_g, xa_wq, xa_wk, xa_wv, xa_wo, norm_moe_g, router_w, router_b, moe_w_gate, moe_b_gate, moe_w_up, moe_b_up, moe_w_down, moe_b_down, norm_final_g):
    depth = w_in.shape[0]
    assert depth == 1
    bp, tp, d = x_prompt.shape
    bs, ts, _ = x_sample.shape
    np_, ns = bp * tp, bs * ts
    n_tok = np_ + ns
    lw = lru_conv_w.shape[2]
    heads, dk, dv = state_gdn_S.shape[2:]
    kdim = heads * dk
    mem_len = mem_prompt.shape[1]
    xa_heads, xa_hd = cache_mem_k.shape[3:]
    xa_dim = xa_heads * xa_hd
    ne = router_w.shape[2]
    past_len = 16384
    gdn_chunk = 64

    row = lambda v: v.reshape(1, -1)
    xp = x_prompt.reshape(np_, d)
    xs = x_sample.reshape(ns, d)

    c_main = 2 * lw + 3 * kdim + heads * dv
    w_in0 = w_in[0]
    w_main = w_in0[:, :c_main].astype(BF16)
    w_ba = w_in0[:, c_main:c_main + 2 * heads].astype(BF16)
    w_g = w_in0[:, c_main + 2 * heads:].astype(BF16)
    w_gates = jnp.concatenate([lru_wx[0], lru_wa[0]], axis=-1).astype(BF16)

    rows_gcd = math.gcd(np_, ns)
    tm_s = _tile(rows_gcd, 256)
    tm_m = _tile(rows_gcd, 512)
    tm_l = _tile(rows_gcd, 1024)
    h1, ba = _norm_ba(xp, xs, row(norm_mix_g[0]), w_ba, tm_m)
    proj = _matmul(h1, w_main, c_main, tm_l, _tile(c_main, 1024), F32, "in_proj")

    lru_args = (lru_conv_w[0], row(lru_conv_b[0]), w_gates, row(lru_bx[0]), row(lru_ba[0]), row(lru_a_param[0]))
    ya_p, lru_h_p = _lru_seq(proj, 0, bp, tp, _tile(tp, 256), jnp.zeros((bp, SUBLANES, lw), F32),
                             jnp.zeros((bp, 1, lw), F32), *lru_args, pos0=0)
    ya_s, lru_h_s = _lru_seq(proj, np_, bs, ts, ts, _pad_tail(state_lru_conv[0]), state_lru_h[0][:, None, :],
                             *lru_args, pos0=past_len)

    gdn_args = (gdn_conv_w[0], row(gdn_A_log[0]), row(gdn_dt_bias[0]), row(gdn_norm_g[0]), heads, dk, dv)
    cp = min(gdn_chunk, tp)
    assert tp % cp == 0 and ts <= gdn_chunk
    gp = 2 if (tp // cp) % 2 == 0 else 1
    yb_p, gdn_s_p = _gdn(proj, 2 * lw, 0, bp, tp, cp, gp, cp, True, ba,
                         jnp.zeros((bp, SUBLANES, 3 * kdim), F32), jnp.zeros((bp, heads, dk, dv), F32), *gdn_args)
    gr_s = ts * math.gcd(bs, max(gdn_chunk // ts, 1))
    gs = 2 if (ns // gr_s) % 2 == 0 else 1
    yb_s, gdn_s_s = _gdn(proj, 2 * lw, np_, bs, ts, ts, gs, gr_s, False, ba,
                         _pad_tail(state_gdn_conv[0]), state_gdn_S[0], *gdn_args)

    u = _merge(h1, ya_p, ya_s, yb_p, yb_s, w_g, w_branch_a[0].astype(BF16), w_branch_b[0].astype(BF16), tm_m,
               _tile(d, 512))
    x1, h2 = _out_norm(u, w_out[0].astype(BF16), xp, xs, row(norm_xa_g[0]), tm_m)

    k_p, v_p = _memkv(mem_prompt.reshape(bp * mem_len, d), row(norm_mem_g[0]), xa_wk[0].astype(BF16),
                      xa_wv[0].astype(BF16), _tile(bp * mem_len, 256))
    q = _matmul(h2, xa_wq[0].astype(BF16), xa_dim, tm_l, xa_dim, F32, "xa_q")
    o_p = _attn(q, 0, bp, tp, 1, _tile(tp, 512), k_p.reshape(bp, mem_len, xa_dim), v_p.reshape(bp, mem_len, xa_dim),
                xa_heads, xa_hd, False)
    nb_s = math.gcd(bs, 4)
    o_s = _attn(q, np_, bs, ts, nb_s, ts, cache_mem_k[0].reshape(bs, mem_len * xa_heads, xa_hd),
                cache_mem_v[0].reshape(bs, mem_len * xa_heads, xa_hd), xa_heads, xa_hd, True)
    x2, h3, logits = _xo_router(o_p, o_s, xa_wo[0].astype(BF16), x1, row(norm_moe_g[0]), router_w[0],
                                row(router_b[0]), tm_m)

    cap, sub, tf = 1536, 256, _tile(moe_w_gate.shape[3], 256)
    n_sb_max = (n_tok * TOP_K + ne * (cap - 1)) // cap
    gates, codes, sb_e, sb_r0, sb_n = _route(logits, cap, n_sb_max)
    slots = _moe(h3, sb_e, sb_r0, sb_n, codes, moe_w_gate[0], moe_w_up[0], moe_w_down[0],
                 moe_b_gate[0], moe_b_up[0], moe_b_down[0], cap, sub, tf)
    y_p =_final(x2, slots, gates, row(norm_final_g), 0, np_, tm_s)
    y_s = _final(x2, slots, gates, row(norm_final_g), np_, ns, tm_s)

    pp = proj[:np_].reshape(bp, tp, c_main)
    ps = proj[np_:].reshape(bs, ts, c_main)
    keep = CONV_W - 1
    return (y_p.reshape(bp, tp, d), y_s.reshape(bs, ts, d),
            pp[:, tp - keep:, :lw][None], lru_h_p.reshape(1, bp, lw),
            pp[:, tp - keep:, 2 * lw:2 * lw + 3 * kdim][None], gdn_s_p[None],
            k_p.reshape(1, bp, mem_len, xa_heads, xa_hd), v_p.reshape(1, bp, mem_len, xa_heads, xa_hd),
            ps[:, ts - keep:, :lw][None], lru_h_s.reshape(1, bs, lw),
            ps[:, ts - keep:, 2 * lw:2 * lw + 3 * kdim][None], gdn_s_s[None])
```

```python
import functools
import math

import jax
import jax.numpy as jnp
from jax import lax
from jax.experimental import pallas as pl
from jax.experimental.pallas import tpu as pltpu

F32 = jnp.float32
BF16 = jnp.bfloat16
NORM_EPS = 1e-6
LRU_C = 8.0
CONV_W = 4
SWIGLU_ALPHA = 1.702
SWIGLU_LIMIT = 7.0
TOP_K = 4
TOP_K_SHIFT = 2
DMA_UNROLL = 8
VMEM_LIMIT_BYTES = 56 * 1024 * 1024
SUBLANES = 8
LANES = 128


def _params(*sem):
    return pltpu.CompilerParams(dimension_semantics=sem, vmem_limit_bytes=VMEM_LIMIT_BYTES)


def _bdot(a, b):
    return jnp.dot(a.astype(BF16), b.astype(BF16), preferred_element_type=F32)


def _bdot_nt(a, b):
    return lax.dot_general(a.astype(BF16), b.astype(BF16), (((1,), (1,)), ((), ())), preferred_element_type=F32)


def _bdot_tn(a, b):
    return lax.dot_general(a.astype(BF16), b.astype(BF16), (((0,), (0,)), ((), ())), preferred_element_type=F32)


def _split2(x):
    hi = x.astype(BF16)
    lo = (x - hi.astype(F32)).astype(BF16)
    return hi, lo


def _split3(x):
    p1 = x.astype(BF16)
    r1 = x - p1.astype(F32)
    p2 = r1.astype(BF16)
    p3 = (r1 - p2.astype(F32)).astype(BF16)
    return p1, p2, p3


def _dot3(a, b):
    ah, al = _split2(a)
    bh, bl = _split2(b)
    d = functools.partial(jnp.dot, preferred_element_type=F32)
    return d(ah, bh) + (d(ah, bl) + d(al, bh))


def _dot_exact_lhs(m_bf16, x):
    d = functools.partial(jnp.dot, preferred_element_type=F32)
    p1, p2, p3 = _split3(x)
    return d(m_bf16, p1) + (d(m_bf16, p2) + d(m_bf16, p3))


def _dot_exact_rhs(x, m_bf16):
    d = functools.partial(jnp.dot, preferred_element_type=F32)
    p1, p2, p3 = _split3(x)
    return d(p1, m_bf16) + (d(p2, m_bf16) + d(p3, m_bf16))


def _sigmoid(x):
    return jax.nn.sigmoid(x)


def _tile(n, pref):
    if n <= pref:
        return n
    t = pref - pref % SUBLANES
    while n % t:
        t -= SUBLANES
    return t


def _softplus(x):
    return jnp.maximum(x, 0.0) + jnp.log1p(jnp.exp(-jnp.abs(x)))


def _gelu_tanh(x):
    return 0.5 * x * (1.0 + jnp.tanh(0.7978845608028654 * (x + 0.044715 * (x * x * x))))


def _rms(x, g):
    r = lax.rsqrt(jnp.mean(x * x, axis=-1, keepdims=True) + NORM_EPS)
    return x * r * g


def _split_rows(a, b, tm):
    na = a.shape[0] // tm
    w = a.shape[1]
    return (pl.BlockSpec((tm, w), lambda i, *_: (jnp.minimum(i, na - 1), 0)),
            pl.BlockSpec((tm, w), lambda i, *_: (jnp.maximum(i - na, 0), 0)))


def _norm_ba_kernel(xa_ref, xb_ref, g_ref, wba_ref, h_ref, ba_ref, *, na):
    x = jnp.where(pl.program_id(0) < na, xa_ref[...], xb_ref[...])
    h = _rms(x, g_ref[...]).astype(BF16)
    h_ref[...] = h
    ba_ref[...] = jnp.dot(h, wba_ref[...], preferred_element_type=F32)


def _norm_ba(xa, xb, g, w_ba, tm):
    d = xa.shape[1]
    m = xa.shape[0] + xb.shape[0]
    nb = w_ba.shape[1]
    return pl.pallas_call(
        functools.partial(_norm_ba_kernel, na=xa.shape[0] // tm),
        grid=(m // tm,),
        in_specs=[*_split_rows(xa, xb, tm),
                  pl.BlockSpec((1, d), lambda i: (0, 0)),
                  pl.BlockSpec((d, nb), lambda i: (0, 0))],
        out_specs=[pl.BlockSpec((tm, d), lambda i: (i, 0)),
                   pl.BlockSpec((tm, nb), lambda i: (i, 0))],
        out_shape=[jax.ShapeDtypeStruct((m, d), BF16), jax.ShapeDtypeStruct((m, nb), F32)],
        compiler_params=_params("parallel"),
        name="norm_ba",
    )(xa, xb, g, w_ba)


def _mm_kernel(a_ref, w_ref, o_ref):
    o_ref[...] = jnp.dot(a_ref[...], w_ref[...], preferred_element_type=F32).astype(o_ref.dtype)


def _matmul(a, w, n_cols, tm, tn, out_dtype, name):
    m, k = a.shape
    return pl.pallas_call(
        _mm_kernel,
        grid=(m // tm, n_cols // tn),
        in_specs=[pl.BlockSpec((tm, k), lambda i, j: (i, 0)),
                  pl.BlockSpec((k, tn), lambda i, j: (0, j))],
        out_specs=pl.BlockSpec((tm, tn), lambda i, j: (i, j)),
        out_shape=jax.ShapeDtypeStruct((m, n_cols), out_dtype),
        compiler_params=_params("parallel", "parallel"),
        name=name,
    )(a, w)


def _conv4_into(xe_ref, rows, cw):
    y = xe_ref[SUBLANES:SUBLANES + rows, :] * cw[CONV_W - 1:CONV_W]
    for s in range(1, CONV_W):
        y = y + xe_ref[SUBLANES - s:SUBLANES - s + rows, :] * cw[CONV_W - 1 - s:CONV_W - s]
    return y


def _scan8(a3, b3):
    row = lax.broadcasted_iota(jnp.int32, a3.shape, 1)
    for s in (1, 2, 4):
        a_sh = pltpu.roll(a3, s, axis=1)
        b_sh = pltpu.roll(b3, s, axis=1)
        m = row >= s
        b3 = jnp.where(m, a3 * b_sh + b3, b3)
        a3 = jnp.where(m, a3 * a_sh, a3)
    return a3, b3


def _lru_gates(xc, wg_ref, bx, bga, ap, first_row_is_pos0):
    nblk = wg_ref.shape[0]
    bw = wg_ref.shape[1]
    xcb = xc.astype(BF16)
    gx, ga = [], []
    for n in range(nblk):
        r = jnp.dot(xcb[:, n * bw:(n + 1) * bw], wg_ref[n], preferred_element_type=F32)
        gx.append(r[:, :bw])
        ga.append(r[:, bw:])
    gate_x = _sigmoid(jnp.concatenate(gx, axis=1) + bx)
    gate_a = _sigmoid(jnp.concatenate(ga, axis=1) + bga)
    log_a = (-LRU_C) * gate_a * _softplus(ap)
    a = jnp.exp(log_a)
    mult = jnp.sqrt(-jnp.tanh(log_a) * (a * a + 1.0))
    if first_row_is_pos0 is not None:
        mult = jnp.where(first_row_is_pos0, 1.0, mult)
    return a, xc * gate_x * mult


def _lru_seq_kernel(xa_ref, ya_ref, tail0_ref, h0_ref, cw_ref, cb_ref, wg_ref, bx_ref, bga_ref, ap_ref,
                    y_ref, hl_ref, xe_ref, a_ref, b_ref, h_ref, *, tt, pos0):
    t = pl.program_id(1)
    w = xa_ref.shape[1]

    @pl.when(t == 0)
    def _():
        xe_ref[0:SUBLANES, :] = tail0_ref[0]
        h_ref[...] = h0_ref[0]

    xe_ref[SUBLANES:SUBLANES + tt, :] = xa_ref[...]
    xc = _conv4_into(xe_ref, tt, cw_ref[...]) + cb_ref[...]
    xe_ref[0:SUBLANES, :] = xe_ref[tt:tt + SUBLANES, :]
    first = None
    if pos0 == 0:
        first = (lax.broadcasted_iota(jnp.int32, (tt, 1), 0) + t * tt) == 0
    a, b = _lru_gates(xc, wg_ref, bx_ref[...], bga_ref[...], ap_ref[...], first)
    a3, b3 = _scan8(a.reshape(tt // SUBLANES, SUBLANES, w), b.reshape(tt // SUBLANES, SUBLANES, w))
    a_ref[...] = a3
    b_ref[...] = b3

    def body(g, h):
        hg = a_ref[g] * h + b_ref[g]
        b_ref[g] = hg
        return hg[SUBLANES - 1:SUBLANES, :]

    h = lax.fori_loop(0, tt // SUBLANES, body, h_ref[...])
    h_ref[...] = h
    hl_ref[0] = h
    hs = b_ref[...].reshape(tt, w)
    y_ref[...] = hs * _gelu_tanh(ya_ref[...])


def _lru_seq(proj, row0, batch, seq, tt, tail0, h0, cw, cb, wg, bx, bga, ap, pos0):
    w = cw.shape[1]
    nt = seq // tt
    rb0 = row0 // tt
    vec = lambda: pl.BlockSpec((1, w), lambda b, t: (0, 0))
    return pl.pallas_call(
        functools.partial(_lru_seq_kernel, tt=tt, pos0=pos0),
        grid=(batch, nt),
        in_specs=[pl.BlockSpec((tt, w), lambda b, t: (rb0 + b * nt + t, 0)),
                  pl.BlockSpec((tt, w), lambda b, t: (rb0 + b * nt + t, 1)),
                  pl.BlockSpec((1, SUBLANES, w), lambda b, t: (b, 0, 0)),
                  pl.BlockSpec((1, 1, w), lambda b, t: (b, 0, 0)),
                  pl.BlockSpec((CONV_W, w), lambda b, t: (0, 0)),
                  vec(),
                  pl.BlockSpec(wg.shape, lambda b, t: (0, 0, 0)),
                  vec(), vec(), vec()],
        out_specs=[pl.BlockSpec((tt, w), lambda b, t: (b * nt + t, 0)),
                   pl.BlockSpec((1, 1, w), lambda b, t: (b, 0, 0))],
        out_shape=[jax.ShapeDtypeStruct((batch * seq, w), F32),
                   jax.ShapeDtypeStruct((batch, 1, w), F32)],
        scratch_shapes=[pltpu.VMEM((SUBLANES + tt, w), F32),
                        pltpu.VMEM((tt // SUBLANES, SUBLANES, w), F32),
                        pltpu.VMEM((tt // SUBLANES, SUBLANES, w), F32),
                        pltpu.VMEM((1, w), F32)],
        compiler_params=_params("parallel", "arbitrary"),
        name="lru_seq",
    )(proj, proj, tail0, h0, cw, cb, wg, bx, bga, ap)


def _lru_short_kernel(xa_ref, ya_ref, tail0_ref, h0_ref, cw_ref, cb_ref, wg_ref, bx_ref, bga_ref, ap_ref,
                      y_ref, hl_ref, xe_ref):
    nb = xe_ref.shape[0]
    w = xa_ref.shape[1]
    xe_ref[:, 0:SUBLANES, :] = tail0_ref[...]
    xe_ref[:, SUBLANES:, :] = xa_ref[...].reshape(nb, SUBLANES, w)
    cw = cw_ref[...]
    xc = xe_ref[:, SUBLANES:, :] * cw[CONV_W - 1:CONV_W]
    for s in range(1, CONV_W):
        xc = xc + xe_ref[:, SUBLANES - s:2 * SUBLANES - s, :] * cw[CONV_W - 1 - s:CONV_W - s]
    xc = xc.reshape(nb * SUBLANES, w) + cb_ref[...]
    a, b = _lru_gates(xc, wg_ref, bx_ref[...], bga_ref[...], ap_ref[...], None)
    a3, b3 = _scan8(a.reshape(nb, SUBLANES, w), b.reshape(nb, SUBLANES, w))
    hs = a3 * h0_ref[...] + b3
    hl_ref[...] = hs[:, SUBLANES - 1:SUBLANES, :]
    y_ref[...] = hs.reshape(nb * SUBLANES, w) * _gelu_tanh(ya_ref[...])


def _lru_short(proj, row0, batch, nb, tail0, h0, cw, cb, wg, bx, bga, ap):
    w = cw.shape[1]
    rows = nb * SUBLANES
    rb0 = row0 // rows
    vec = lambda: pl.BlockSpec((1, w), lambda b: (0, 0))
    return pl.pallas_call(
        _lru_short_kernel,
        grid=(batch // nb,),
        in_specs=[pl.BlockSpec((rows, w), lambda b: (rb0 + b, 0)),
                  pl.BlockSpec((rows, w), lambda b: (rb0 + b, 1)),
                  pl.BlockSpec((nb, SUBLANES, w), lambda b: (b, 0, 0)),
                  pl.BlockSpec((nb, 1, w), lambda b: (b, 0, 0)),
                  pl.BlockSpec((CONV_W, w), lambda b: (0, 0)),
                  vec(),
                  pl.BlockSpec(wg.shape, lambda b: (0, 0, 0)),
                  vec(), vec(), vec()],
        out_specs=[pl.BlockSpec((rows, w), lambda b: (b, 0)),
                   pl.BlockSpec((nb, 1, w), lambda b: (b, 0, 0))],
        out_shape=[jax.ShapeDtypeStruct((batch * SUBLANES, w), F32),
                   jax.ShapeDtypeStruct((batch, 1, w), F32)],
        scratch_shapes=[pltpu.VMEM((nb, 2 * SUBLANES, w), F32)],
        compiler_params=_params("parallel"),
        name="lru_short",
    )(proj, proj, tail0, h0, cw, cb, wg, bx, bga, ap)


def _gdn_kernel(q_ref, k_ref, v_ref, z_ref, ba_ref, bat_ref, tail0_ref, s0_ref, cw_ref, alog_ref, dtb_ref,
                alogt_ref, dtbt_ref, ng_ref, y_ref, sout_ref, xe_ref, s_ref, *, groups, gr, c, carry, heads, dk, dv):
    ci = pl.program_id(1)
    kdim = heads * dk
    rows = groups * gr
    nseg = gr // c
    nconv, crow = xe_ref.shape[0], xe_ref.shape[1] - SUBLANES
    hs = range(heads)
    units = [(gi, h) for gi in range(groups) for h in hs]

    @pl.when(ci == 0)
    def _():
        xe_ref[:, 0:SUBLANES, :] = tail0_ref[...]
        if carry:
            s_ref[...] = s0_ref[0]

    xe_ref[:, SUBLANES:, 0:kdim] = q_ref[...].reshape(nconv, crow, kdim)
    xe_ref[:, SUBLANES:, kdim:2 * kdim] = k_ref[...].reshape(nconv, crow, kdim)
    xe_ref[:, SUBLANES:, 2 * kdim:] = v_ref[...].reshape(nconv, crow, kdim)
    cw = cw_ref[...]
    pre = xe_ref[:, SUBLANES:, :] * cw[CONV_W - 1:CONV_W]
    for s in range(1, CONV_W):
        pre = pre + xe_ref[:, SUBLANES - s:SUBLANES - s + crow, :] * cw[CONV_W - 1 - s:CONV_W - s]
    xe_ref[:, 0:SUBLANES, :] = xe_ref[:, crow:crow + SUBLANES, :]
    pre = pre.reshape(rows, 3 * kdim)
    qkv = pre * _sigmoid(pre)

    ba = ba_ref[...]
    beta = _sigmoid(ba[:, 0:heads])
    g = -jnp.exp(alog_ref[...]) * _softplus(ba[:, heads:2 * heads] + dtb_ref[...])
    ii = lax.broadcasted_iota(jnp.int32, (gr, gr), 0)
    jj = lax.broadcasted_iota(jnp.int32, (gr, gr), 1)
    same = (ii // c) == (jj // c)
    incl = jnp.logical_and(same, ii >= jj)
    strict = jnp.logical_and(same, ii > jj)
    eye = (ii == jj).astype(F32)
    incl_b = incl.astype(BF16)
    inclt_b = jnp.logical_and(same, ii <= jj).astype(BF16)
    same_b = same.astype(BF16)
    z = z_ref[...]
    ng = ng_ref[...]
    scale = dk ** -0.5

    gcs, gcts, egcs, ekds, gtots = [], [], [], [], []
    for gi in range(groups):
        gg = g[gi * gr:(gi + 1) * gr]
        gt = -jnp.exp(alogt_ref[...]) * _softplus(bat_ref[0, 0, gi][heads:2 * heads, :] + dtbt_ref[...])
        gc = _dot_exact_lhs(incl_b, gg)
        gtot = _dot_exact_lhs(same_b, gg)
        gcs.append(gc)
        gcts.append(_dot_exact_rhs(gt, inclt_b))
        egcs.append(jnp.exp(gc))
        ekds.append(jnp.exp(gtot - gc))
        gtots.append(gtot)

    k_, kb_, dec_, rhs_, qs_ = {}, {}, {}, {}, {}
    for (gi, h) in units:
        r0 = gi * gr
        qh = qkv[r0:r0 + gr, h * dk:(h + 1) * dk]
        kh = qkv[r0:r0 + gr, kdim + h * dk:kdim + (h + 1) * dk]
        vh = qkv[r0:r0 + gr, 2 * kdim + h * dv:2 * kdim + (h + 1) * dv]
        qh = qh * lax.rsqrt(jnp.sum(qh * qh, axis=-1, keepdims=True) + NORM_EPS)
        kh = kh * lax.rsqrt(jnp.sum(kh * kh, axis=-1, keepdims=True) + NORM_EPS)
        bh = beta[r0:r0 + gr, h:h + 1]
        diff = gcs[gi][:, h:h + 1] - gcts[gi][h:h + 1, :]
        u = (gi, h)
        dec_[u] = jnp.where(incl, jnp.exp(jnp.where(incl, diff, 0.0)), 0.0)
        k_[u] = kh
        kb_[u] = kh * bh
        qs_[u] = qh * scale
        rhs_[u] = jnp.concatenate([vh * bh, kb_[u] * egcs[gi][:, h:h + 1]], axis=1)
    npow = {u: jnp.where(strict, -(_bdot_nt(kb_[u], k_[u]) * dec_[u]), 0.0) for u in units}
    qk_ = {u: _bdot_nt(qs_[u], k_[u]) * dec_[u] for u in units}
    p = {u: eye + npow[u] for u in units}
    lvl = 1
    while 2 * lvl < c:
        npow = {u: _dot3(npow[u], npow[u]) for u in units}
        p = {u: p[u] + _dot3(p[u], npow[u]) for u in units}
        lvl *= 2
    sol = {u: _dot3(p[u], rhs_[u]) for u in units}

    o_ = {}
    if carry:
        assert nseg == 1
        cur = {h: s_ref[h] for h in hs}
        for gi in range(groups):
            wq = {}
            for h in hs:
                u = (gi, h)
                lhs = jnp.concatenate([sol[u][:, dv:], qs_[u] * egcs[gi][:, h:h + 1]], axis=0)
                wq[h] = _bdot(lhs, cur[h])
            for h in hs:
                u = (gi, h)
                vnew = sol[u][:, :dv] - wq[h][:gr]
                o_[u] = wq[h][gr:] + _bdot(qk_[u], vnew)
                kd = k_[u] * ekds[gi][:, h:h + 1]
                glast = jnp.exp(gtots[gi][0:1, h:h + 1])
                cur[h] = cur[h] * glast + _bdot_tn(kd, vnew)
        for h in hs:
            s_ref[h] = cur[h]
        sout_ref[0] = s_ref[...]
    else:
        segs = [(gi, j, h) for gi in range(groups) for j in range(nseg) for h in hs]
        wq = {}
        for (gi, j, h) in segs:
            u = (gi, h)
            sl = slice(j * c, (j + 1) * c)
            lhs = jnp.concatenate([sol[u][sl, dv:], qs_[u][sl] * egcs[gi][sl, h:h + 1]], axis=0)
            wq[(gi, j, h)] = _bdot(lhs, s0_ref[gi * nseg + j, h])
        vn = {}
        for (gi, j, h) in segs:
            u = (gi, h)
            sl = slice(j * c, (j + 1) * c)
            vn[(gi, j, h)] = sol[u][sl, :dv] - wq[(gi, j, h)][:c]
            kd = k_[u][sl] * ekds[gi][sl, h:h + 1]
            glast = jnp.exp(gtots[gi][j * c:j * c + 1, h:h + 1])
            sout_ref[gi * nseg + j, h] = s0_ref[gi * nseg + j, h] * glast + _bdot_tn(kd, vn[(gi, j, h)])
        for u in units:
            gi, h = u
            vnew = jnp.concatenate([vn[(gi, j, h)] for j in range(nseg)], axis=0)
            os_ = jnp.concatenate([wq[(gi, j, h)][c:] for j in range(nseg)], axis=0)
            o_[u] = os_ + _bdot(qk_[u], vnew)

    ys = []
    for gi in range(groups):
        row = []
        for h in hs:
            zz = z[gi * gr:(gi + 1) * gr, h * dv:(h + 1) * dv]
            row.append(_rms(o_[(gi, h)], ng) * (zz * _sigmoid(zz)))
        ys.append(jnp.concatenate(row, axis=1))
    y_ref[...] = jnp.concatenate(ys, axis=0)


def _gdn(proj, col0, row0, batch, seq, c, groups, gr, carry, ba, tail0, s0, cw, alog, dtb, ng, heads, dk, dv):
    kdim = heads * dk
    rows = groups * gr
    if carry:
        nb, nsteps, sb = batch, seq // rows, 1
    else:
        sb = rows // seq
        nb, nsteps = batch // sb, 1
    crow = rows // sb
    rb0 = row0 // rows
    cb0 = col0 // kdim
    rblk = lambda b, ci: rb0 + b * nsteps + ci
    small = lambda shape: pl.BlockSpec(shape, lambda b, ci: (0,) * len(shape))
    bat = ba[row0:row0 + batch * seq].reshape(nb, nsteps, groups, gr, 2 * heads)
    bat = jnp.swapaxes(bat, 3, 4)
    return pl.pallas_call(
        functools.partial(_gdn_kernel, groups=groups, gr=gr, c=c, carry=carry, heads=heads, dk=dk, dv=dv),
        grid=(nb, nsteps),
        in_specs=[pl.BlockSpec((rows, kdim), lambda b, ci: (rblk(b, ci), cb0)),
                  pl.BlockSpec((rows, kdim), lambda b, ci: (rblk(b, ci), cb0 + 1)),
                  pl.BlockSpec((rows, kdim), lambda b, ci: (rblk(b, ci), cb0 + 2)),
                  pl.BlockSpec((rows, kdim), lambda b, ci: (rblk(b, ci), cb0 + 3)),
                  pl.BlockSpec((rows, 2 * heads), lambda b, ci: (rblk(b, ci), 0)),
                  pl.BlockSpec((1, 1, groups, 2 * heads, gr), lambda b, ci: (b, ci, 0, 0, 0)),
                  pl.BlockSpec((sb, SUBLANES, 3 * kdim), lambda b, ci: (b, 0, 0)),
                  pl.BlockSpec((sb, heads, dk, dv), lambda b, ci: (b, 0, 0, 0)),
                  small((CONV_W, 3 * kdim)),
                  small((1, heads)), small((1, heads)), small((heads, 1)), small((heads, 1)),
                  small((1, dv))],
        out_specs=[pl.BlockSpec((rows, heads * dv), lambda b, ci: (b * nsteps + ci, 0)),
                   pl.BlockSpec((sb, heads, dk, dv), lambda b, ci: (b, 0, 0, 0))],
        out_shape=[jax.ShapeDtypeStruct((batch * seq, heads * dv), F32),
                   jax.ShapeDtypeStruct((batch, heads, dk, dv), F32)],
        scratch_shapes=[pltpu.VMEM((sb, SUBLANES + crow, 3 * kdim), F32),
                        pltpu.VMEM((heads, dk, dv), F32)],
        compiler_params=_params("parallel", "arbitrary"),
        name="gdn",
    )(proj, proj, proj, proj, ba, bat, tail0, s0, cw, alog, dtb, alog.T, dtb.T, ng)


def _merge_kernel(h_ref, ya1_ref, ya2_ref, yb1_ref, yb2_ref, wga_ref, wgb_ref, wa_ref, wb_ref, u_ref, *, na):
    d = functools.partial(jnp.dot, preferred_element_type=F32)
    first = pl.program_id(0) < na
    h = h_ref[...]
    ya = jnp.where(first, ya1_ref[...], ya2_ref[...]).astype(BF16)
    yb = jnp.where(first, yb1_ref[...], yb2_ref[...]).astype(BF16)
    u = _sigmoid(d(h, wga_ref[...])) * d(ya, wa_ref[...])
    u = u + _sigmoid(d(h, wgb_ref[...])) * d(yb, wb_ref[...])
    u_ref[...] = u.astype(BF16)


def _merge(h, ya1, ya2, yb1, yb2, w_g, w_a, w_b, tm, tn):
    m, d = h.shape
    n = w_a.shape[1]
    nj = n // tn
    return pl.pallas_call(
        functools.partial(_merge_kernel, na=ya1.shape[0] // tm),
        grid=(m // tm, nj),
        in_specs=[pl.BlockSpec((tm, d), lambda i, j: (i, 0)),
                  *_split_rows(ya1, ya2, tm),
                  *_split_rows(yb1, yb2, tm),
                  pl.BlockSpec((d, tn), lambda i, j: (0, j)),
                  pl.BlockSpec((d, tn), lambda i, j: (0, nj + j)),
                  pl.BlockSpec((w_a.shape[0], tn), lambda i, j: (0, j)),
                  pl.BlockSpec((w_b.shape[0], tn), lambda i, j: (0, j))],
        out_specs=pl.BlockSpec((tm, tn), lambda i, j: (i, j)),
        out_shape=jax.ShapeDtypeStruct((m, n), BF16),
        compiler_params=_params("parallel", "parallel"),
        name="merge",
    )(h, ya1, ya2, yb1, yb2, w_g, w_g, w_a, w_b)


def _out_norm_kernel(u_ref, w_ref, xa_ref, xb_ref, g_ref, x1_ref, h_ref, *, na):
    x = jnp.where(pl.program_id(0) < na, xa_ref[...], xb_ref[...])
    x1 = x + jnp.dot(u_ref[...], w_ref[...], preferred_element_type=F32)
    x1_ref[...] = x1
    h_ref[...] = _rms(x1, g_ref[...]).astype(BF16)


def _out_norm(u, w, xa, xb, g, tm):
    m, k = u.shape
    d = xa.shape[1]
    return pl.pallas_call(
        functools.partial(_out_norm_kernel, na=xa.shape[0] // tm),
        grid=(m // tm,),
        in_specs=[pl.BlockSpec((tm, k), lambda i: (i, 0)),
                  pl.BlockSpec((k, d), lambda i: (0, 0)),
                  *_split_rows(xa, xb, tm),
                  pl.BlockSpec((1, d), lambda i: (0, 0))],
        out_specs=[pl.BlockSpec((tm, d), lambda i: (i, 0)),
                   pl.BlockSpec((tm, d), lambda i: (i, 0))],
        out_shape=[jax.ShapeDtypeStruct((m, d), F32), jax.ShapeDtypeStruct((m, d), BF16)],
        compiler_params=_params("parallel"),
        name="out_norm",
    )(u, w, xa, xb, g)


def _memkv_kernel(m_ref, g_ref, wk_ref, wv_ref, k_ref, v_ref):
    mn = _rms(m_ref[...], g_ref[...]).astype(BF16)
    k_ref[...] = jnp.dot(mn, wk_ref[...], preferred_element_type=F32)
    v_ref[...] = jnp.dot(mn, wv_ref[...], preferred_element_type=F32)


def _memkv(mem, g, wk, wv, tm):
    m, d = mem.shape
    n = wk.shape[1]
    return pl.pallas_call(
        _memkv_kernel,
        grid=(m // tm,),
        in_specs=[pl.BlockSpec((tm, d), lambda i: (i, 0)),
                  pl.BlockSpec((1, d), lambda i: (0, 0)),
                  pl.BlockSpec((d, n), lambda i: (0, 0)),
                  pl.BlockSpec((d, n), lambda i: (0, 0))],
        out_specs=[pl.BlockSpec((tm, n), lambda i: (i, 0)), pl.BlockSpec((tm, n), lambda i: (i, 0))],
        out_shape=[jax.ShapeDtypeStruct((m, n), F32), jax.ShapeDtypeStruct((m, n), F32)],
        compiler_params=_params("parallel"),
        name="memkv",
    )(mem, g, wk, wv)


def _attn_kernel(q_ref, k_ref, v_ref, o_ref, *, nb, tq, heads, hd, interleaved):
    mem = k_ref.shape[1] // heads if interleaved else k_ref.shape[1]
    scale = hd ** -0.5
    units = [(b, h) for b in range(nb) for h in range(heads)]

    def head(ref, b, h):
        if interleaved:
            return ref[b, pl.ds(h, mem, stride=heads), :].astype(BF16)
        return ref[b, :, h * hd:(h + 1) * hd].astype(BF16)

    q = q_ref[...].astype(BF16)
    s = {(b, h): lax.dot_general(q[b * tq:(b + 1) * tq, h * hd:(h + 1) * hd], head(k_ref, b, h),
                                 (((1,), (1,)), ((), ())), preferred_element_type=F32) * scale for (b, h) in units}
    p = {u: jnp.exp(s[u] - jnp.max(s[u], axis=-1, keepdims=True)) for u in units}
    o = {(b, h): jnp.dot(p[(b, h)].astype(BF16), head(v_ref, b, h), preferred_element_type=F32)
         / jnp.sum(p[(b, h)], axis=-1, keepdims=True) for (b, h) in units}
    o_ref[...] = jnp.concatenate([jnp.concatenate([o[(b, h)] for h in range(heads)], axis=1) for b in range(nb)],
                                 axis=0)


def _attn(q, row0, batch, seq, nb, tq, k, v, heads, hd, interleaved):
    nt = seq // tq
    rows = nb * tq
    rb0 = row0 // rows
    n = heads * hd
    return pl.pallas_call(
        functools.partial(_attn_kernel, nb=nb, tq=tq, heads=heads, hd=hd, interleaved=interleaved),
        grid=(batch // nb, nt),
        in_specs=[pl.BlockSpec((rows, n), lambda b, t: (rb0 + b * nt + t, 0)),
                  pl.BlockSpec((nb,) + k.shape[1:], lambda b, t: (b, 0, 0)),
                  pl.BlockSpec((nb,) + v.shape[1:], lambda b, t: (b, 0, 0))],
        out_specs=pl.BlockSpec((rows, n), lambda b, t: (b * nt + t, 0)),
        out_shape=jax.ShapeDtypeStruct((batch * seq, n), F32),
        compiler_params=_params("parallel", "parallel"),
        name="attn",
    )(q, k, v)


def _xo_router_kernel(oa_ref, ob_ref, wo_ref, x1_ref, g_ref, rw_ref, rb_ref, x2_ref, h_ref, lg_ref, *, na):
    o = jnp.where(pl.program_id(0) < na, oa_ref[...], ob_ref[...]).astype(BF16)
    x2 = x1_ref[...] + jnp.dot(o, wo_ref[...], preferred_element_type=F32)
    x2_ref[...] = x2
    h = _rms(x2, g_ref[...])
    h_ref[...] = h
    lg_ref[...] = _dot3(h, rw_ref[...]) + rb_ref[...]


def _xo_router(oa, ob, wo, x1, g, rw, rb, tm):
    m, d = x1.shape
    ne = rw.shape[1]
    row = lambda w: pl.BlockSpec((tm, w), lambda i: (i, 0))
    full = lambda a: pl.BlockSpec(a.shape, lambda i: (0, 0))
    return pl.pallas_call(
        functools.partial(_xo_router_kernel, na=oa.shape[0] // tm),
        grid=(m // tm,),
        in_specs=[*_split_rows(oa, ob, tm), full(wo), row(d), full(g), full(rw), full(rb)],
        out_specs=[row(d), row(d), row(ne)],
        out_shape=[jax.ShapeDtypeStruct((m, d), F32), jax.ShapeDtypeStruct((m, d), F32),
                   jax.ShapeDtypeStruct((m, ne), F32)],
        compiler_params=_params("parallel"),
        name="xo_router",
    )(oa, ob, wo, x1, g, rw, rb)


def _moe_kernel(e_ref, r0_ref, n_ref, code_ref, h_hbm, wg_ref, wu_ref, wd_ref, bg_ref, bu_ref, bd_ref,
                slots_hbm, xbuf, acc, wgb, wub, wdb, gsem, ssem, *, sub, nf):
    s = pl.program_id(0)
    f = pl.program_id(1)
    n = n_ref[s]
    r0 = r0_ref[s]

    @pl.when(jnp.logical_and(s == 0, f == 0))
    def _():
        xbuf[...] = jnp.zeros_like(xbuf)

    n_tok = h_hbm.shape[0]

    def row_in(i):
        tok = lax.shift_right_logical(code_ref[r0 + i], TOP_K_SHIFT)
        return pltpu.make_async_copy(h_hbm.at[pl.ds(tok, 1), :], xbuf.at[pl.ds(i, 1), :], gsem)

    def row_out(i):
        code = code_ref[r0 + i]
        dst = (code & (TOP_K - 1)) * n_tok + lax.shift_right_logical(code, TOP_K_SHIFT)
        return pltpu.make_async_copy(acc.at[pl.ds(i, 1), :], slots_hbm.at[pl.ds(dst, 1), :], ssem)

    def for_rows(fn):
        nfull = n // DMA_UNROLL

        def chunk(i, c):
            for k in range(DMA_UNROLL):
                fn(i * DMA_UNROLL + k)
            return c

        lax.fori_loop(0, nfull, chunk, 0)
        lax.fori_loop(nfull * DMA_UNROLL, n, lambda i, c: (fn(i), c)[1], 0)

    @pl.when(n > 0)
    def _active():
        @pl.when(f == 0)
        def _gather():
            for_rows(lambda i: row_in(i).start())
            for_rows(lambda i: row_in(i).wait())

        wgb[...] = wg_ref[0].astype(BF16)
        wub[...] = wu_ref[0].astype(BF16)
        wdb[...] = wd_ref[0].astype(BF16)
        bg = bg_ref[0]
        bu = bu_ref[0]
        bd = bd_ref[0]

        nsub = (n + sub - 1) // sub

        @pl.when(f == 0)
        def _init():
            def fill(j, c):
                acc[pl.ds(pl.multiple_of(j * sub, sub), sub), :] = jnp.broadcast_to(bd, (sub, acc.shape[1]))
                return c

            lax.fori_loop(0, nsub, fill, 0)

        def block(off, m):
            x = xbuf[pl.ds(off, m), :].astype(BF16)
            gate = jnp.dot(x, wgb[...], preferred_element_type=F32) + bg
            up = jnp.dot(x, wub[...], preferred_element_type=F32) + bu
            gate = jnp.minimum(gate, SWIGLU_LIMIT)
            up = jnp.clip(up, -SWIGLU_LIMIT, SWIGLU_LIMIT)
            act = gate * _sigmoid(SWIGLU_ALPHA * gate) * (up + 1.0)
            acc[pl.ds(off, m), :] += jnp.dot(act.astype(BF16), wdb[...], preferred_element_type=F32)

        big = 2 * sub
        nbig = nsub // 2

        def big_block(j, c):
            block(pl.multiple_of(j * big, big), big)
            return c

        lax.fori_loop(0, nbig, big_block, 0)

        @pl.when(nsub % 2 == 1)
        def _odd():
            block(pl.multiple_of(nbig * big, sub), sub)

        @pl.when(f == nf - 1)
        def _scatter():
            for_rows(lambda i: row_out(i).start())
            for_rows(lambda i: row_out(i).wait())


def _moe(h, sb_e, sb_r0, sb_n, codes, wg, wu, wd, bg, bu, bd, cap, sub, tf):
    n_tok, d = h.shape
    ne, _, dff = wg.shape
    nf = dff // tf
    g = sb_e.shape[0]
    fi = lambda s, f, n_ref: jnp.where(n_ref[s] > 0, f, nf - 1)
    grid_spec = pltpu.PrefetchScalarGridSpec(
        num_scalar_prefetch=4,
        grid=(g, nf),
        in_specs=[pl.BlockSpec(memory_space=pl.ANY),
                  pl.BlockSpec((1, d, tf), lambda s, f, e, r, n, c: (e[s], 0, fi(s, f, n))),
                  pl.BlockSpec((1, d, tf), lambda s, f, e, r, n, c: (e[s], 0, fi(s, f, n))),
                  pl.BlockSpec((1, tf, d), lambda s, f, e, r, n, c: (e[s], fi(s, f, n), 0)),
                  pl.BlockSpec((1, 1, tf), lambda s, f, e, r, n, c: (e[s], 0, fi(s, f, n))),
                  pl.BlockSpec((1, 1, tf), lambda s, f, e, r, n, c: (e[s], 0, fi(s, f, n))),
                  pl.BlockSpec((1, 1, d), lambda s, f, e, r, n, c: (e[s], 0, 0))],
        out_specs=pl.BlockSpec(memory_space=pl.ANY),
        scratch_shapes=[pltpu.VMEM((cap, d), F32), pltpu.VMEM((cap, d), F32),
                        pltpu.VMEM((d, tf), BF16), pltpu.VMEM((d, tf), BF16), pltpu.VMEM((tf, d), BF16),
                        pltpu.SemaphoreType.DMA(()), pltpu.SemaphoreType.DMA(())],
    )
    return pl.pallas_call(
        functools.partial(_moe_kernel, sub=sub, nf=nf),
        grid_spec=grid_spec,
        out_shape=jax.ShapeDtypeStruct((n_tok * TOP_K, d), F32),
        compiler_params=_params("arbitrary", "arbitrary"),
        name="moe",
    )(sb_e, sb_r0, sb_n, codes, h, wg, wu, wd, bg.reshape(ne, 1, dff), bu.reshape(ne, 1, dff),
      bd.reshape(ne, 1, d))


def _route(logits, cap, n_sb_max):
    n_tok, ne = logits.shape
    top_val, top_idx = lax.top_k(logits, TOP_K)
    gates = jax.nn.softmax(top_val, axis=-1)
    e_flat = top_idx.reshape(-1).astype(jnp.int32)
    codes = jnp.argsort(e_flat, stable=True).astype(jnp.int32)
    counts = jnp.zeros((ne,), jnp.int32).at[e_flat].add(1)
    start = jnp.cumsum(counts) - counts
    n_sb = (counts + cap - 1) // cap
    sb_end = jnp.cumsum(n_sb)
    sb_start = sb_end - n_sb
    total = sb_end[-1]
    s = jnp.arange(n_sb_max, dtype=jnp.int32)
    e_s = jnp.minimum(jnp.searchsorted(sb_end, s, side='right'), ne - 1).astype(jnp.int32)
    e_last = e_s[jnp.maximum(total - 1, 0)]
    active = s < total
    e_s = jnp.where(active, e_s, e_last)
    within = s - sb_start[e_s]
    r0 = jnp.where(active, start[e_s] + within * cap, 0).astype(jnp.int32)
    n = jnp.where(active, jnp.clip(counts[e_s] - within * cap, 0, cap), 0).astype(jnp.int32)
    return gates, codes, e_s, r0, n


def _final_kernel(x2_ref, s0_ref, s1_ref, s2_ref, s3_ref, gates_ref, g_ref, y_ref):
    x = x2_ref[...]
    gt = gates_ref[...]
    for k, s_ref in enumerate((s0_ref, s1_ref, s2_ref, s3_ref)):
        x = x + s_ref[...] * gt[:, k:k + 1]
    y_ref[...] = _rms(x, g_ref[...])


def _final(x2, slots, gates, g, row0, rows, tm):
    n_tok, d = x2.shape
    rb0 = row0 // tm
    nblk = n_tok // tm
    slot = lambda k: pl.BlockSpec((tm, d), lambda i: (k * nblk + rb0 + i, 0))
    return pl.pallas_call(
        _final_kernel,
        grid=(rows // tm,),
        in_specs=[pl.BlockSpec((tm, d), lambda i: (rb0 + i, 0)),
                  slot(0), slot(1), slot(2), slot(3),
                  pl.BlockSpec((tm, TOP_K), lambda i: (rb0 + i, 0)),
                  pl.BlockSpec((1, d), lambda i: (0, 0))],
        out_specs=pl.BlockSpec((tm, d), lambda i: (i, 0)),
        out_shape=jax.ShapeDtypeStruct((rows, d), F32),
        compiler_params=_params("parallel"),
        name="final",
    )(x2, slots, slots, slots, slots, gates, g)


def _pad_tail(buf):
    return jnp.pad(buf, ((0, 0), (SUBLANES - (CONV_W - 1), 0), (0, 0)))


def kernel(x_prompt, x_sample, mem_prompt, state_lru_conv, state_lru_h, state_gdn_conv, state_gdn_S, cache_mem_k, cache_mem_v, norm_mix_g, w_in, lru_conv_w, lru_conv_b, lru_wx, lru_bx, lru_wa, lru_ba, lru_a_param, gdn_conv_w, gdn_A_log, gdn_dt_bias, gdn_norm_g, w_branch_a, w_branch_b, w_out, norm_xa_g, norm_mem_g, xa_wq, xa_wk, xa_wv, xa_wo, norm_moe_g, router_w, router_b, moe_w_gate, moe_b_gate, moe_w_up, moe_b_up, moe_w_down, moe_b_down, norm_final_g):
    depth = w_in.shape[0]
    assert depth == 1
    bp, tp, d = x_prompt.shape
    bs, ts, _ = x_sample.shape
    np_, ns = bp * tp, bs * ts
    n_tok = np_ + ns
    lw = lru_conv_w.shape[2]
    heads, dk, dv = state_gdn_S.shape[2:]
    kdim = heads * dk
    mem_len = mem_prompt.shape[1]
    xa_heads, xa_hd = cache_mem_k.shape[3:]
    xa_dim = xa_heads * xa_hd
    ne = router_w.shape[2]
    past_len = 16384
    gdn_chunk = 64

    row = lambda v: v.reshape(1, -1)
    xp = x_prompt.reshape(np_, d)
    xs = x_sample.reshape(ns, d)

    c_main = 2 * lw + 3 * kdim + heads * dv
    w_in0 = w_in[0]
    w_main = w_in0[:, :c_main].astype(BF16)
    w_ba = w_in0[:, c_main:c_main + 2 * heads].astype(BF16)
    w_g = w_in0[:, c_main + 2 * heads:].astype(BF16)
    w_gates = jnp.concatenate([lru_wx[0], lru_wa[0]], axis=-1).astype(BF16)

    rows_gcd = math.gcd(np_, ns)
    tm_s = _tile(rows_gcd, 256)
    tm_m = _tile(rows_gcd, 512)
    tm_l = _tile(rows_gcd, 1024)
    h1, ba = _norm_ba(xp, xs, row(norm_mix_g[0]), w_ba, tm_m)
    proj = _matmul(h1, w_main, c_main, tm_l, _tile(c_main, 1024), F32, "in_proj")

    lru_args = (lru_conv_w[0], row(lru_conv_b[0]), w_gates, row(lru_bx[0]), row(lru_ba[0]), row(lru_a_param[0]))
    ya_p, lru_h_p = _lru_seq(proj, 0, bp, tp, _tile(tp, 256), jnp.zeros((bp, SUBLANES, lw), F32),
                             jnp.zeros((bp, 1, lw), F32), *lru_args, pos0=0)
    if ts == SUBLANES:
        ya_s, lru_h_s = _lru_short(proj, np_, bs, math.gcd(bs, 32), _pad_tail(state_lru_conv[0]),
                                   state_lru_h[0][:, None, :], *lru_args)
    else:
        ya_s, lru_h_s = _lru_seq(proj, np_, bs, ts, ts, _pad_tail(state_lru_conv[0]), state_lru_h[0][:, None, :],
                                 *lru_args, pos0=past_len)

    gdn_args = (gdn_conv_w[0], row(gdn_A_log[0]), row(gdn_dt_bias[0]), row(gdn_norm_g[0]), heads, dk, dv)
    cp = min(gdn_chunk, tp)
    assert tp % cp == 0 and ts <= gdn_chunk
    gp = 2 if (tp // cp) % 2 == 0 else 1
    yb_p, gdn_s_p = _gdn(proj, 2 * lw, 0, bp, tp, cp, gp, cp, True, ba,
                         jnp.zeros((bp, SUBLANES, 3 * kdim), F32), jnp.zeros((bp, heads, dk, dv), F32), *gdn_args)
    gr_s = ts * math.gcd(bs, max(gdn_chunk // ts, 1))
    gs = 2 if (ns // gr_s) % 2 == 0 else 1
    yb_s, gdn_s_s = _gdn(proj, 2 * lw, np_, bs, ts, ts, gs, gr_s, False, ba,
                         _pad_tail(state_gdn_conv[0]), state_gdn_S[0], *gdn_args)

    u = _merge(h1, ya_p, ya_s, yb_p, yb_s, w_g, w_branch_a[0].astype(BF16), w_branch_b[0].astype(BF16), tm_m,
               _tile(d, 512))
    x1, h2 = _out_norm(u, w_out[0].astype(BF16), xp, xs, row(norm_xa_g[0]), tm_m)

    k_p, v_p = _memkv(mem_prompt.reshape(bp * mem_len, d), row(norm_mem_g[0]), xa_wk[0].astype(BF16),
                      xa_wv[0].astype(BF16), _tile(bp * mem_len, 256))
    q = _matmul(h2, xa_wq[0].astype(BF16), xa_dim, tm_l, xa_dim, F32, "xa_q")
    o_p = _attn(q, 0, bp, tp, 1, _tile(tp, 512), k_p.reshape(bp, mem_len, xa_dim), v_p.reshape(bp, mem_len, xa_dim),
                xa_heads, xa_hd, False)
    nb_s = math.gcd(bs, 4)
    o_s = _attn(q, np_, bs, ts, nb_s, ts, cache_mem_k[0].reshape(bs, mem_len * xa_heads, xa_hd),
                cache_mem_v[0].reshape(bs, mem_len * xa_heads, xa_hd), xa_heads, xa_hd, True)
    x2, h3, logits = _xo_router(o_p, o_s, xa_wo[0].astype(BF16), x1, row(norm_moe_g[0]), router_w[0],
                                row(router_b[0]), tm_m)

    cap, sub, tf = 1536, 256, _tile(moe_w_gate.shape[3], 256)
    n_sb_max = (n_tok * TOP_K + ne * (cap - 1)) // cap
    gates, codes, sb_e, sb_r0, sb_n = _route(logits, cap, n_sb_max)
    slots = _moe(h3, sb_e, sb_r0, sb_n, codes, moe_w_gate[0], moe_w_up[0], moe_w_down[0],
                 moe_b_gate[0], moe_b_up[0], moe_b_down[0], cap, sub, tf)
    y_p =_final(x2, slots, gates, row(norm_final_g), 0, np_, tm_s)
    y_s = _final(x2, slots, gates, row(norm_final_g), np_, ns, tm_s)

    keep = CONV_W - 1
    pp = jnp.stack([lax.slice(proj, ((b + 1) * tp - keep, 0), ((b + 1) * tp, c_main)) for b in range(bp)])
    ps = proj[np_:].reshape(bs, ts, c_main)
    return (y_p.reshape(bp, tp, d), y_s.reshape(bs, ts, d),
            pp[:, :, :lw][None], lru_h_p.reshape(1, bp, lw),
            pp[:, :, 2 * lw:2 * lw + 3 * kdim][None], gdn_s_p[None],
            k_p.reshape(1, bp, mem_len, xa_heads, xa_hd), v_p.reshape(1, bp, mem_len, xa_heads, xa_hd),
            ps[:, ts - keep:, :lw][None], lru_h_s.reshape(1, bs, lw),
            ps[:, ts - keep:, 2 * lw:2 * lw + 3 * kdim][None], gdn_s_s[None])
```

```python
import functools
import math

import jax
import jax.numpy as jnp
from jax import lax
from jax.experimental import pallas as pl
from jax.experimental.pallas import tpu as pltpu

F32 = jnp.float32
BF16 = jnp.bfloat16
NORM_EPS = 1e-6
LRU_C = 8.0
CONV_W = 4
SWIGLU_ALPHA = 1.702
SWIGLU_LIMIT = 7.0
TOP_K = 4
TOP_K_SHIFT = 2
DMA_UNROLL = 8
VMEM_LIMIT_BYTES = 56 * 1024 * 1024
SUBLANES = 8
LANES = 128


def _params(*sem):
    return pltpu.CompilerParams(dimension_semantics=sem, vmem_limit_bytes=VMEM_LIMIT_BYTES)


def _bdot(a, b):
    return jnp.dot(a.astype(BF16), b.astype(BF16), preferred_element_type=F32)


def _bdot_nt(a, b):
    return lax.dot_general(a.astype(BF16), b.astype(BF16), (((1,), (1,)), ((), ())), preferred_element_type=F32)


def _bdot_tn(a, b):
    return lax.dot_general(a.astype(BF16), b.astype(BF16), (((0,), (0,)), ((), ())), preferred_element_type=F32)


def _split2(x):
    hi = x.astype(BF16)
    lo = (x - hi.astype(F32)).astype(BF16)
    return hi, lo


def _split3(x):
    p1 = x.astype(BF16)
    r1 = x - p1.astype(F32)
    p2 = r1.astype(BF16)
    p3 = (r1 - p2.astype(F32)).astype(BF16)
    return p1, p2, p3


def _dot3(a, b):
    ah, al = _split2(a)
    bh, bl = _split2(b)
    d = functools.partial(jnp.dot, preferred_element_type=F32)
    return d(ah, bh) + (d(ah, bl) + d(al, bh))


def _dot_exact_lhs(m_bf16, x):
    d = functools.partial(jnp.dot, preferred_element_type=F32)
    p1, p2, p3 = _split3(x)
    return d(m_bf16, p1) + (d(m_bf16, p2) + d(m_bf16, p3))


def _dot_exact_rhs(x, m_bf16):
    d = functools.partial(jnp.dot, preferred_element_type=F32)
    p1, p2, p3 = _split3(x)
    return d(p1, m_bf16) + (d(p2, m_bf16) + d(p3, m_bf16))


def _sigmoid(x):
    return jax.nn.sigmoid(x)


def _tile(n, pref):
    if n <= pref:
        return n
    t = pref - pref % SUBLANES
    while n % t:
        t -= SUBLANES
    return t


def _softplus(x):
    return jnp.maximum(x, 0.0) + jnp.log1p(jnp.exp(-jnp.abs(x)))


def _gelu_tanh(x):
    return 0.5 * x * (1.0 + jnp.tanh(0.7978845608028654 * (x + 0.044715 * (x * x * x))))


def _rms(x, g):
    r = lax.rsqrt(jnp.mean(x * x, axis=-1, keepdims=True) + NORM_EPS)
    return x * r * g


def _split_rows(a, b, tm):
    na = a.shape[0] // tm
    w = a.shape[1]
    return (pl.BlockSpec((tm, w), lambda i, *_: (jnp.minimum(i, na - 1), 0)),
            pl.BlockSpec((tm, w), lambda i, *_: (jnp.maximum(i - na, 0), 0)))


def _norm_ba_kernel(xa_ref, xb_ref, g_ref, wba_ref, h_ref, ba_ref, *, na):
    x = jnp.where(pl.program_id(0) < na, xa_ref[...], xb_ref[...])
    h = _rms(x, g_ref[...]).astype(BF16)
    h_ref[...] = h
    ba_ref[...] = jnp.dot(h, wba_ref[...], preferred_element_type=F32)


def _norm_ba(xa, xb, g, w_ba, tm):
    d = xa.shape[1]
    m = xa.shape[0] + xb.shape[0]
    nb = w_ba.shape[1]
    return pl.pallas_call(
        functools.partial(_norm_ba_kernel, na=xa.shape[0] // tm),
        grid=(m // tm,),
        in_specs=[*_split_rows(xa, xb, tm),
                  pl.BlockSpec((1, d), lambda i: (0, 0)),
                  pl.BlockSpec((d, nb), lambda i: (0, 0))],
        out_specs=[pl.BlockSpec((tm, d), lambda i: (i, 0)),
                   pl.BlockSpec((tm, nb), lambda i: (i, 0))],
        out_shape=[jax.ShapeDtypeStruct((m, d), BF16), jax.ShapeDtypeStruct((m, nb), F32)],
        compiler_params=_params("parallel"),
        name="norm_ba",
    )(xa, xb, g, w_ba)


def _mm_kernel(a_ref, w_ref, o_ref):
    o_ref[...] = jnp.dot(a_ref[...], w_ref[...], preferred_element_type=F32).astype(o_ref.dtype)


def _matmul(a, w, n_cols, tm, tn, out_dtype, name):
    m, k = a.shape
    return pl.pallas_call(
        _mm_kernel,
        grid=(m // tm, n_cols // tn),
        in_specs=[pl.BlockSpec((tm, k), lambda i, j: (i, 0)),
                  pl.BlockSpec((k, tn), lambda i, j: (0, j))],
        out_specs=pl.BlockSpec((tm, tn), lambda i, j: (i, j)),
        out_shape=jax.ShapeDtypeStruct((m, n_cols), out_dtype),
        compiler_params=_params("parallel", "parallel"),
        name=name,
    )(a, w)


def _conv4_into(xe_ref, rows, cw):
    y = xe_ref[SUBLANES:SUBLANES + rows, :] * cw[CONV_W - 1:CONV_W]
    for s in range(1, CONV_W):
        y = y + xe_ref[SUBLANES - s:SUBLANES - s + rows, :] * cw[CONV_W - 1 - s:CONV_W - s]
    return y


def _scan8(a3, b3):
    row = lax.broadcasted_iota(jnp.int32, a3.shape, 1)
    for s in (1, 2, 4):
        a_sh = pltpu.roll(a3, s, axis=1)
        b_sh = pltpu.roll(b3, s, axis=1)
        m = row >= s
        b3 = jnp.where(m, a3 * b_sh + b3, b3)
        a3 = jnp.where(m, a3 * a_sh, a3)
    return a3, b3


def _lru_gates(xc, wg_ref, bx, bga, ap, first_row_is_pos0):
    nblk = wg_ref.shape[0]
    bw = wg_ref.shape[1]
    xcb = xc.astype(BF16)
    gx, ga = [], []
    for n in range(nblk):
        r = jnp.dot(xcb[:, n * bw:(n + 1) * bw], wg_ref[n], preferred_element_type=F32)
        gx.append(r[:, :bw])
        ga.append(r[:, bw:])
    gate_x = _sigmoid(jnp.concatenate(gx, axis=1) + bx)
    gate_a = _sigmoid(jnp.concatenate(ga, axis=1) + bga)
    log_a = (-LRU_C) * gate_a * _softplus(ap)
    a = jnp.exp(log_a)
    mult = jnp.sqrt(-jnp.tanh(log_a) * (a * a + 1.0))
    if first_row_is_pos0 is not None:
        mult = jnp.where(first_row_is_pos0, 1.0, mult)
    return a, xc * gate_x * mult


def _lru_seq_kernel(xa_ref, ya_ref, tail0_ref, h0_ref, cw_ref, cb_ref, wg_ref, bx_ref, bga_ref, ap_ref,
                    y_ref, hl_ref, xe_ref, a_ref, b_ref, h_ref, *, tt, pos0):
    t = pl.program_id(1)
    w = xa_ref.shape[1]

    @pl.when(t == 0)
    def _():
        xe_ref[0:SUBLANES, :] = tail0_ref[0]
        h_ref[...] = h0_ref[0]

    xe_ref[SUBLANES:SUBLANES + tt, :] = xa_ref[...]
    xc = _conv4_into(xe_ref, tt, cw_ref[...]) + cb_ref[...]
    xe_ref[0:SUBLANES, :] = xe_ref[tt:tt + SUBLANES, :]
    first = None
    if pos0 == 0:
        first = (lax.broadcasted_iota(jnp.int32, (tt, 1), 0) + t * tt) == 0
    a, b = _lru_gates(xc, wg_ref, bx_ref[...], bga_ref[...], ap_ref[...], first)
    a3, b3 = _scan8(a.reshape(tt // SUBLANES, SUBLANES, w), b.reshape(tt // SUBLANES, SUBLANES, w))
    a_ref[...] = a3
    b_ref[...] = b3

    def body(g, h):
        hg = a_ref[g] * h + b_ref[g]
        b_ref[g] = hg
        return hg[SUBLANES - 1:SUBLANES, :]

    h = lax.fori_loop(0, tt // SUBLANES, body, h_ref[...])
    h_ref[...] = h
    hl_ref[0] = h
    hs = b_ref[...].reshape(tt, w)
    y_ref[...] = hs * _gelu_tanh(ya_ref[...])


def _lru_seq(proj, row0, batch, seq, tt, tail0, h0, cw, cb, wg, bx, bga, ap, pos0):
    w = cw.shape[1]
    nt = seq // tt
    rb0 = row0 // tt
    vec = lambda: pl.BlockSpec((1, w), lambda b, t: (0, 0))
    return pl.pallas_call(
        functools.partial(_lru_seq_kernel, tt=tt, pos0=pos0),
        grid=(batch, nt),
        in_specs=[pl.BlockSpec((tt, w), lambda b, t: (rb0 + b * nt + t, 0)),
                  pl.BlockSpec((tt, w), lambda b, t: (rb0 + b * nt + t, 1)),
                  pl.BlockSpec((1, SUBLANES, w), lambda b, t: (b, 0, 0)),
                  pl.BlockSpec((1, 1, w), lambda b, t: (b, 0, 0)),
                  pl.BlockSpec((CONV_W, w), lambda b, t: (0, 0)),
                  vec(),
                  pl.BlockSpec(wg.shape, lambda b, t: (0, 0, 0)),
                  vec(), vec(), vec()],
        out_specs=[pl.BlockSpec((tt, w), lambda b, t: (b * nt + t, 0)),
                   pl.BlockSpec((1, 1, w), lambda b, t: (b, 0, 0))],
        out_shape=[jax.ShapeDtypeStruct((batch * seq, w), F32),
                   jax.ShapeDtypeStruct((batch, 1, w), F32)],
        scratch_shapes=[pltpu.VMEM((SUBLANES + tt, w), F32),
                        pltpu.VMEM((tt // SUBLANES, SUBLANES, w), F32),
                        pltpu.VMEM((tt // SUBLANES, SUBLANES, w), F32),
                        pltpu.VMEM((1, w), F32)],
        compiler_params=_params("parallel", "arbitrary"),
        name="lru_seq",
    )(proj, proj, tail0, h0, cw, cb, wg, bx, bga, ap)


def _lru_short_kernel(xa_ref, ya_ref, tail0_ref, h0_ref, cw_ref, cb_ref, wg_ref, bx_ref, bga_ref, ap_ref,
                      y_ref, hl_ref, xe_ref):
    nb = xe_ref.shape[0]
    w = xa_ref.shape[1]
    xe_ref[:, 0:SUBLANES, :] = tail0_ref[...]
    xe_ref[:, SUBLANES:, :] = xa_ref[...].reshape(nb, SUBLANES, w)
    cw = cw_ref[...]
    xc = xe_ref[:, SUBLANES:, :] * cw[CONV_W - 1:CONV_W]
    for s in range(1, CONV_W):
        xc = xc + xe_ref[:, SUBLANES - s:2 * SUBLANES - s, :] * cw[CONV_W - 1 - s:CONV_W - s]
    xc = xc.reshape(nb * SUBLANES, w) + cb_ref[...]
    a, b = _lru_gates(xc, wg_ref, bx_ref[...], bga_ref[...], ap_ref[...], None)
    a3, b3 = _scan8(a.reshape(nb, SUBLANES, w), b.reshape(nb, SUBLANES, w))
    hs = a3 * h0_ref[...] + b3
    hl_ref[...] = hs[:, SUBLANES - 1:SUBLANES, :]
    y_ref[...] = hs.reshape(nb * SUBLANES, w) * _gelu_tanh(ya_ref[...])


def _lru_short(proj, row0, batch, nb, tail0, h0, cw, cb, wg, bx, bga, ap):
    w = cw.shape[1]
    rows = nb * SUBLANES
    rb0 = row0 // rows
    vec = lambda: pl.BlockSpec((1, w), lambda b: (0, 0))
    return pl.pallas_call(
        _lru_short_kernel,
        grid=(batch // nb,),
        in_specs=[pl.BlockSpec((rows, w), lambda b: (rb0 + b, 0)),
                  pl.BlockSpec((rows, w), lambda b: (rb0 + b, 1)),
                  pl.BlockSpec((nb, SUBLANES, w), lambda b: (b, 0, 0)),
                  pl.BlockSpec((nb, 1, w), lambda b: (b, 0, 0)),
                  pl.BlockSpec((CONV_W, w), lambda b: (0, 0)),
                  vec(),
                  pl.BlockSpec(wg.shape, lambda b: (0, 0, 0)),
                  vec(), vec(), vec()],
        out_specs=[pl.BlockSpec((rows, w), lambda b: (b, 0)),
                   pl.BlockSpec((nb, 1, w), lambda b: (b, 0, 0))],
        out_shape=[jax.ShapeDtypeStruct((batch * SUBLANES, w), F32),
                   jax.ShapeDtypeStruct((batch, 1, w), F32)],
        scratch_shapes=[pltpu.VMEM((nb, 2 * SUBLANES, w), F32)],
        compiler_params=_params("parallel"),
        name="lru_short",
    )(proj, proj, tail0, h0, cw, cb, wg, bx, bga, ap)


def _gdn_kernel(q_ref, k_ref, v_ref, z_ref, ba_ref, bat_ref, tail0_ref, s0_ref, cw_ref, alog_ref, dtb_ref,
                alogt_ref, dtbt_ref, ng_ref, y_ref, sout_ref, xe_ref, s_ref, *, groups, gr, c, carry, heads, dk, dv):
    ci = pl.program_id(1)
    kdim = heads * dk
    rows = groups * gr
    nseg = gr // c
    nconv, crow = xe_ref.shape[0], xe_ref.shape[1] - SUBLANES
    hs = range(heads)
    units = [(gi, h) for gi in range(groups) for h in hs]

    @pl.when(ci == 0)
    def _():
        xe_ref[:, 0:SUBLANES, :] = tail0_ref[...]
        if carry:
            s_ref[...] = s0_ref[0]

    xe_ref[:, SUBLANES:, 0:kdim] = q_ref[...].reshape(nconv, crow, kdim)
    xe_ref[:, SUBLANES:, kdim:2 * kdim] = k_ref[...].reshape(nconv, crow, kdim)
    xe_ref[:, SUBLANES:, 2 * kdim:] = v_ref[...].reshape(nconv, crow, kdim)
    cw = cw_ref[...]
    pre = xe_ref[:, SUBLANES:, :] * cw[CONV_W - 1:CONV_W]
    for s in range(1, CONV_W):
        pre = pre + xe_ref[:, SUBLANES - s:SUBLANES - s + crow, :] * cw[CONV_W - 1 - s:CONV_W - s]
    xe_ref[:, 0:SUBLANES, :] = xe_ref[:, crow:crow + SUBLANES, :]
    pre = pre.reshape(rows, 3 * kdim)
    qkv = pre * _sigmoid(pre)

    ba = ba_ref[...]
    beta = _sigmoid(ba[:, 0:heads])
    g = -jnp.exp(alog_ref[...]) * _softplus(ba[:, heads:2 * heads] + dtb_ref[...])
    ii = lax.broadcasted_iota(jnp.int32, (gr, gr), 0)
    jj = lax.broadcasted_iota(jnp.int32, (gr, gr), 1)
    same = (ii // c) == (jj // c)
    incl = jnp.logical_and(same, ii >= jj)
    strict = jnp.logical_and(same, ii > jj)
    eye = (ii == jj).astype(F32)
    incl_b = incl.astype(BF16)
    inclt_b = jnp.logical_and(same, ii <= jj).astype(BF16)
    same_b = same.astype(BF16)
    z = z_ref[...]
    ng = ng_ref[...]
    scale = dk ** -0.5

    gcs, gcts, egcs, ekds, gtots = [], [], [], [], []
    for gi in range(groups):
        gg = g[gi * gr:(gi + 1) * gr]
        gt = -jnp.exp(alogt_ref[...]) * _softplus(bat_ref[0, 0, gi][heads:2 * heads, :] + dtbt_ref[...])
        gc = _dot_exact_lhs(incl_b, gg)
        gtot = _dot_exact_lhs(same_b, gg)
        gcs.append(gc)
        gcts.append(_dot_exact_rhs(gt, inclt_b))
        egcs.append(jnp.exp(gc))
        ekds.append(jnp.exp(gtot - gc))
        gtots.append(gtot)

    k_, kb_, dec_, rhs_, qs_ = {}, {}, {}, {}, {}
    for (gi, h) in units:
        r0 = gi * gr
        qh = qkv[r0:r0 + gr, h * dk:(h + 1) * dk]
        kh = qkv[r0:r0 + gr, kdim + h * dk:kdim + (h + 1) * dk]
        vh = qkv[r0:r0 + gr, 2 * kdim + h * dv:2 * kdim + (h + 1) * dv]
        qh = qh * lax.rsqrt(jnp.sum(qh * qh, axis=-1, keepdims=True) + NORM_EPS)
        kh = kh * lax.rsqrt(jnp.sum(kh * kh, axis=-1, keepdims=True) + NORM_EPS)
        bh = beta[r0:r0 + gr, h:h + 1]
        diff = gcs[gi][:, h:h + 1] - gcts[gi][h:h + 1, :]
        u = (gi, h)
        dec_[u] = jnp.where(incl, jnp.exp(jnp.where(incl, diff, 0.0)), 0.0)
        k_[u] = kh
        kb_[u] = kh * bh
        qs_[u] = qh * scale
        rhs_[u] = jnp.concatenate([vh * bh, kb_[u] * egcs[gi][:, h:h + 1]], axis=1)
    npow = {u: jnp.where(strict, -(_bdot_nt(kb_[u], k_[u]) * dec_[u]), 0.0) for u in units}
    qk_ = {u: _bdot_nt(qs_[u], k_[u]) * dec_[u] for u in units}
    p = {u: eye + npow[u] for u in units}
    lvl = 1
    while 2 * lvl < c:
        npow = {u: _dot3(npow[u], npow[u]) for u in units}
        p = {u: p[u] + _dot3(p[u], npow[u]) for u in units}
        lvl *= 2
    sol = {u: _dot3(p[u], rhs_[u]) for u in units}

    o_ = {}
    if carry:
        assert nseg == 1
        cur = {h: s_ref[h] for h in hs}
        for gi in range(groups):
            wq = {}
            for h in hs:
                u = (gi, h)
                lhs = jnp.concatenate([sol[u][:, dv:], qs_[u] * egcs[gi][:, h:h + 1]], axis=0)
                wq[h] = _bdot(lhs, cur[h])
            for h in hs:
                u = (gi, h)
                vnew = sol[u][:, :dv] - wq[h][:gr]
                o_[u] = wq[h][gr:] + _bdot(qk_[u], vnew)
                kd = k_[u] * ekds[gi][:, h:h + 1]
                glast = jnp.exp(gtots[gi][0:1, h:h + 1])
                cur[h] = cur[h] * glast + _bdot_tn(kd, vnew)
        for h in hs:
            s_ref[h] = cur[h]
        sout_ref[0] = s_ref[...]
    else:
        segs = [(gi, j, h) for gi in range(groups) for j in range(nseg) for h in hs]
        wq = {}
        for (gi, j, h) in segs:
            u = (gi, h)
            sl = slice(j * c, (j + 1) * c)
            lhs = jnp.concatenate([sol[u][sl, dv:], qs_[u][sl] * egcs[gi][sl, h:h + 1]], axis=0)
            wq[(gi, j, h)] = _bdot(lhs, s0_ref[gi * nseg + j, h])
        vn = {}
        for (gi, j, h) in segs:
            u = (gi, h)
            sl = slice(j * c, (j + 1) * c)
            vn[(gi, j, h)] = sol[u][sl, :dv] - wq[(gi, j, h)][:c]
            kd = k_[u][sl] * ekds[gi][sl, h:h + 1]
            glast = jnp.exp(gtots[gi][j * c:j * c + 1, h:h + 1])
            sout_ref[gi * nseg + j, h] = s0_ref[gi * nseg + j, h] * glast + _bdot_tn(kd, vn[(gi, j, h)])
        for u in units:
            gi, h = u
            vnew = jnp.concatenate([vn[(gi, j, h)] for j in range(nseg)], axis=0)
            os_ = jnp.concatenate([wq[(gi, j, h)][c:] for j in range(nseg)], axis=0)
            o_[u] = os_ + _bdot(qk_[u], vnew)

    ys = []
    for gi in range(groups):
        row = []
        for h in hs:
            zz = z[gi * gr:(gi + 1) * gr, h * dv:(h + 1) * dv]
            row.append(_rms(o_[(gi, h)], ng) * (zz * _sigmoid(zz)))
        ys.append(jnp.concatenate(row, axis=1))
    y_ref[...] = jnp.concatenate(ys, axis=0)


def _gdn(proj, col0, row0, batch, seq, c, groups, gr, carry, ba, tail0, s0, cw, alog, dtb, ng, heads, dk, dv):
    kdim = heads * dk
    rows = groups * gr
    if carry:
        nb, nsteps, sb = batch, seq // rows, 1
    else:
        sb = rows // seq
        nb, nsteps = batch // sb, 1
    crow = rows // sb
    rb0 = row0 // rows
    cb0 = col0 // kdim
    rblk = lambda b, ci: rb0 + b * nsteps + ci
    small = lambda shape: pl.BlockSpec(shape, lambda b, ci: (0,) * len(shape))
    bat = ba[row0:row0 + batch * seq].reshape(nb, nsteps, groups, gr, 2 * heads)
    bat = jnp.swapaxes(bat, 3, 4)
    return pl.pallas_call(
        functools.partial(_gdn_kernel, groups=groups, gr=gr, c=c, carry=carry, heads=heads, dk=dk, dv=dv),
        grid=(nb, nsteps),
        in_specs=[pl.BlockSpec((rows, kdim), lambda b, ci: (rblk(b, ci), cb0)),
                  pl.BlockSpec((rows, kdim), lambda b, ci: (rblk(b, ci), cb0 + 1)),
                  pl.BlockSpec((rows, kdim), lambda b, ci: (rblk(b, ci), cb0 + 2)),
                  pl.BlockSpec((rows, kdim), lambda b, ci: (rblk(b, ci), cb0 + 3)),
                  pl.BlockSpec((rows, 2 * heads), lambda b, ci: (rblk(b, ci), 0)),
                  pl.BlockSpec((1, 1, groups, 2 * heads, gr), lambda b, ci: (b, ci, 0, 0, 0)),
                  pl.BlockSpec((sb, SUBLANES, 3 * kdim), lambda b, ci: (b, 0, 0)),
                  pl.BlockSpec((sb, heads, dk, dv), lambda b, ci: (b, 0, 0, 0)),
                  small((CONV_W, 3 * kdim)),
                  small((1, heads)), small((1, heads)), small((heads, 1)), small((heads, 1)),
                  small((1, dv))],
        out_specs=[pl.BlockSpec((rows, heads * dv), lambda b, ci: (b * nsteps + ci, 0)),
                   pl.BlockSpec((sb, heads, dk, dv), lambda b, ci: (b, 0, 0, 0))],
        out_shape=[jax.ShapeDtypeStruct((batch * seq, heads * dv), F32),
                   jax.ShapeDtypeStruct((batch, heads, dk, dv), F32)],
        scratch_shapes=[pltpu.VMEM((sb, SUBLANES + crow, 3 * kdim), F32),
                        pltpu.VMEM((heads, dk, dv), F32)],
        compiler_params=_params("parallel", "arbitrary"),
        name="gdn",
    )(proj, proj, proj, proj, ba, bat, tail0, s0, cw, alog, dtb, alog.T, dtb.T, ng)


def _merge_kernel(h_ref, ya1_ref, ya2_ref, yb1_ref, yb2_ref, wga_ref, wgb_ref, wa_ref, wb_ref, u_ref, *, na):
    d = functools.partial(jnp.dot, preferred_element_type=F32)
    first = pl.program_id(0) < na
    h = h_ref[...]
    ya = jnp.where(first, ya1_ref[...], ya2_ref[...]).astype(BF16)
    yb = jnp.where(first, yb1_ref[...], yb2_ref[...]).astype(BF16)
    u = _sigmoid(d(h, wga_ref[...])) * d(ya, wa_ref[...])
    u = u + _sigmoid(d(h, wgb_ref[...])) * d(yb, wb_ref[...])
    u_ref[...] = u.astype(BF16)


def _merge(h, ya1, ya2, yb1, yb2, w_g, w_a, w_b, tm, tn):
    m, d = h.shape
    n = w_a.shape[1]
    nj = n // tn
    return pl.pallas_call(
        functools.partial(_merge_kernel, na=ya1.shape[0] // tm),
        grid=(m // tm, nj),
        in_specs=[pl.BlockSpec((tm, d), lambda i, j: (i, 0)),
                  *_split_rows(ya1, ya2, tm),
                  *_split_rows(yb1, yb2, tm),
                  pl.BlockSpec((d, tn), lambda i, j: (0, j)),
                  pl.BlockSpec((d, tn), lambda i, j: (0, nj + j)),
                  pl.BlockSpec((w_a.shape[0], tn), lambda i, j: (0, j)),
                  pl.BlockSpec((w_b.shape[0], tn), lambda i, j: (0, j))],
        out_specs=pl.BlockSpec((tm, tn), lambda i, j: (i, j)),
        out_shape=jax.ShapeDtypeStruct((m, n), BF16),
        compiler_params=_params("parallel", "parallel"),
        name="merge",
    )(h, ya1, ya2, yb1, yb2, w_g, w_g, w_a, w_b)


def _out_norm_kernel(u_ref, w_ref, xa_ref, xb_ref, g_ref, x1_ref, h_ref, *, na):
    x = jnp.where(pl.program_id(0) < na, xa_ref[...], xb_ref[...])
    x1 = x + jnp.dot(u_ref[...], w_ref[...], preferred_element_type=F32)
    x1_ref[...] = x1
    h_ref[...] = _rms(x1, g_ref[...]).astype(BF16)


def _out_norm(u, w, xa, xb, g, tm):
    m, k = u.shape
    d = xa.shape[1]
    return pl.pallas_call(
        functools.partial(_out_norm_kernel, na=xa.shape[0] // tm),
        grid=(m // tm,),
        in_specs=[pl.BlockSpec((tm, k), lambda i: (i, 0)),
                  pl.BlockSpec((k, d), lambda i: (0, 0)),
                  *_split_rows(xa, xb, tm),
                  pl.BlockSpec((1, d), lambda i: (0, 0))],
        out_specs=[pl.BlockSpec((tm, d), lambda i: (i, 0)),
                   pl.BlockSpec((tm, d), lambda i: (i, 0))],
        out_shape=[jax.ShapeDtypeStruct((m, d), F32), jax.ShapeDtypeStruct((m, d), BF16)],
        compiler_params=_params("parallel"),
        name="out_norm",
    )(u, w, xa, xb, g)


def _memkv_kernel(m_ref, g_ref, wk_ref, wv_ref, k_ref, v_ref):
    mn = _rms(m_ref[...], g_ref[...]).astype(BF16)
    k_ref[...] = jnp.dot(mn, wk_ref[...], preferred_element_type=F32)
    v_ref[...] = jnp.dot(mn, wv_ref[...], preferred_element_type=F32)


def _memkv(mem, g, wk, wv, tm):
    m, d = mem.shape
    n = wk.shape[1]
    return pl.pallas_call(
        _memkv_kernel,
        grid=(m // tm,),
        in_specs=[pl.BlockSpec((tm, d), lambda i: (i, 0)),
                  pl.BlockSpec((1, d), lambda i: (0, 0)),
                  pl.BlockSpec((d, n), lambda i: (0, 0)),
                  pl.BlockSpec((d, n), lambda i: (0, 0))],
        out_specs=[pl.BlockSpec((tm, n), lambda i: (i, 0)), pl.BlockSpec((tm, n), lambda i: (i, 0))],
        out_shape=[jax.ShapeDtypeStruct((m, n), F32), jax.ShapeDtypeStruct((m, n), F32)],
        compiler_params=_params("parallel"),
        name="memkv",
    )(mem, g, wk, wv)


def _attn_kernel(q_ref, k_ref, v_ref, o_ref, *, nb, tq, heads, hd, interleaved):
    mem = k_ref.shape[1] // heads if interleaved else k_ref.shape[1]
    scale = hd ** -0.5
    units = [(b, h) for b in range(nb) for h in range(heads)]

    def head(ref, b, h):
        if interleaved:
            return ref[b, pl.ds(h, mem, stride=heads), :].astype(BF16)
        return ref[b, :, h * hd:(h + 1) * hd].astype(BF16)

    q = q_ref[...].astype(BF16)
    s = {(b, h): lax.dot_general(q[b * tq:(b + 1) * tq, h * hd:(h + 1) * hd], head(k_ref, b, h),
                                 (((1,), (1,)), ((), ())), preferred_element_type=F32) * scale for (b, h) in units}
    p = {u: jnp.exp(s[u] - jnp.max(s[u], axis=-1, keepdims=True)) for u in units}
    o = {(b, h): jnp.dot(p[(b, h)].astype(BF16), head(v_ref, b, h), preferred_element_type=F32)
         / jnp.sum(p[(b, h)], axis=-1, keepdims=True) for (b, h) in units}
    o_ref[...] = jnp.concatenate([jnp.concatenate([o[(b, h)] for h in range(heads)], axis=1) for b in range(nb)],
                                 axis=0)


def _attn(q, row0, batch, seq, nb, tq, k, v, heads, hd, interleaved):
    nt = seq // tq
    rows = nb * tq
    rb0 = row0 // rows
    n = heads * hd
    return pl.pallas_call(
        functools.partial(_attn_kernel, nb=nb, tq=tq, heads=heads, hd=hd, interleaved=interleaved),
        grid=(batch // nb, nt),
        in_specs=[pl.BlockSpec((rows, n), lambda b, t: (rb0 + b * nt + t, 0)),
                  pl.BlockSpec((nb,) + k.shape[1:], lambda b, t: (b, 0, 0)),
                  pl.BlockSpec((nb,) + v.shape[1:], lambda b, t: (b, 0, 0))],
        out_specs=pl.BlockSpec((rows, n), lambda b, t: (b * nt + t, 0)),
        out_shape=jax.ShapeDtypeStruct((batch * seq, n), F32),
        compiler_params=_params("parallel", "parallel"),
        name="attn",
    )(q, k, v)


def _xo_router_kernel(oa_ref, ob_ref, wo_ref, x1_ref, g_ref, rw_ref, rb_ref, x2_ref, h_ref, lg_ref, *, na):
    o = jnp.where(pl.program_id(0) < na, oa_ref[...], ob_ref[...]).astype(BF16)
    x2 = x1_ref[...] + jnp.dot(o, wo_ref[...], preferred_element_type=F32)
    x2_ref[...] = x2
    h = _rms(x2, g_ref[...])
    h_ref[...] = h
    lg_ref[...] = _dot3(h, rw_ref[...]) + rb_ref[...]


def _xo_router(oa, ob, wo, x1, g, rw, rb, tm):
    m, d = x1.shape
    ne = rw.shape[1]
    row = lambda w: pl.BlockSpec((tm, w), lambda i: (i, 0))
    full = lambda a: pl.BlockSpec(a.shape, lambda i: (0, 0))
    return pl.pallas_call(
        functools.partial(_xo_router_kernel, na=oa.shape[0] // tm),
        grid=(m // tm,),
        in_specs=[*_split_rows(oa, ob, tm), full(wo), row(d), full(g), full(rw), full(rb)],
        out_specs=[row(d), row(d), row(ne)],
        out_shape=[jax.ShapeDtypeStruct((m, d), F32), jax.ShapeDtypeStruct((m, d), F32),
                   jax.ShapeDtypeStruct((m, ne), F32)],
        compiler_params=_params("parallel"),
        name="xo_router",
    )(oa, ob, wo, x1, g, rw, rb)


def _moe_kernel(e_ref, r0_ref, n_ref, tok_ref, dst_ref, h_hbm, wg_ref, wu_ref, wd_ref, bg_ref, bu_ref, bd_ref,
                slots_hbm, xbuf, acc, wgb, wub, wdb, gsem, ssem, *, sub, big, nf):
    s = pl.program_id(0)
    f = pl.program_id(1)
    n = n_ref[s]
    r0 = r0_ref[s]
    d = acc.shape[2]

    @pl.when(jnp.logical_and(s == 0, f == 0))
    def _():
        xbuf[...] = jnp.zeros_like(xbuf)

    def row_in(c, k):
        tok = tok_ref[r0 + c * SUBLANES + k]
        return pltpu.make_async_copy(h_hbm.at[pl.ds(tok, 1), :], xbuf.at[c, pl.ds(k, 1), :], gsem)

    def row_out(c, k):
        dst = dst_ref[r0 + c * SUBLANES + k]
        return pltpu.make_async_copy(acc.at[c, pl.ds(k, 1), :], slots_hbm.at[pl.ds(dst, 1), :], ssem)

    def for_rows(fn):
        nfull = n // SUBLANES

        def group(c, carry):
            for k in range(SUBLANES):
                fn(c, k)
            return carry

        lax.fori_loop(0, nfull, group, 0)
        lax.fori_loop(0, n - nfull * SUBLANES, lambda k, carry: (fn(nfull, k), carry)[1], 0)

    @pl.when(n > 0)
    def _active():
        @pl.when(f == 0)
        def _gather():
            for_rows(lambda c, k: row_in(c, k).start())
            for_rows(lambda c, k: row_in(c, k).wait())

        wgb[...] = wg_ref[0].astype(BF16)
        wub[...] = wu_ref[0].astype(BF16)
        wdb[...] = wd_ref[0].astype(BF16)
        bg = bg_ref[0]
        bu = bu_ref[0]
        bd = bd_ref[0]

        nsub = (n + sub - 1) // sub
        sub8 = sub // SUBLANES
        big8 = big // SUBLANES

        @pl.when(f == 0)
        def _init():
            def fill(j, carry):
                acc[pl.ds(pl.multiple_of(j * sub8, sub8), sub8)] = jnp.broadcast_to(bd, (sub8, SUBLANES, d))
                return carry

            lax.fori_loop(0, nsub, fill, 0)

        def block(off8, m):
            m8 = m // SUBLANES
            x = xbuf[pl.ds(off8, m8)].reshape(m, d).astype(BF16)
            gate = jnp.dot(x, wgb[...], preferred_element_type=F32) + bg
            up = jnp.dot(x, wub[...], preferred_element_type=F32) + bu
            gate = jnp.minimum(gate, SWIGLU_LIMIT)
            up = jnp.clip(up, -SWIGLU_LIMIT, SWIGLU_LIMIT)
            act = gate * _sigmoid(SWIGLU_ALPHA * gate) * (up + 1.0)
            part = jnp.dot(act.astype(BF16), wdb[...], preferred_element_type=F32)
            acc[pl.ds(off8, m8)] += part.reshape(m8, SUBLANES, d)

        per_big = big // sub
        nbig = nsub // per_big
        tail = nsub - nbig * per_big

        def big_block(j, carry):
            block(pl.multiple_of(j * big8, big8), big)
            return carry

        lax.fori_loop(0, nbig, big_block, 0)
        for t in range(1, per_big):
            @pl.when(tail == t)
            def _tail(t=t):
                block(pl.multiple_of(nbig * big8, sub8), t * sub)

        @pl.when(f == nf - 1)
        def _scatter():
            for_rows(lambda c, k: row_out(c, k).start())
            for_rows(lambda c, k: row_out(c, k).wait())


def _moe(h, sb_e, sb_r0, sb_n, tok, dst, wg, wu, wd, bg, bu, bd, cap, sub, big, tf):
    n_tok, d = h.shape
    ne, _, dff = wg.shape
    nf = dff // tf
    g = sb_e.shape[0]
    fi = lambda s, f, n_ref: jnp.where(n_ref[s] > 0, f, nf - 1)
    grid_spec = pltpu.PrefetchScalarGridSpec(
        num_scalar_prefetch=5,
        grid=(g, nf),
        in_specs=[pl.BlockSpec(memory_space=pl.ANY),
                  pl.BlockSpec((1, d, tf), lambda s, f, e, r, n, t, o: (e[s], 0, fi(s, f, n))),
                  pl.BlockSpec((1, d, tf), lambda s, f, e, r, n, t, o: (e[s], 0, fi(s, f, n))),
                  pl.BlockSpec((1, tf, d), lambda s, f, e, r, n, t, o: (e[s], fi(s, f, n), 0)),
                  pl.BlockSpec((1, 1, tf), lambda s, f, e, r, n, t, o: (e[s], 0, fi(s, f, n))),
                  pl.BlockSpec((1, 1, tf), lambda s, f, e, r, n, t, o: (e[s], 0, fi(s, f, n))),
                  pl.BlockSpec((1, 1, d), lambda s, f, e, r, n, t, o: (e[s], 0, 0))],
        out_specs=pl.BlockSpec(memory_space=pl.ANY),
        scratch_shapes=[pltpu.VMEM((cap // SUBLANES, SUBLANES, d), F32),
                        pltpu.VMEM((cap // SUBLANES, SUBLANES, d), F32),
                        pltpu.VMEM((d, tf), BF16), pltpu.VMEM((d, tf), BF16), pltpu.VMEM((tf, d), BF16),
                        pltpu.SemaphoreType.DMA(()), pltpu.SemaphoreType.DMA(())],
    )
    return pl.pallas_call(
        functools.partial(_moe_kernel, sub=sub, big=big, nf=nf),
        grid_spec=grid_spec,
        out_shape=jax.ShapeDtypeStruct((n_tok * TOP_K, d), F32),
        compiler_params=_params("arbitrary", "arbitrary"),
        name="moe",
    )(sb_e, sb_r0, sb_n, tok, dst, h, wg, wu, wd, bg.reshape(ne, 1, dff), bu.reshape(ne, 1, dff),
      bd.reshape(ne, 1, d))


def _route(logits, cap, n_sb_max):
    n_tok, ne = logits.shape
    top_val, top_idx = lax.top_k(logits, TOP_K)
    gates = jax.nn.softmax(top_val, axis=-1)
    e_flat = top_idx.reshape(-1).astype(jnp.int32)
    codes = jnp.argsort(e_flat, stable=True).astype(jnp.int32)
    counts = jnp.zeros((ne,), jnp.int32).at[e_flat].add(1)
    start = jnp.cumsum(counts) - counts
    n_sb = (counts + cap - 1) // cap
    sb_end = jnp.cumsum(n_sb)
    sb_start = sb_end - n_sb
    total = sb_end[-1]
    s = jnp.arange(n_sb_max, dtype=jnp.int32)
    e_s = jnp.minimum(jnp.searchsorted(sb_end, s, side='right'), ne - 1).astype(jnp.int32)
    e_last = e_s[jnp.maximum(total - 1, 0)]
    active = s < total
    e_s = jnp.where(active, e_s, e_last)
    within = s - sb_start[e_s]
    r0 = jnp.where(active, start[e_s] + within * cap, 0).astype(jnp.int32)
    n = jnp.where(active, jnp.clip(counts[e_s] - within * cap, 0, cap), 0).astype(jnp.int32)
    tok = lax.shift_right_logical(codes, TOP_K_SHIFT)
    dst = (codes & (TOP_K - 1)) * n_tok + tok
    return gates, tok, dst, e_s, r0, n


def _final_kernel(x2_ref, s0_ref, s1_ref, s2_ref, s3_ref, gates_ref, g_ref, y_ref):
    x = x2_ref[...]
    gt = gates_ref[...]
    for k, s_ref in enumerate((s0_ref, s1_ref, s2_ref, s3_ref)):
        x = x + s_ref[...] * gt[:, k:k + 1]
    y_ref[...] = _rms(x, g_ref[...])


def _final(x2, slots, gates, g, row0, rows, tm):
    n_tok, d = x2.shape
    rb0 = row0 // tm
    nblk = n_tok // tm
    slot = lambda k: pl.BlockSpec((tm, d), lambda i: (k * nblk + rb0 + i, 0))
    return pl.pallas_call(
        _final_kernel,
        grid=(rows // tm,),
        in_specs=[pl.BlockSpec((tm, d), lambda i: (rb0 + i, 0)),
                  slot(0), slot(1), slot(2), slot(3),
                  pl.BlockSpec((tm, TOP_K), lambda i: (rb0 + i, 0)),
                  pl.BlockSpec((1, d), lambda i: (0, 0))],
        out_specs=pl.BlockSpec((tm, d), lambda i: (i, 0)),
        out_shape=jax.ShapeDtypeStruct((rows, d), F32),
        compiler_params=_params("parallel"),
        name="final",
    )(x2, slots, slots, slots, slots, gates, g)


def _pad_tail(buf):
    return jnp.pad(buf, ((0, 0), (SUBLANES - (CONV_W - 1), 0), (0, 0)))


def kernel(x_prompt, x_sample, mem_prompt, state_lru_conv, state_lru_h, state_gdn_conv, state_gdn_S, cache_mem_k, cache_mem_v, norm_mix_g, w_in, lru_conv_w, lru_conv_b, lru_wx, lru_bx, lru_wa, lru_ba, lru_a_param, gdn_conv_w, gdn_A_log, gdn_dt_bias, gdn_norm_g, w_branch_a, w_branch_b, w_out, norm_xa_g, norm_mem_g, xa_wq, xa_wk, xa_wv, xa_wo, norm_moe_g, router_w, router_b, moe_w_gate, moe_b_gate, moe_w_up, moe_b_up, moe_w_down, moe_b_down, norm_final_g):
    depth = w_in.shape[0]
    assert depth == 1
    bp, tp, d = x_prompt.shape
    bs, ts, _ = x_sample.shape
    np_, ns = bp * tp, bs * ts
    n_tok = np_ + ns
    lw = lru_conv_w.shape[2]
    heads, dk, dv = state_gdn_S.shape[2:]
    kdim = heads * dk
    mem_len = mem_prompt.shape[1]
    xa_heads, xa_hd = cache_mem_k.shape[3:]
    xa_dim = xa_heads * xa_hd
    ne = router_w.shape[2]
    past_len = 16384
    gdn_chunk = 64

    row = lambda v: v.reshape(1, -1)
    xp = x_prompt.reshape(np_, d)
    xs = x_sample.reshape(ns, d)

    c_main = 2 * lw + 3 * kdim + heads * dv
    w_in0 = w_in[0]
    w_main = w_in0[:, :c_main].astype(BF16)
    w_ba = w_in0[:, c_main:c_main + 2 * heads].astype(BF16)
    w_g = w_in0[:, c_main + 2 * heads:].astype(BF16)
    w_gates = jnp.concatenate([lru_wx[0], lru_wa[0]], axis=-1).astype(BF16)

    rows_gcd = math.gcd(np_, ns)
    tm_s = _tile(rows_gcd, 256)
    tm_m = _tile(rows_gcd, 512)
    tm_l = _tile(rows_gcd, 1024)
    h1, ba = _norm_ba(xp, xs, row(norm_mix_g[0]), w_ba, tm_m)
    proj = _matmul(h1, w_main, c_main, tm_l, _tile(c_main, 1024), F32, "in_proj")

    lru_args = (lru_conv_w[0], row(lru_conv_b[0]), w_gates, row(lru_bx[0]), row(lru_ba[0]), row(lru_a_param[0]))
    ya_p, lru_h_p = _lru_seq(proj, 0, bp, tp, _tile(tp, 256), jnp.zeros((bp, SUBLANES, lw), F32),
                             jnp.zeros((bp, 1, lw), F32), *lru_args, pos0=0)
    if ts == SUBLANES:
        ya_s, lru_h_s = _lru_short(proj, np_, bs, math.gcd(bs, 32), _pad_tail(state_lru_conv[0]),
                                   state_lru_h[0][:, None, :], *lru_args)
    else:
        ya_s, lru_h_s = _lru_seq(proj, np_, bs, ts, ts, _pad_tail(state_lru_conv[0]), state_lru_h[0][:, None, :],
                                 *lru_args, pos0=past_len)

    gdn_args = (gdn_conv_w[0], row(gdn_A_log[0]), row(gdn_dt_bias[0]), row(gdn_norm_g[0]), heads, dk, dv)
    cp = min(gdn_chunk, tp)
    assert tp % cp == 0 and ts <= gdn_chunk
    gp = 2 if (tp // cp) % 2 == 0 else 1
    yb_p, gdn_s_p = _gdn(proj, 2 * lw, 0, bp, tp, cp, gp, cp, True, ba,
                         jnp.zeros((bp, SUBLANES, 3 * kdim), F32), jnp.zeros((bp, heads, dk, dv), F32), *gdn_args)
    gr_s = ts * math.gcd(bs, max(gdn_chunk // ts, 1))
    gs = 2 if (ns // gr_s) % 2 == 0 else 1
    yb_s, gdn_s_s = _gdn(proj, 2 * lw, np_, bs, ts, ts, gs, gr_s, False, ba,
                         _pad_tail(state_gdn_conv[0]), state_gdn_S[0], *gdn_args)

    u = _merge(h1, ya_p, ya_s, yb_p, yb_s, w_g, w_branch_a[0].astype(BF16), w_branch_b[0].astype(BF16), tm_m,
               _tile(d, 512))
    x1, h2 = _out_norm(u, w_out[0].astype(BF16), xp, xs, row(norm_xa_g[0]), tm_m)

    k_p, v_p = _memkv(mem_prompt.reshape(bp * mem_len, d), row(norm_mem_g[0]), xa_wk[0].astype(BF16),
                      xa_wv[0].astype(BF16), _tile(bp * mem_len, 256))
    q = _matmul(h2, xa_wq[0].astype(BF16), xa_dim, tm_l, xa_dim, F32, "xa_q")
    o_p = _attn(q, 0, bp, tp, 1, _tile(tp, 512), k_p.reshape(bp, mem_len, xa_dim), v_p.reshape(bp, mem_len, xa_dim),
                xa_heads, xa_hd, False)
    nb_s = math.gcd(bs, 4)
    o_s = _attn(q, np_, bs, ts, nb_s, ts, cache_mem_k[0].reshape(bs, mem_len * xa_heads, xa_hd),
                cache_mem_v[0].reshape(bs, mem_len * xa_heads, xa_hd), xa_heads, xa_hd, True)
    x2, h3, logits = _xo_router(o_p, o_s, xa_wo[0].astype(BF16), x1, row(norm_moe_g[0]), router_w[0],
                                row(router_b[0]), tm_m)

    cap, sub, big, tf = 1536, 128, 512, _tile(moe_w_gate.shape[3], 256)
    n_sb_max = (n_tok * TOP_K + ne * (cap - 1)) // cap
    gates, tok, dst, sb_e, sb_r0, sb_n = _route(logits, cap, n_sb_max)
    slots = _moe(h3, sb_e, sb_r0, sb_n, tok, dst, moe_w_gate[0], moe_w_up[0], moe_w_down[0],
                 moe_b_gate[0], moe_b_up[0], moe_b_down[0], cap, sub, big, tf)
    y_p =_final(x2, slots, gates, row(norm_final_g), 0, np_, tm_s)
    y_s = _final(x2, slots, gates, row(norm_final_g), np_, ns, tm_s)

    keep = CONV_W - 1
    pp = jnp.stack([lax.slice(proj, ((b + 1) * tp - keep, 0), ((b + 1) * tp, c_main)) for b in range(bp)])
    ps = proj[np_:].reshape(bs, ts, c_main)
    return (y_p.reshape(bp, tp, d), y_s.reshape(bs, ts, d),
            pp[:, :, :lw][None], lru_h_p.reshape(1, bp, lw),
            pp[:, :, 2 * lw:2 * lw + 3 * kdim][None], gdn_s_p[None],
            k_p.reshape(1, bp, mem_len, xa_heads, xa_hd), v_p.reshape(1, bp, mem_len, xa_heads, xa_hd),
            ps[:, ts - keep:, :lw][None], lru_h_s.reshape(1, bs, lw),
            ps[:, ts - keep:, 2 * lw:2 * lw + 3 * kdim][None], gdn_s_s[None])
```

```python
import functools
import math

import jax
import jax.numpy as jnp
from jax import lax
from jax.experimental import pallas as pl
from jax.experimental.pallas import tpu as pltpu

F32 = jnp.float32
BF16 = jnp.bfloat16
NORM_EPS = 1e-6
LRU_C = 8.0
CONV_W = 4
SWIGLU_ALPHA = 1.702
SWIGLU_LIMIT = 7.0
TOP_K = 4
TOP_K_SHIFT = 2
VMEM_LIMIT_BYTES = 56 * 1024 * 1024
SUBLANES = 8
LANES = 128


def _params(*sem):
    return pltpu.CompilerParams(dimension_semantics=sem, vmem_limit_bytes=VMEM_LIMIT_BYTES)


def _bdot(a, b):
    return jnp.dot(a.astype(BF16), b.astype(BF16), preferred_element_type=F32)


def _bdot_nt(a, b):
    return lax.dot_general(a.astype(BF16), b.astype(BF16), (((1,), (1,)), ((), ())), preferred_element_type=F32)


def _bdot_tn(a, b):
    return lax.dot_general(a.astype(BF16), b.astype(BF16), (((0,), (0,)), ((), ())), preferred_element_type=F32)


def _split2(x):
    hi = x.astype(BF16)
    lo = (x - hi.astype(F32)).astype(BF16)
    return hi, lo


def _split3(x):
    p1 = x.astype(BF16)
    r1 = x - p1.astype(F32)
    p2 = r1.astype(BF16)
    p3 = (r1 - p2.astype(F32)).astype(BF16)
    return p1, p2, p3


def _dot3(a, b):
    ah, al = _split2(a)
    bh, bl = _split2(b)
    d = functools.partial(jnp.dot, preferred_element_type=F32)
    return d(ah, bh) + (d(ah, bl) + d(al, bh))


def _dot_exact_lhs(m_bf16, x):
    d = functools.partial(jnp.dot, preferred_element_type=F32)
    p1, p2, p3 = _split3(x)
    return d(m_bf16, p1) + (d(m_bf16, p2) + d(m_bf16, p3))


def _dot_exact_rhs(x, m_bf16):
    d = functools.partial(jnp.dot, preferred_element_type=F32)
    p1, p2, p3 = _split3(x)
    return d(p1, m_bf16) + (d(p2, m_bf16) + d(p3, m_bf16))


def _sigmoid(x):
    return jax.nn.sigmoid(x)


def _tile(n, pref):
    if n <= pref:
        return n
    t = pref - pref % SUBLANES
    while n % t:
        t -= SUBLANES
    return t


def _softplus(x):
    return jnp.maximum(x, 0.0) + jnp.log1p(jnp.exp(-jnp.abs(x)))


def _gelu_tanh(x):
    return 0.5 * x * (1.0 + jnp.tanh(0.7978845608028654 * (x + 0.044715 * (x * x * x))))


def _rms(x, g):
    r = lax.rsqrt(jnp.mean(x * x, axis=-1, keepdims=True) + NORM_EPS)
    return x * r * g


def _split_rows(a, b, tm):
    na = a.shape[0] // tm
    w = a.shape[1]
    return (pl.BlockSpec((tm, w), lambda i, *_: (jnp.minimum(i, na - 1), 0)),
            pl.BlockSpec((tm, w), lambda i, *_: (jnp.maximum(i - na, 0), 0)))


def _norm_ba_kernel(xa_ref, xb_ref, g_ref, wba_ref, h_ref, ba_ref, *, na):
    x = jnp.where(pl.program_id(0) < na, xa_ref[...], xb_ref[...])
    h = _rms(x, g_ref[...]).astype(BF16)
    h_ref[...] = h
    ba_ref[...] = jnp.dot(h, wba_ref[...], preferred_element_type=F32)


def _norm_ba(xa, xb, g, w_ba, tm):
    d = xa.shape[1]
    m = xa.shape[0] + xb.shape[0]
    nb = w_ba.shape[1]
    return pl.pallas_call(
        functools.partial(_norm_ba_kernel, na=xa.shape[0] // tm),
        grid=(m // tm,),
        in_specs=[*_split_rows(xa, xb, tm),
                  pl.BlockSpec((1, d), lambda i: (0, 0)),
                  pl.BlockSpec((d, nb), lambda i: (0, 0))],
        out_specs=[pl.BlockSpec((tm, d), lambda i: (i, 0)),
                   pl.BlockSpec((tm, nb), lambda i: (i, 0))],
        out_shape=[jax.ShapeDtypeStruct((m, d), BF16), jax.ShapeDtypeStruct((m, nb), F32)],
        compiler_params=_params("parallel"),
        name="norm_ba",
    )(xa, xb, g, w_ba)


def _mm_kernel(a_ref, w_ref, o_ref):
    o_ref[...] = jnp.dot(a_ref[...], w_ref[...].astype(BF16), preferred_element_type=F32).astype(o_ref.dtype)


def _matmul(a, w, n_cols, tm, tn, out_dtype, name):
    m, k = a.shape
    return pl.pallas_call(
        _mm_kernel,
        grid=(n_cols // tn, m // tm),
        in_specs=[pl.BlockSpec((tm, k), lambda j, i: (i, 0)),
                  pl.BlockSpec((k, tn), lambda j, i: (0, j))],
        out_specs=pl.BlockSpec((tm, tn), lambda j, i: (i, j)),
        out_shape=jax.ShapeDtypeStruct((m, n_cols), out_dtype),
        compiler_params=_params("parallel", "parallel"),
        name=name,
    )(a, w)


def _conv4_into(xe_ref, rows, cw):
    y = xe_ref[SUBLANES:SUBLANES + rows, :] * cw[CONV_W - 1:CONV_W]
    for s in range(1, CONV_W):
        y = y + xe_ref[SUBLANES - s:SUBLANES - s + rows, :] * cw[CONV_W - 1 - s:CONV_W - s]
    return y


def _scan8(a3, b3):
    row = lax.broadcasted_iota(jnp.int32, a3.shape, 1)
    for s in (1, 2, 4):
        a_sh = pltpu.roll(a3, s, axis=1)
        b_sh = pltpu.roll(b3, s, axis=1)
        m = row >= s
        b3 = jnp.where(m, a3 * b_sh + b3, b3)
        a3 = jnp.where(m, a3 * a_sh, a3)
    return a3, b3


def _lru_gates(xc, wg_ref, bx, bga, ap, first_row_is_pos0):
    nblk = wg_ref.shape[0]
    bw = wg_ref.shape[1]
    xcb = xc.astype(BF16)
    gx, ga = [], []
    for n in range(nblk):
        r = jnp.dot(xcb[:, n * bw:(n + 1) * bw], wg_ref[n], preferred_element_type=F32)
        gx.append(r[:, :bw])
        ga.append(r[:, bw:])
    gate_x = _sigmoid(jnp.concatenate(gx, axis=1) + bx)
    gate_a = _sigmoid(jnp.concatenate(ga, axis=1) + bga)
    log_a = (-LRU_C) * gate_a * _softplus(ap)
    a = jnp.exp(log_a)
    mult = jnp.sqrt(-jnp.tanh(log_a) * (a * a + 1.0))
    if first_row_is_pos0 is not None:
        mult = jnp.where(first_row_is_pos0, 1.0, mult)
    return a, xc * gate_x * mult


def _lru_seq_kernel(xa_ref, ya_ref, tail0_ref, h0_ref, cw_ref, cb_ref, wg_ref, bx_ref, bga_ref, ap_ref,
                    y_ref, hl_ref, xe_ref, a_ref, b_ref, h_ref, *, tt, pos0):
    t = pl.program_id(1)
    w = xa_ref.shape[1]

    @pl.when(t == 0)
    def _():
        xe_ref[0:SUBLANES, :] = tail0_ref[0]
        h_ref[...] = h0_ref[0]

    xe_ref[SUBLANES:SUBLANES + tt, :] = xa_ref[...]
    xc = _conv4_into(xe_ref, tt, cw_ref[...]) + cb_ref[...]
    xe_ref[0:SUBLANES, :] = xe_ref[tt:tt + SUBLANES, :]
    first = None
    if pos0 == 0:
        first = (lax.broadcasted_iota(jnp.int32, (tt, 1), 0) + t * tt) == 0
    a, b = _lru_gates(xc, wg_ref, bx_ref[...], bga_ref[...], ap_ref[...], first)
    a3, b3 = _scan8(a.reshape(tt // SUBLANES, SUBLANES, w), b.reshape(tt // SUBLANES, SUBLANES, w))
    a_ref[...] = a3
    b_ref[...] = b3

    def body(g, h):
        hg = a_ref[g] * h + b_ref[g]
        b_ref[g] = hg
        return hg[SUBLANES - 1:SUBLANES, :]

    h = lax.fori_loop(0, tt // SUBLANES, body, h_ref[...])
    h_ref[...] = h
    hl_ref[0] = h
    hs = b_ref[...].reshape(tt, w)
    y_ref[...] = hs * _gelu_tanh(ya_ref[...])


def _lru_seq(proj, row0, batch, seq, tt, tail0, h0, cw, cb, wg, bx, bga, ap, pos0):
    w = cw.shape[1]
    nt = seq // tt
    rb0 = row0 // tt
    vec = lambda: pl.BlockSpec((1, w), lambda b, t: (0, 0))
    return pl.pallas_call(
        functools.partial(_lru_seq_kernel, tt=tt, pos0=pos0),
        grid=(batch, nt),
        in_specs=[pl.BlockSpec((tt, w), lambda b, t: (rb0 + b * nt + t, 0)),
                  pl.BlockSpec((tt, w), lambda b, t: (rb0 + b * nt + t, 1)),
                  pl.BlockSpec((1, SUBLANES, w), lambda b, t: (b, 0, 0)),
                  pl.BlockSpec((1, 1, w), lambda b, t: (b, 0, 0)),
                  pl.BlockSpec((CONV_W, w), lambda b, t: (0, 0)),
                  vec(),
                  pl.BlockSpec(wg.shape, lambda b, t: (0, 0, 0)),
                  vec(), vec(), vec()],
        out_specs=[pl.BlockSpec((tt, w), lambda b, t: (b * nt + t, 0)),
                   pl.BlockSpec((1, 1, w), lambda b, t: (b, 0, 0))],
        out_shape=[jax.ShapeDtypeStruct((batch * seq, w), F32),
                   jax.ShapeDtypeStruct((batch, 1, w), F32)],
        scratch_shapes=[pltpu.VMEM((SUBLANES + tt, w), F32),
                        pltpu.VMEM((tt // SUBLANES, SUBLANES, w), F32),
                        pltpu.VMEM((tt // SUBLANES, SUBLANES, w), F32),
                        pltpu.VMEM((1, w), F32)],
        compiler_params=_params("parallel", "arbitrary"),
        name="lru_seq",
    )(proj, proj, tail0, h0, cw, cb, wg, bx, bga, ap)


def _lru_short_kernel(xa_ref, ya_ref, tail0_ref, h0_ref, cw_ref, cb_ref, wg_ref, bx_ref, bga_ref, ap_ref,
                      y_ref, hl_ref, xe_ref):
    nb = xe_ref.shape[0]
    w = xa_ref.shape[1]
    xe_ref[:, 0:SUBLANES, :] = tail0_ref[...]
    xe_ref[:, SUBLANES:, :] = xa_ref[...].reshape(nb, SUBLANES, w)
    cw = cw_ref[...]
    xc = xe_ref[:, SUBLANES:, :] * cw[CONV_W - 1:CONV_W]
    for s in range(1, CONV_W):
        xc = xc + xe_ref[:, SUBLANES - s:2 * SUBLANES - s, :] * cw[CONV_W - 1 - s:CONV_W - s]
    xc = xc.reshape(nb * SUBLANES, w) + cb_ref[...]
    a, b = _lru_gates(xc, wg_ref, bx_ref[...], bga_ref[...], ap_ref[...], None)
    a3, b3 = _scan8(a.reshape(nb, SUBLANES, w), b.reshape(nb, SUBLANES, w))
    hs = a3 * h0_ref[...] + b3
    hl_ref[...] = hs[:, SUBLANES - 1:SUBLANES, :]
    y_ref[...] = hs.reshape(nb * SUBLANES, w) * _gelu_tanh(ya_ref[...])


def _lru_short(proj, row0, batch, nb, tail0, h0, cw, cb, wg, bx, bga, ap):
    w = cw.shape[1]
    rows = nb * SUBLANES
    rb0 = row0 // rows
    vec = lambda: pl.BlockSpec((1, w), lambda b: (0, 0))
    return pl.pallas_call(
        _lru_short_kernel,
        grid=(batch // nb,),
        in_specs=[pl.BlockSpec((rows, w), lambda b: (rb0 + b, 0)),
                  pl.BlockSpec((rows, w), lambda b: (rb0 + b, 1)),
                  pl.BlockSpec((nb, SUBLANES, w), lambda b: (b, 0, 0)),
                  pl.BlockSpec((nb, 1, w), lambda b: (b, 0, 0)),
                  pl.BlockSpec((CONV_W, w), lambda b: (0, 0)),
                  vec(),
                  pl.BlockSpec(wg.shape, lambda b: (0, 0, 0)),
                  vec(), vec(), vec()],
        out_specs=[pl.BlockSpec((rows, w), lambda b: (b, 0)),
                   pl.BlockSpec((nb, 1, w), lambda b: (b, 0, 0))],
        out_shape=[jax.ShapeDtypeStruct((batch * SUBLANES, w), F32),
                   jax.ShapeDtypeStruct((batch, 1, w), F32)],
        scratch_shapes=[pltpu.VMEM((nb, 2 * SUBLANES, w), F32)],
        compiler_params=_params("parallel"),
        name="lru_short",
    )(proj, proj, tail0, h0, cw, cb, wg, bx, bga, ap)


def _gdn_kernel(q_ref, k_ref, v_ref, z_ref, ba_ref, bat_ref, tail0_ref, s0_ref, cw_ref, alog_ref, dtb_ref,
                alogt_ref, dtbt_ref, ng_ref, y_ref, sout_ref, xe_ref, s_ref, *, groups, gr, c, carry, heads, dk, dv):
    ci = pl.program_id(1)
    kdim = heads * dk
    rows = groups * gr
    nseg = gr // c
    nconv, crow = xe_ref.shape[0], xe_ref.shape[1] - SUBLANES
    hs = range(heads)
    units = [(gi, h) for gi in range(groups) for h in hs]

    @pl.when(ci == 0)
    def _():
        xe_ref[:, 0:SUBLANES, :] = tail0_ref[...]
        if carry:
            s_ref[...] = s0_ref[0]

    xe_ref[:, SUBLANES:, 0:kdim] = q_ref[...].reshape(nconv, crow, kdim)
    xe_ref[:, SUBLANES:, kdim:2 * kdim] = k_ref[...].reshape(nconv, crow, kdim)
    xe_ref[:, SUBLANES:, 2 * kdim:] = v_ref[...].reshape(nconv, crow, kdim)
    cw = cw_ref[...]
    pre = xe_ref[:, SUBLANES:, :] * cw[CONV_W - 1:CONV_W]
    for s in range(1, CONV_W):
        pre = pre + xe_ref[:, SUBLANES - s:SUBLANES - s + crow, :] * cw[CONV_W - 1 - s:CONV_W - s]
    xe_ref[:, 0:SUBLANES, :] = xe_ref[:, crow:crow + SUBLANES, :]
    pre = pre.reshape(rows, 3 * kdim)
    qkv = pre * _sigmoid(pre)

    ba = ba_ref[...]
    beta = _sigmoid(ba[:, 0:heads])
    g = -jnp.exp(alog_ref[...]) * _softplus(ba[:, heads:2 * heads] + dtb_ref[...])
    ii = lax.broadcasted_iota(jnp.int32, (gr, gr), 0)
    jj = lax.broadcasted_iota(jnp.int32, (gr, gr), 1)
    same = (ii // c) == (jj // c)
    incl = jnp.logical_and(same, ii >= jj)
    strict = jnp.logical_and(same, ii > jj)
    eye = (ii == jj).astype(F32)
    incl_b = incl.astype(BF16)
    inclt_b = jnp.logical_and(same, ii <= jj).astype(BF16)
    same_b = same.astype(BF16)
    z = z_ref[...]
    ng = ng_ref[...]
    scale = dk ** -0.5

    gcs, gcts, egcs, ekds, gtots = [], [], [], [], []
    for gi in range(groups):
        gg = g[gi * gr:(gi + 1) * gr]
        gt = -jnp.exp(alogt_ref[...]) * _softplus(bat_ref[0, 0, gi][heads:2 * heads, :] + dtbt_ref[...])
        gc = _dot_exact_lhs(incl_b, gg)
        gtot = _dot_exact_lhs(same_b, gg)
        gcs.append(gc)
        gcts.append(_dot_exact_rhs(gt, inclt_b))
        egcs.append(jnp.exp(gc))
        ekds.append(jnp.exp(gtot - gc))
        gtots.append(gtot)

    k_, kb_, dec_, rhs_, qs_ = {}, {}, {}, {}, {}
    for (gi, h) in units:
        r0 = gi * gr
        qh = qkv[r0:r0 + gr, h * dk:(h + 1) * dk]
        kh = qkv[r0:r0 + gr, kdim + h * dk:kdim + (h + 1) * dk]
        vh = qkv[r0:r0 + gr, 2 * kdim + h * dv:2 * kdim + (h + 1) * dv]
        qh = qh * lax.rsqrt(jnp.sum(qh * qh, axis=-1, keepdims=True) + NORM_EPS)
        kh = kh * lax.rsqrt(jnp.sum(kh * kh, axis=-1, keepdims=True) + NORM_EPS)
        bh = beta[r0:r0 + gr, h:h + 1]
        diff = gcs[gi][:, h:h + 1] - gcts[gi][h:h + 1, :]
        u = (gi, h)
        dec_[u] = jnp.where(incl, jnp.exp(jnp.where(incl, diff, 0.0)), 0.0)
        k_[u] = kh
        kb_[u] = kh * bh
        qs_[u] = qh * scale
        rhs_[u] = jnp.concatenate([vh * bh, kb_[u] * egcs[gi][:, h:h + 1]], axis=1)
    npow = {u: jnp.where(strict, -(_bdot_nt(kb_[u], k_[u]) * dec_[u]), 0.0) for u in units}
    qk_ = {u: _bdot_nt(qs_[u], k_[u]) * dec_[u] for u in units}
    p = {u: eye + npow[u] for u in units}
    lvl = 1
    while 2 * lvl < c:
        npow = {u: _dot3(npow[u], npow[u]) for u in units}
        p = {u: p[u] + _dot3(p[u], npow[u]) for u in units}
        lvl *= 2
    sol = {u: _dot3(p[u], rhs_[u]) for u in units}

    o_ = {}
    if carry:
        assert nseg == 1
        cur = {h: s_ref[h] for h in hs}
        for gi in range(groups):
            wq = {}
            for h in hs:
                u = (gi, h)
                lhs = jnp.concatenate([sol[u][:, dv:], qs_[u] * egcs[gi][:, h:h + 1]], axis=0)
                wq[h] = _bdot(lhs, cur[h])
            for h in hs:
                u = (gi, h)
                vnew = sol[u][:, :dv] - wq[h][:gr]
                o_[u] = wq[h][gr:] + _bdot(qk_[u], vnew)
                kd = k_[u] * ekds[gi][:, h:h + 1]
                glast = jnp.exp(gtots[gi][0:1, h:h + 1])
                cur[h] = cur[h] * glast + _bdot_tn(kd, vnew)
        for h in hs:
            s_ref[h] = cur[h]
        sout_ref[0] = s_ref[...]
    else:
        segs = [(gi, j, h) for gi in range(groups) for j in range(nseg) for h in hs]
        wq = {}
        for (gi, j, h) in segs:
            u = (gi, h)
            sl = slice(j * c, (j + 1) * c)
            lhs = jnp.concatenate([sol[u][sl, dv:], qs_[u][sl] * egcs[gi][sl, h:h + 1]], axis=0)
            wq[(gi, j, h)] = _bdot(lhs, s0_ref[gi * nseg + j, h])
        vn = {}
        for (gi, j, h) in segs:
            u = (gi, h)
            sl = slice(j * c, (j + 1) * c)
            vn[(gi, j, h)] = sol[u][sl, :dv] - wq[(gi, j, h)][:c]
            kd = k_[u][sl] * ekds[gi][sl, h:h + 1]
            glast = jnp.exp(gtots[gi][j * c:j * c + 1, h:h + 1])
            sout_ref[gi * nseg + j, h] = s0_ref[gi * nseg + j, h] * glast + _bdot_tn(kd, vn[(gi, j, h)])
        for u in units:
            gi, h = u
            vnew = jnp.concatenate([vn[(gi, j, h)] for j in range(nseg)], axis=0)
            os_ = jnp.concatenate([wq[(gi, j, h)][c:] for j in range(nseg)], axis=0)
            o_[u] = os_ + _bdot(qk_[u], vnew)

    ys = []
    for gi in range(groups):
        row = []
        for h in hs:
            zz = z[gi * gr:(gi + 1) * gr, h * dv:(h + 1) * dv]
            row.append(_rms(o_[(gi, h)], ng) * (zz * _sigmoid(zz)))
        ys.append(jnp.concatenate(row, axis=1))
    y_ref[...] = jnp.concatenate(ys, axis=0)


def _gdn(proj, col0, row0, batch, seq, c, groups, gr, carry, ba, tail0, s0, cw, alog, dtb, ng, heads, dk, dv):
    kdim = heads * dk
    rows = groups * gr
    if carry:
        nb, nsteps, sb = batch, seq // rows, 1
    else:
        sb = rows // seq
        nb, nsteps = batch // sb, 1
    crow = rows // sb
    rb0 = row0 // rows
    cb0 = col0 // kdim
    rblk = lambda b, ci: rb0 + b * nsteps + ci
    small = lambda shape: pl.BlockSpec(shape, lambda b, ci: (0,) * len(shape))
    bat = ba[row0:row0 + batch * seq].reshape(nb, nsteps, groups, gr, 2 * heads)
    bat = jnp.swapaxes(bat, 3, 4)
    return pl.pallas_call(
        functools.partial(_gdn_kernel, groups=groups, gr=gr, c=c, carry=carry, heads=heads, dk=dk, dv=dv),
        grid=(nb, nsteps),
        in_specs=[pl.BlockSpec((rows, kdim), lambda b, ci: (rblk(b, ci), cb0)),
                  pl.BlockSpec((rows, kdim), lambda b, ci: (rblk(b, ci), cb0 + 1)),
                  pl.BlockSpec((rows, kdim), lambda b, ci: (rblk(b, ci), cb0 + 2)),
                  pl.BlockSpec((rows, kdim), lambda b, ci: (rblk(b, ci), cb0 + 3)),
                  pl.BlockSpec((rows, 2 * heads), lambda b, ci: (rblk(b, ci), 0)),
                  pl.BlockSpec((1, 1, groups, 2 * heads, gr), lambda b, ci: (b, ci, 0, 0, 0)),
                  pl.BlockSpec((sb, SUBLANES, 3 * kdim), lambda b, ci: (b, 0, 0)),
                  pl.BlockSpec((sb, heads, dk, dv), lambda b, ci: (b, 0, 0, 0)),
                  small((CONV_W, 3 * kdim)),
                  small((1, heads)), small((1, heads)), small((heads, 1)), small((heads, 1)),
                  small((1, dv))],
        out_specs=[pl.BlockSpec((rows, heads * dv), lambda b, ci: (b * nsteps + ci, 0)),
                   pl.BlockSpec((sb, heads, dk, dv), lambda b, ci: (b, 0, 0, 0))],
        out_shape=[jax.ShapeDtypeStruct((batch * seq, heads * dv), F32),
                   jax.ShapeDtypeStruct((batch, heads, dk, dv), F32)],
        scratch_shapes=[pltpu.VMEM((sb, SUBLANES + crow, 3 * kdim), F32),
                        pltpu.VMEM((heads, dk, dv), F32)],
        compiler_params=_params("parallel", "arbitrary"),
        name="gdn",
    )(proj, proj, proj, proj, ba, bat, tail0, s0, cw, alog, dtb, alog.T, dtb.T, ng)


def _merge_kernel(h_ref, ya1_ref, ya2_ref, yb1_ref, yb2_ref, wga_ref, wgb_ref, wa_ref, wb_ref, u_ref, *, na):
    d = functools.partial(jnp.dot, preferred_element_type=F32)
    first = pl.program_id(0) < na
    h = h_ref[...]
    ya = jnp.where(first, ya1_ref[...], ya2_ref[...]).astype(BF16)
    yb = jnp.where(first, yb1_ref[...], yb2_ref[...]).astype(BF16)
    u = _sigmoid(d(h, wga_ref[...])) * d(ya, wa_ref[...])
    u = u + _sigmoid(d(h, wgb_ref[...])) * d(yb, wb_ref[...])
    u_ref[...] = u.astype(BF16)


def _merge(h, ya1, ya2, yb1, yb2, w_g, w_a, w_b, tm, tn):
    m, d = h.shape
    n = w_a.shape[1]
    nj = n // tn
    return pl.pallas_call(
        functools.partial(_merge_kernel, na=ya1.shape[0] // tm),
        grid=(m // tm, nj),
        in_specs=[pl.BlockSpec((tm, d), lambda i, j: (i, 0)),
                  *_split_rows(ya1, ya2, tm),
                  *_split_rows(yb1, yb2, tm),
                  pl.BlockSpec((d, tn), lambda i, j: (0, j)),
                  pl.BlockSpec((d, tn), lambda i, j: (0, nj + j)),
                  pl.BlockSpec((w_a.shape[0], tn), lambda i, j: (0, j)),
                  pl.BlockSpec((w_b.shape[0], tn), lambda i, j: (0, j))],
        out_specs=pl.BlockSpec((tm, tn), lambda i, j: (i, j)),
        out_shape=jax.ShapeDtypeStruct((m, n), BF16),
        compiler_params=_params("parallel", "parallel"),
        name="merge",
    )(h, ya1, ya2, yb1, yb2, w_g, w_g, w_a, w_b)


def _out_norm_kernel(u_ref, w_ref, xa_ref, xb_ref, g_ref, x1_ref, h_ref, *, na):
    x = jnp.where(pl.program_id(0) < na, xa_ref[...], xb_ref[...])
    x1 = x + jnp.dot(u_ref[...], w_ref[...], preferred_element_type=F32)
    x1_ref[...] = x1
    h_ref[...] = _rms(x1, g_ref[...]).astype(BF16)


def _out_norm(u, w, xa, xb, g, tm):
    m, k = u.shape
    d = xa.shape[1]
    return pl.pallas_call(
        functools.partial(_out_norm_kernel, na=xa.shape[0] // tm),
        grid=(m // tm,),
        in_specs=[pl.BlockSpec((tm, k), lambda i: (i, 0)),
                  pl.BlockSpec((k, d), lambda i: (0, 0)),
                  *_split_rows(xa, xb, tm),
                  pl.BlockSpec((1, d), lambda i: (0, 0))],
        out_specs=[pl.BlockSpec((tm, d), lambda i: (i, 0)),
                   pl.BlockSpec((tm, d), lambda i: (i, 0))],
        out_shape=[jax.ShapeDtypeStruct((m, d), F32), jax.ShapeDtypeStruct((m, d), BF16)],
        compiler_params=_params("parallel"),
        name="out_norm",
    )(u, w, xa, xb, g)


def _memkv_kernel(m_ref, g_ref, wk_ref, wv_ref, k_ref, v_ref):
    mn = _rms(m_ref[...], g_ref[...]).astype(BF16)
    k_ref[...] = jnp.dot(mn, wk_ref[...], preferred_element_type=F32)
    v_ref[...] = jnp.dot(mn, wv_ref[...], preferred_element_type=F32)


def _memkv(mem, g, wk, wv, tm):
    m, d = mem.shape
    n = wk.shape[1]
    return pl.pallas_call(
        _memkv_kernel,
        grid=(m // tm,),
        in_specs=[pl.BlockSpec((tm, d), lambda i: (i, 0)),
                  pl.BlockSpec((1, d), lambda i: (0, 0)),
                  pl.BlockSpec((d, n), lambda i: (0, 0)),
                  pl.BlockSpec((d, n), lambda i: (0, 0))],
        out_specs=[pl.BlockSpec((tm, n), lambda i: (i, 0)), pl.BlockSpec((tm, n), lambda i: (i, 0))],
        out_shape=[jax.ShapeDtypeStruct((m, n), F32), jax.ShapeDtypeStruct((m, n), F32)],
        compiler_params=_params("parallel"),
        name="memkv",
    )(mem, g, wk, wv)


def _attn_kernel(q_ref, k_ref, v_ref, o_ref, *, nb, tq, heads, hd, interleaved):
    mem = k_ref.shape[1] // heads if interleaved else k_ref.shape[1]
    scale = hd ** -0.5
    units = [(b, h) for b in range(nb) for h in range(heads)]

    def head(ref, b, h):
        if interleaved:
            return ref[b, pl.ds(h, mem, stride=heads), :].astype(BF16)
        return ref[b, :, h * hd:(h + 1) * hd].astype(BF16)

    q = q_ref[...].astype(BF16)
    s = {(b, h): lax.dot_general(q[b * tq:(b + 1) * tq, h * hd:(h + 1) * hd], head(k_ref, b, h),
                                 (((1,), (1,)), ((), ())), preferred_element_type=F32) * scale for (b, h) in units}
    p = {u: jnp.exp(s[u] - jnp.max(s[u], axis=-1, keepdims=True)) for u in units}
    o = {(b, h): jnp.dot(p[(b, h)].astype(BF16), head(v_ref, b, h), preferred_element_type=F32)
         / jnp.sum(p[(b, h)], axis=-1, keepdims=True) for (b, h) in units}
    o_ref[...] = jnp.concatenate([jnp.concatenate([o[(b, h)] for h in range(heads)], axis=1) for b in range(nb)],
                                 axis=0)


def _attn(q, row0, batch, seq, nb, tq, k, v, heads, hd, interleaved):
    nt = seq // tq
    rows = nb * tq
    rb0 = row0 // rows
    n = heads * hd
    return pl.pallas_call(
        functools.partial(_attn_kernel, nb=nb, tq=tq, heads=heads, hd=hd, interleaved=interleaved),
        grid=(batch // nb, nt),
        in_specs=[pl.BlockSpec((rows, n), lambda b, t: (rb0 + b * nt + t, 0)),
                  pl.BlockSpec((nb,) + k.shape[1:], lambda b, t: (b, 0, 0)),
                  pl.BlockSpec((nb,) + v.shape[1:], lambda b, t: (b, 0, 0))],
        out_specs=pl.BlockSpec((rows, n), lambda b, t: (b * nt + t, 0)),
        out_shape=jax.ShapeDtypeStruct((batch * seq, n), F32),
        compiler_params=_params("parallel", "parallel"),
        name="attn",
    )(q, k, v)


def _xo_router_kernel(oa_ref, ob_ref, wo_ref, x1_ref, g_ref, rw_ref, rb_ref, x2_ref, h_ref, lg_ref, *, na):
    o = jnp.where(pl.program_id(0) < na, oa_ref[...], ob_ref[...]).astype(BF16)
    x2 = x1_ref[...] + jnp.dot(o, wo_ref[...], preferred_element_type=F32)
    x2_ref[...] = x2
    h = _rms(x2, g_ref[...])
    h_ref[...] = h
    lg_ref[...] = _dot3(h, rw_ref[...]) + rb_ref[...]


def _xo_router(oa, ob, wo, x1, g, rw, rb, tm):
    m, d = x1.shape
    ne = rw.shape[1]
    row = lambda w: pl.BlockSpec((tm, w), lambda i: (i, 0))
    full = lambda a: pl.BlockSpec(a.shape, lambda i: (0, 0))
    return pl.pallas_call(
        functools.partial(_xo_router_kernel, na=oa.shape[0] // tm),
        grid=(m // tm,),
        in_specs=[*_split_rows(oa, ob, tm), full(wo), row(d), full(g), full(rw), full(rb)],
        out_specs=[row(d), row(d), row(ne)],
        out_shape=[jax.ShapeDtypeStruct((m, d), F32), jax.ShapeDtypeStruct((m, d), F32),
                   jax.ShapeDtypeStruct((m, ne), F32)],
        compiler_params=_params("parallel"),
        name="xo_router",
    )(oa, ob, wo, x1, g, rw, rb)


def _moe_kernel(e_ref, r0_ref, n_ref, tok_ref, dst_ref, h_hbm, wg_ref, wu_ref, wd_ref, bg_ref, bu_ref, bd_ref,
                slots_hbm, xbuf, acc, gsem, ssem, *, sub, big, nf):
    s = pl.program_id(0)
    f = pl.program_id(1)
    n = n_ref[s]
    r0 = r0_ref[s]
    d = acc.shape[2]
    s_next = jnp.minimum(s + 1, pl.num_programs(0) - 1)
    n_next = jnp.where(s + 1 < pl.num_programs(0), n_ref[s_next], 0)
    r0_next = r0_ref[s_next]

    @pl.when(jnp.logical_and(s == 0, f == 0))
    def _():
        xbuf[...] = jnp.zeros_like(xbuf)

    def row_in(base, c, k):
        tok = tok_ref[base + c * SUBLANES + k]
        return pltpu.make_async_copy(h_hbm.at[pl.ds(tok, 1), :], xbuf.at[c, pl.ds(k, 1), :], gsem)

    def row_out(c, k):
        dst = dst_ref[r0 + c * SUBLANES + k]
        return pltpu.make_async_copy(acc.at[c, pl.ds(k, 1), :], slots_hbm.at[pl.ds(dst, 1), :], ssem)

    def for_rows(count, fn):
        nfull = count // SUBLANES

        def group(c, carry):
            for k in range(SUBLANES):
                fn(c, k)
            return carry

        lax.fori_loop(0, nfull, group, 0)
        lax.fori_loop(0, count - nfull * SUBLANES, lambda k, carry: (fn(nfull, k), carry)[1], 0)

    @pl.when(n > 0)
    def _active():
        @pl.when(f == 0)
        def _gather():
            @pl.when(s == 0)
            def _first():
                for_rows(n, lambda c, k: row_in(r0, c, k).start())

            for_rows(n, lambda c, k: row_in(r0, c, k).wait())

        bg = bg_ref[0]
        bu = bu_ref[0]
        bd = bd_ref[0]

        nsub = (n + sub - 1) // sub
        sub8 = sub // SUBLANES
        big8 = big // SUBLANES

        @pl.when(f == 0)
        def _init():
            def fill(j, carry):
                acc[pl.ds(pl.multiple_of(j * sub8, sub8), sub8)] = jnp.broadcast_to(bd, (sub8, SUBLANES, d))
                return carry

            lax.fori_loop(0, nsub, fill, 0)

        def block(off8, m):
            m8 = m // SUBLANES
            x = xbuf[pl.ds(off8, m8)].reshape(m, d).astype(BF16)
            gate = jnp.dot(x, wg_ref[0].astype(BF16), preferred_element_type=F32) + bg
            up = jnp.dot(x, wu_ref[0].astype(BF16), preferred_element_type=F32) + bu
            gate = jnp.minimum(gate, SWIGLU_LIMIT)
            up = jnp.clip(up, -SWIGLU_LIMIT, SWIGLU_LIMIT)
            act = gate * _sigmoid(SWIGLU_ALPHA * gate) * (up + 1.0)
            part = jnp.dot(act.astype(BF16), wd_ref[0].astype(BF16), preferred_element_type=F32)
            acc[pl.ds(off8, m8)] += part.reshape(m8, SUBLANES, d)

        per_big = big // sub
        nbig = nsub // per_big
        tail = nsub - nbig * per_big

        def big_block(j, carry):
            block(pl.multiple_of(j * big8, big8), big)
            return carry

        lax.fori_loop(0, nbig, big_block, 0)
        for t in range(1, per_big):
            @pl.when(tail == t)
            def _tail(t=t):
                block(pl.multiple_of(nbig * big8, sub8), t * sub)

        @pl.when(f == nf - 1)
        def _scatter():
            for_rows(n, lambda c, k: row_out(c, k).start())
            for_rows(n_next, lambda c, k: row_in(r0_next, c, k).start())
            for_rows(n, lambda c, k: row_out(c, k).wait())


def _moe(h, sb_e, sb_r0, sb_n, tok, dst, wg, wu, wd, bg, bu, bd, cap, sub, big, tf):
    n_tok, d = h.shape
    ne, _, dff = wg.shape
    nf = dff // tf
    g = sb_e.shape[0]
    fi = lambda s, f, n_ref: jnp.where(n_ref[s] > 0, f, nf - 1)
    grid_spec = pltpu.PrefetchScalarGridSpec(
        num_scalar_prefetch=5,
        grid=(g, nf),
        in_specs=[pl.BlockSpec(memory_space=pl.ANY),
                  pl.BlockSpec((1, d, tf), lambda s, f, e, r, n, t, o: (e[s], 0, fi(s, f, n))),
                  pl.BlockSpec((1, d, tf), lambda s, f, e, r, n, t, o: (e[s], 0, fi(s, f, n))),
                  pl.BlockSpec((1, tf, d), lambda s, f, e, r, n, t, o: (e[s], fi(s, f, n), 0)),
                  pl.BlockSpec((1, 1, tf), lambda s, f, e, r, n, t, o: (e[s], 0, fi(s, f, n))),
                  pl.BlockSpec((1, 1, tf), lambda s, f, e, r, n, t, o: (e[s], 0, fi(s, f, n))),
                  pl.BlockSpec((1, 1, d), lambda s, f, e, r, n, t, o: (e[s], 0, 0))],
        out_specs=pl.BlockSpec(memory_space=pl.ANY),
        scratch_shapes=[pltpu.VMEM((cap // SUBLANES, SUBLANES, d), F32),
                        pltpu.VMEM((cap // SUBLANES, SUBLANES, d), F32),
                        pltpu.SemaphoreType.DMA(()), pltpu.SemaphoreType.DMA(())],
    )
    return pl.pallas_call(
        functools.partial(_moe_kernel, sub=sub, big=big, nf=nf),
        grid_spec=grid_spec,
        out_shape=jax.ShapeDtypeStruct((n_tok * TOP_K, d), F32),
        compiler_params=_params("arbitrary", "arbitrary"),
        name="moe",
    )(sb_e, sb_r0, sb_n, tok, dst, h, wg, wu, wd, bg.reshape(ne, 1, dff), bu.reshape(ne, 1, dff),
      bd.reshape(ne, 1, d))


def _route(logits, cap, n_sb_max):
    n_tok, ne = logits.shape
    top_val, top_idx = lax.top_k(logits, TOP_K)
    gates = jax.nn.softmax(top_val, axis=-1)
    e_flat = top_idx.reshape(-1).astype(jnp.int32)
    codes = jnp.argsort(e_flat, stable=True).astype(jnp.int32)
    counts = jnp.zeros((ne,), jnp.int32).at[e_flat].add(1)
    start = jnp.cumsum(counts) - counts
    n_sb = (counts + cap - 1) // cap
    sb_end = jnp.cumsum(n_sb)
    sb_start = sb_end - n_sb
    total = sb_end[-1]
    s = jnp.arange(n_sb_max, dtype=jnp.int32)
    e_s = jnp.minimum(jnp.searchsorted(sb_end, s, side='right'), ne - 1).astype(jnp.int32)
    e_last = e_s[jnp.maximum(total - 1, 0)]
    active = s < total
    e_s = jnp.where(active, e_s, e_last)
    within = s - sb_start[e_s]
    r0 = jnp.where(active, start[e_s] + within * cap, 0).astype(jnp.int32)
    n = jnp.where(active, jnp.clip(counts[e_s] - within * cap, 0, cap), 0).astype(jnp.int32)
    tok = lax.shift_right_logical(codes, TOP_K_SHIFT)
    dst = (codes & (TOP_K - 1)) * n_tok + tok
    return gates, tok, dst, e_s, r0, n


def _final_kernel(x2_ref, s0_ref, s1_ref, s2_ref, s3_ref, gates_ref, g_ref, y_ref):
    x = x2_ref[...]
    gt = gates_ref[...]
    for k, s_ref in enumerate((s0_ref, s1_ref, s2_ref, s3_ref)):
        x = x + s_ref[...] * gt[:, k:k + 1]
    y_ref[...] = _rms(x, g_ref[...])


def _final(x2, slots, gates, g, row0, rows, tm):
    n_tok, d = x2.shape
    rb0 = row0 // tm
    nblk = n_tok // tm
    slot = lambda k: pl.BlockSpec((tm, d), lambda i: (k * nblk + rb0 + i, 0))
    return pl.pallas_call(
        _final_kernel,
        grid=(rows // tm,),
        in_specs=[pl.BlockSpec((tm, d), lambda i: (rb0 + i, 0)),
                  slot(0), slot(1), slot(2), slot(3),
                  pl.BlockSpec((tm, TOP_K), lambda i: (rb0 + i, 0)),
                  pl.BlockSpec((1, d), lambda i: (0, 0))],
        out_specs=pl.BlockSpec((tm, d), lambda i: (i, 0)),
        out_shape=jax.ShapeDtypeStruct((rows, d), F32),
        compiler_params=_params("parallel"),
        name="final",
    )(x2, slots, slots, slots, slots, gates, g)


def _pad_tail(buf):
    return jnp.pad(buf, ((0, 0), (SUBLANES - (CONV_W - 1), 0), (0, 0)))


def kernel(x_prompt, x_sample, mem_prompt, state_lru_conv, state_lru_h, state_gdn_conv, state_gdn_S, cache_mem_k, cache_mem_v, norm_mix_g, w_in, lru_conv_w, lru_conv_b, lru_wx, lru_bx, lru_wa, lru_ba, lru_a_param, gdn_conv_w, gdn_A_log, gdn_dt_bias, gdn_norm_g, w_branch_a, w_branch_b, w_out, norm_xa_g, norm_mem_g, xa_wq, xa_wk, xa_wv, xa_wo, norm_moe_g, router_w, router_b, moe_w_gate, moe_b_gate, moe_w_up, moe_b_up, moe_w_down, moe_b_down, norm_final_g):
    depth = w_in.shape[0]
    assert depth == 1
    bp, tp, d = x_prompt.shape
    bs, ts, _ = x_sample.shape
    np_, ns = bp * tp, bs * ts
    n_tok = np_ + ns
    lw = lru_conv_w.shape[2]
    heads, dk, dv = state_gdn_S.shape[2:]
    kdim = heads * dk
    mem_len = mem_prompt.shape[1]
    xa_heads, xa_hd = cache_mem_k.shape[3:]
    xa_dim = xa_heads * xa_hd
    ne = router_w.shape[2]
    past_len = 16384
    gdn_chunk = 64

    row = lambda v: v.reshape(1, -1)
    xp = x_prompt.reshape(np_, d)
    xs = x_sample.reshape(ns, d)

    c_main = 2 * lw + 3 * kdim + heads * dv
    w_in0 = w_in[0]
    w_ba = w_in0[:, c_main:c_main + 2 * heads].astype(BF16)
    w_g = w_in0[:, c_main + 2 * heads:].astype(BF16)
    w_gates = jnp.concatenate([lru_wx[0], lru_wa[0]], axis=-1).astype(BF16)

    rows_gcd = math.gcd(np_, ns)
    tm_s = _tile(rows_gcd, 256)
    tm_m = _tile(rows_gcd, 512)
    tm_l = _tile(rows_gcd, 1024)
    h1, ba = _norm_ba(xp, xs, row(norm_mix_g[0]), w_ba, tm_m)
    proj = _matmul(h1, w_in0, c_main, tm_l, _tile(c_main, 1024), F32, "in_proj")

    lru_args = (lru_conv_w[0], row(lru_conv_b[0]), w_gates, row(lru_bx[0]), row(lru_ba[0]), row(lru_a_param[0]))
    ya_p, lru_h_p = _lru_seq(proj, 0, bp, tp, _tile(tp, 256), jnp.zeros((bp, SUBLANES, lw), F32),
                             jnp.zeros((bp, 1, lw), F32), *lru_args, pos0=0)
    if ts == SUBLANES:
        ya_s, lru_h_s = _lru_short(proj, np_, bs, math.gcd(bs, 32), _pad_tail(state_lru_conv[0]),
                                   state_lru_h[0][:, None, :], *lru_args)
    else:
        ya_s, lru_h_s = _lru_seq(proj, np_, bs, ts, ts, _pad_tail(state_lru_conv[0]), state_lru_h[0][:, None, :],
                                 *lru_args, pos0=past_len)

    gdn_args = (gdn_conv_w[0], row(gdn_A_log[0]), row(gdn_dt_bias[0]), row(gdn_norm_g[0]), heads, dk, dv)
    cp = min(gdn_chunk, tp)
    assert tp % cp == 0 and ts <= gdn_chunk
    gp = 4 if (tp // cp) % 4 == 0 else 1
    yb_p, gdn_s_p = _gdn(proj, 2 * lw, 0, bp, tp, cp, gp, cp, True, ba,
                         jnp.zeros((bp, SUBLANES, 3 * kdim), F32), jnp.zeros((bp, heads, dk, dv), F32), *gdn_args)
    gr_s = ts * math.gcd(bs, max(gdn_chunk // ts, 1))
    gs = 2 if (ns // gr_s) % 2 == 0 else 1
    yb_s, gdn_s_s = _gdn(proj, 2 * lw, np_, bs, ts, ts, gs, gr_s, False, ba,
                         _pad_tail(state_gdn_conv[0]), state_gdn_S[0], *gdn_args)

    u = _merge(h1, ya_p, ya_s, yb_p, yb_s, w_g, w_branch_a[0].astype(BF16), w_branch_b[0].astype(BF16), tm_m,
               _tile(d, 512))
    x1, h2 = _out_norm(u, w_out[0].astype(BF16), xp, xs, row(norm_xa_g[0]), tm_m)

    k_p, v_p = _memkv(mem_prompt.reshape(bp * mem_len, d), row(norm_mem_g[0]), xa_wk[0].astype(BF16),
                      xa_wv[0].astype(BF16), _tile(bp * mem_len, 256))
    q = _matmul(h2, xa_wq[0], xa_dim, tm_l, xa_dim, F32, "xa_q")
    o_p = _attn(q, 0, bp, tp, 1, _tile(tp, 512), k_p.reshape(bp, mem_len, xa_dim), v_p.reshape(bp, mem_len, xa_dim),
                xa_heads, xa_hd, False)
    nb_s = math.gcd(bs, 4)
    o_s = _attn(q, np_, bs, ts, nb_s, ts, cache_mem_k[0].reshape(bs, mem_len * xa_heads, xa_hd),
                cache_mem_v[0].reshape(bs, mem_len * xa_heads, xa_hd), xa_heads, xa_hd, True)
    x2, h3, logits = _xo_router(o_p, o_s, xa_wo[0].astype(BF16), x1, row(norm_moe_g[0]), router_w[0],
                                row(router_b[0]), tm_m)

    cap, sub, big, tf = 1536, 128, 512, _tile(moe_w_gate.shape[3], 256)
    n_sb_max = (n_tok * TOP_K + ne * (cap - 1)) // cap
    gates, tok, dst, sb_e, sb_r0, sb_n = _route(logits, cap, n_sb_max)
    slots = _moe(h3, sb_e, sb_r0, sb_n, tok, dst, moe_w_gate[0], moe_w_up[0], moe_w_down[0],
                 moe_b_gate[0], moe_b_up[0], moe_b_down[0], cap, sub, big, tf)
    y_p =_final(x2, slots, gates, row(norm_final_g), 0, np_, tm_s)
    y_s = _final(x2, slots, gates, row(norm_final_g), np_, ns, tm_s)

    keep = CONV_W - 1
    pp = jnp.stack([lax.slice(proj, ((b + 1) * tp - keep, 0), ((b + 1) * tp, c_main)) for b in range(bp)])
    ps = proj[np_:].reshape(bs, ts, c_main)
    return (y_p.reshape(bp, tp, d), y_s.reshape(bs, ts, d),
            pp[:, :, :lw][None], lru_h_p.reshape(1, bp, lw),
            pp[:, :, 2 * lw:2 * lw + 3 * kdim][None], gdn_s_p[None],
            k_p.reshape(1, bp, mem_len, xa_heads, xa_hd), v_p.reshape(1, bp, mem_len, xa_heads, xa_hd),
            ps[:, ts - keep:, :lw][None], lru_h_s.reshape(1, bs, lw),
            ps[:, ts - keep:, 2 * lw:2 * lw + 3 * kdim][None], gdn_s_s[None])
```

```python
import functools
import math

import jax
import jax.numpy as jnp
from jax import lax
from jax.experimental import pallas as pl
from jax.experimental.pallas import tpu as pltpu

F32 = jnp.float32
BF16 = jnp.bfloat16
NORM_EPS = 1e-6
LRU_C = 8.0
CONV_W = 4
SWIGLU_ALPHA = 1.702
SWIGLU_LIMIT = 7.0
TOP_K = 4
TOP_K_SHIFT = 2
VMEM_LIMIT_BYTES = 56 * 1024 * 1024
SUBLANES = 8
LANES = 128


def _params(*sem):
    return pltpu.CompilerParams(dimension_semantics=sem, vmem_limit_bytes=VMEM_LIMIT_BYTES)


def _bdot(a, b):
    return jnp.dot(a.astype(BF16), b.astype(BF16), preferred_element_type=F32)


def _bdot_nt(a, b):
    return lax.dot_general(a.astype(BF16), b.astype(BF16), (((1,), (1,)), ((), ())), preferred_element_type=F32)


def _dot_nt(a, b):
    return lax.dot_general(a, b, (((1,), (1,)), ((), ())), preferred_element_type=F32)


def _bdot_tn(a, b):
    return lax.dot_general(a.astype(BF16), b.astype(BF16), (((0,), (0,)), ((), ())), preferred_element_type=F32)


def _split2(x):
    hi = x.astype(BF16)
    lo = (x - hi.astype(F32)).astype(BF16)
    return hi, lo


def _split3(x):
    p1 = x.astype(BF16)
    r1 = x - p1.astype(F32)
    p2 = r1.astype(BF16)
    p3 = (r1 - p2.astype(F32)).astype(BF16)
    return p1, p2, p3


def _dot3(a, b):
    ah, al = _split2(a)
    bh, bl = _split2(b)
    d = functools.partial(jnp.dot, preferred_element_type=F32)
    return d(ah, bh) + (d(ah, bl) + d(al, bh))


def _dot_exact_lhs(m_bf16, x):
    d = functools.partial(jnp.dot, preferred_element_type=F32)
    p1, p2, p3 = _split3(x)
    return d(m_bf16, p1) + (d(m_bf16, p2) + d(m_bf16, p3))


def _dot_exact_rhs(x, m_bf16):
    d = functools.partial(jnp.dot, preferred_element_type=F32)
    p1, p2, p3 = _split3(x)
    return d(p1, m_bf16) + (d(p2, m_bf16) + d(p3, m_bf16))


def _sigmoid(x):
    return jax.nn.sigmoid(x)


def _tile(n, pref):
    if n <= pref:
        return n
    t = pref - pref % SUBLANES
    while n % t:
        t -= SUBLANES
    return t


def _softplus(x):
    return jnp.maximum(x, 0.0) + jnp.log1p(jnp.exp(-jnp.abs(x)))


def _gelu_tanh(x):
    return 0.5 * x * (1.0 + jnp.tanh(0.7978845608028654 * (x + 0.044715 * (x * x * x))))


def _rms(x, g):
    r = lax.rsqrt(jnp.mean(x * x, axis=-1, keepdims=True) + NORM_EPS)
    return x * r * g


def _split_rows(a, b, tm):
    na = a.shape[0] // tm
    w = a.shape[1]
    return (pl.BlockSpec((tm, w), lambda i, *_: (jnp.minimum(i, na - 1), 0)),
            pl.BlockSpec((tm, w), lambda i, *_: (jnp.maximum(i - na, 0), 0)))


def _norm_ba_kernel(xa_ref, xb_ref, g_ref, wba_ref, h_ref, ba_ref, *, na):
    x = jnp.where(pl.program_id(0) < na, xa_ref[...], xb_ref[...])
    h = _rms(x, g_ref[...]).astype(BF16)
    h_ref[...] = h
    ba_ref[...] = _dot_nt(h, wba_ref[...])


def _norm_ba(xa, xb, g, w_ba_t, tm):
    d = xa.shape[1]
    m = xa.shape[0] + xb.shape[0]
    nb = w_ba_t.shape[0]
    return pl.pallas_call(
        functools.partial(_norm_ba_kernel, na=xa.shape[0] // tm),
        grid=(m // tm,),
        in_specs=[*_split_rows(xa, xb, tm),
                  pl.BlockSpec((1, d), lambda i: (0, 0)),
                  pl.BlockSpec((nb, d), lambda i: (0, 0))],
        out_specs=[pl.BlockSpec((tm, d), lambda i: (i, 0)),
                   pl.BlockSpec((tm, nb), lambda i: (i, 0))],
        out_shape=[jax.ShapeDtypeStruct((m, d), BF16), jax.ShapeDtypeStruct((m, nb), F32)],
        compiler_params=_params("parallel"),
        name="norm_ba",
    )(xa, xb, g, w_ba_t)


def _mm_kernel(a_ref, w_ref, o_ref, *, w_is_t):
    w = w_ref[...].astype(BF16)
    acc = _dot_nt(a_ref[...], w) if w_is_t else jnp.dot(a_ref[...], w, preferred_element_type=F32)
    o_ref[...] = acc.astype(o_ref.dtype)


def _matmul(a, w, n_cols, tm, tn, out_dtype, name, w_is_t=False):
    m, k = a.shape
    w_spec = pl.BlockSpec((tn, k), lambda j, i: (j, 0)) if w_is_t else pl.BlockSpec((k, tn), lambda j, i: (0, j))
    return pl.pallas_call(
        functools.partial(_mm_kernel, w_is_t=w_is_t),
        grid=(n_cols // tn, m // tm),
        in_specs=[pl.BlockSpec((tm, k), lambda j, i: (i, 0)), w_spec],
        out_specs=pl.BlockSpec((tm, tn), lambda j, i: (i, j)),
        out_shape=jax.ShapeDtypeStruct((m, n_cols), out_dtype),
        compiler_params=_params("parallel", "parallel"),
        name=name,
    )(a, w)


def _conv4_into(xe_ref, rows, cw):
    y = xe_ref[SUBLANES:SUBLANES + rows, :] * cw[CONV_W - 1:CONV_W]
    for s in range(1, CONV_W):
        y = y + xe_ref[SUBLANES - s:SUBLANES - s + rows, :] * cw[CONV_W - 1 - s:CONV_W - s]
    return y


def _scan8(a3, b3):
    row = lax.broadcasted_iota(jnp.int32, a3.shape, 1)
    for s in (1, 2, 4):
        a_sh = pltpu.roll(a3, s, axis=1)
        b_sh = pltpu.roll(b3, s, axis=1)
        m = row >= s
        b3 = jnp.where(m, a3 * b_sh + b3, b3)
        a3 = jnp.where(m, a3 * a_sh, a3)
    return a3, b3


def _lru_gates(xc, wg_ref, bx, bga, ap, first_row_is_pos0):
    nblk = wg_ref.shape[0]
    bw = wg_ref.shape[1]
    xcb = xc.astype(BF16)
    gx, ga = [], []
    for n in range(nblk):
        r = jnp.dot(xcb[:, n * bw:(n + 1) * bw], wg_ref[n], preferred_element_type=F32)
        gx.append(r[:, :bw])
        ga.append(r[:, bw:])
    gate_x = _sigmoid(jnp.concatenate(gx, axis=1) + bx)
    gate_a = _sigmoid(jnp.concatenate(ga, axis=1) + bga)
    log_a = (-LRU_C) * gate_a * _softplus(ap)
    a = jnp.exp(log_a)
    mult = jnp.sqrt(-jnp.tanh(log_a) * (a * a + 1.0))
    if first_row_is_pos0 is not None:
        mult = jnp.where(first_row_is_pos0, 1.0, mult)
    return a, xc * gate_x * mult


def _lru_seq_kernel(xa_ref, ya_ref, tail0_ref, h0_ref, cw_ref, cb_ref, wg_ref, bx_ref, bga_ref, ap_ref,
                    y_ref, hl_ref, xe_ref, a_ref, b_ref, h_ref, *, tt, pos0):
    t = pl.program_id(1)
    w = xa_ref.shape[1]

    @pl.when(t == 0)
    def _():
        xe_ref[0:SUBLANES, :] = tail0_ref[0]
        h_ref[...] = h0_ref[0]

    xe_ref[SUBLANES:SUBLANES + tt, :] = xa_ref[...]
    xc = _conv4_into(xe_ref, tt, cw_ref[...]) + cb_ref[...]
    xe_ref[0:SUBLANES, :] = xe_ref[tt:tt + SUBLANES, :]
    first = None
    if pos0 == 0:
        first = (lax.broadcasted_iota(jnp.int32, (tt, 1), 0) + t * tt) == 0
    a, b = _lru_gates(xc, wg_ref, bx_ref[...], bga_ref[...], ap_ref[...], first)
    a3, b3 = _scan8(a.reshape(tt // SUBLANES, SUBLANES, w), b.reshape(tt // SUBLANES, SUBLANES, w))
    a_ref[...] = a3
    b_ref[...] = b3

    def body(g, h):
        hg = a_ref[g] * h + b_ref[g]
        b_ref[g] = hg
        return hg[SUBLANES - 1:SUBLANES, :]

    h = lax.fori_loop(0, tt // SUBLANES, body, h_ref[...])
    h_ref[...] = h
    hl_ref[0] = h
    hs = b_ref[...].reshape(tt, w)
    y_ref[...] = (hs * _gelu_tanh(ya_ref[...])).astype(BF16)


def _lru_seq(proj, row0, batch, seq, tt, tail0, h0, cw, cb, wg, bx, bga, ap, pos0):
    w = cw.shape[1]
    nt = seq // tt
    rb0 = row0 // tt
    vec = lambda: pl.BlockSpec((1, w), lambda b, t: (0, 0))
    return pl.pallas_call(
        functools.partial(_lru_seq_kernel, tt=tt, pos0=pos0),
        grid=(batch, nt),
        in_specs=[pl.BlockSpec((tt, w), lambda b, t: (rb0 + b * nt + t, 0)),
                  pl.BlockSpec((tt, w), lambda b, t: (rb0 + b * nt + t, 1)),
                  pl.BlockSpec((1, SUBLANES, w), lambda b, t: (b, 0, 0)),
                  pl.BlockSpec((1, 1, w), lambda b, t: (b, 0, 0)),
                  pl.BlockSpec((CONV_W, w), lambda b, t: (0, 0)),
                  vec(),
                  pl.BlockSpec(wg.shape, lambda b, t: (0, 0, 0)),
                  vec(), vec(), vec()],
        out_specs=[pl.BlockSpec((tt, w), lambda b, t: (b * nt + t, 0)),
                   pl.BlockSpec((1, 1, w), lambda b, t: (b, 0, 0))],
        out_shape=[jax.ShapeDtypeStruct((batch * seq, w), BF16),
                   jax.ShapeDtypeStruct((batch, 1, w), F32)],
        scratch_shapes=[pltpu.VMEM((SUBLANES + tt, w), F32),
                        pltpu.VMEM((tt // SUBLANES, SUBLANES, w), F32),
                        pltpu.VMEM((tt // SUBLANES, SUBLANES, w), F32),
                        pltpu.VMEM((1, w), F32)],
        compiler_params=_params("parallel", "arbitrary"),
        name="lru_seq",
    )(proj, proj, tail0, h0, cw, cb, wg, bx, bga, ap)


def _lru_short_kernel(xa_ref, ya_ref, tail0_ref, h0_ref, cw_ref, cb_ref, wg_ref, bx_ref, bga_ref, ap_ref,
                      y_ref, hl_ref, xe_ref):
    nb = xe_ref.shape[0]
    w = xa_ref.shape[1]
    xe_ref[:, 0:SUBLANES, :] = tail0_ref[...]
    xe_ref[:, SUBLANES:, :] = xa_ref[...].reshape(nb, SUBLANES, w)
    cw = cw_ref[...]
    xc = xe_ref[:, SUBLANES:, :] * cw[CONV_W - 1:CONV_W]
    for s in range(1, CONV_W):
        xc = xc + xe_ref[:, SUBLANES - s:2 * SUBLANES - s, :] * cw[CONV_W - 1 - s:CONV_W - s]
    xc = xc.reshape(nb * SUBLANES, w) + cb_ref[...]
    a, b = _lru_gates(xc, wg_ref, bx_ref[...], bga_ref[...], ap_ref[...], None)
    a3, b3 = _scan8(a.reshape(nb, SUBLANES, w), b.reshape(nb, SUBLANES, w))
    hs = a3 * h0_ref[...] + b3
    hl_ref[...] = hs[:, SUBLANES - 1:SUBLANES, :]
    y_ref[...] = (hs.reshape(nb * SUBLANES, w) * _gelu_tanh(ya_ref[...])).astype(BF16)


def _lru_short(proj, row0, batch, nb, tail0, h0, cw, cb, wg, bx, bga, ap):
    w = cw.shape[1]
    rows = nb * SUBLANES
    rb0 = row0 // rows
    vec = lambda: pl.BlockSpec((1, w), lambda b: (0, 0))
    return pl.pallas_call(
        _lru_short_kernel,
        grid=(batch // nb,),
        in_specs=[pl.BlockSpec((rows, w), lambda b: (rb0 + b, 0)),
                  pl.BlockSpec((rows, w), lambda b: (rb0 + b, 1)),
                  pl.BlockSpec((nb, SUBLANES, w), lambda b: (b, 0, 0)),
                  pl.BlockSpec((nb, 1, w), lambda b: (b, 0, 0)),
                  pl.BlockSpec((CONV_W, w), lambda b: (0, 0)),
                  vec(),
                  pl.BlockSpec(wg.shape, lambda b: (0, 0, 0)),
                  vec(), vec(), vec()],
        out_specs=[pl.BlockSpec((rows, w), lambda b: (b, 0)),
                   pl.BlockSpec((nb, 1, w), lambda b: (b, 0, 0))],
        out_shape=[jax.ShapeDtypeStruct((batch * SUBLANES, w), BF16),
                   jax.ShapeDtypeStruct((batch, 1, w), F32)],
        scratch_shapes=[pltpu.VMEM((nb, 2 * SUBLANES, w), F32)],
        compiler_params=_params("parallel"),
        name="lru_short",
    )(proj, proj, tail0, h0, cw, cb, wg, bx, bga, ap)


def _gdn_kernel(q_ref, k_ref, v_ref, z_ref, ba_ref, bat_ref, tail0_ref, s0_ref, cw_ref, alog_ref, dtb_ref,
                alogt_ref, dtbt_ref, ng_ref, y_ref, sout_ref, xe_ref, s_ref, *, groups, gr, c, carry, heads, dk, dv):
    ci = pl.program_id(1)
    kdim = heads * dk
    rows = groups * gr
    nseg = gr // c
    nconv, crow = xe_ref.shape[0], xe_ref.shape[1] - SUBLANES
    hs = range(heads)
    units = [(gi, h) for gi in range(groups) for h in hs]

    @pl.when(ci == 0)
    def _():
        xe_ref[:, 0:SUBLANES, :] = tail0_ref[...]
        if carry:
            s_ref[...] = s0_ref[0]

    xe_ref[:, SUBLANES:, 0:kdim] = q_ref[...].reshape(nconv, crow, kdim)
    xe_ref[:, SUBLANES:, kdim:2 * kdim] = k_ref[...].reshape(nconv, crow, kdim)
    xe_ref[:, SUBLANES:, 2 * kdim:] = v_ref[...].reshape(nconv, crow, kdim)
    cw = cw_ref[...]
    pre = xe_ref[:, SUBLANES:, :] * cw[CONV_W - 1:CONV_W]
    for s in range(1, CONV_W):
        pre = pre + xe_ref[:, SUBLANES - s:SUBLANES - s + crow, :] * cw[CONV_W - 1 - s:CONV_W - s]
    xe_ref[:, 0:SUBLANES, :] = xe_ref[:, crow:crow + SUBLANES, :]
    pre = pre.reshape(rows, 3 * kdim)
    qkv = pre * _sigmoid(pre)

    ba = ba_ref[...]
    beta = _sigmoid(ba[:, 0:heads])
    g = -jnp.exp(alog_ref[...]) * _softplus(ba[:, heads:2 * heads] + dtb_ref[...])
    ii = lax.broadcasted_iota(jnp.int32, (gr, gr), 0)
    jj = lax.broadcasted_iota(jnp.int32, (gr, gr), 1)
    same = (ii // c) == (jj // c)
    incl = jnp.logical_and(same, ii >= jj)
    strict = jnp.logical_and(same, ii > jj)
    eye = (ii == jj).astype(F32)
    incl_b = incl.astype(BF16)
    inclt_b = jnp.logical_and(same, ii <= jj).astype(BF16)
    same_b = same.astype(BF16)
    z = z_ref[...]
    ng = ng_ref[...]
    scale = dk ** -0.5

    gcs, gcts, egcs, ekds, gtots = [], [], [], [], []
    for gi in range(groups):
        gg = g[gi * gr:(gi + 1) * gr]
        gt = -jnp.exp(alogt_ref[...]) * _softplus(bat_ref[0, 0, gi][heads:2 * heads, :] + dtbt_ref[...])
        gc = _dot_exact_lhs(incl_b, gg)
        gtot = _dot_exact_lhs(same_b, gg)
        gcs.append(gc)
        gcts.append(_dot_exact_rhs(gt, inclt_b))
        egcs.append(jnp.exp(gc))
        ekds.append(jnp.exp(gtot - gc))
        gtots.append(gtot)

    k_, kb_, dec_, rhs_, qs_ = {}, {}, {}, {}, {}
    for (gi, h) in units:
        r0 = gi * gr
        qh = qkv[r0:r0 + gr, h * dk:(h + 1) * dk]
        kh = qkv[r0:r0 + gr, kdim + h * dk:kdim + (h + 1) * dk]
        vh = qkv[r0:r0 + gr, 2 * kdim + h * dv:2 * kdim + (h + 1) * dv]
        qh = qh * lax.rsqrt(jnp.sum(qh * qh, axis=-1, keepdims=True) + NORM_EPS)
        kh = kh * lax.rsqrt(jnp.sum(kh * kh, axis=-1, keepdims=True) + NORM_EPS)
        bh = beta[r0:r0 + gr, h:h + 1]
        diff = gcs[gi][:, h:h + 1] - gcts[gi][h:h + 1, :]
        u = (gi, h)
        dec_[u] = jnp.where(incl, jnp.exp(jnp.where(incl, diff, 0.0)), 0.0)
        k_[u] = kh
        kb_[u] = kh * bh
        qs_[u] = qh * scale
        rhs_[u] = jnp.concatenate([vh * bh, kb_[u] * egcs[gi][:, h:h + 1]], axis=1)
    npow = {u: jnp.where(strict, -(_bdot_nt(kb_[u], k_[u]) * dec_[u]), 0.0) for u in units}
    qk_ = {u: _bdot_nt(qs_[u], k_[u]) * dec_[u] for u in units}
    p = {u: eye + npow[u] for u in units}
    lvl = 1
    while 2 * lvl < c:
        npow = {u: _dot3(npow[u], npow[u]) for u in units}
        p = {u: p[u] + _dot3(p[u], npow[u]) for u in units}
        lvl *= 2
    sol = {u: _dot3(p[u], rhs_[u]) for u in units}

    o_ = {}
    if carry:
        assert nseg == 1
        cur = {h: s_ref[h] for h in hs}
        for gi in range(groups):
            wq = {}
            for h in hs:
                u = (gi, h)
                lhs = jnp.concatenate([sol[u][:, dv:], qs_[u] * egcs[gi][:, h:h + 1]], axis=0)
                wq[h] = _bdot(lhs, cur[h])
            for h in hs:
                u = (gi, h)
                vnew = sol[u][:, :dv] - wq[h][:gr]
                o_[u] = wq[h][gr:] + _bdot(qk_[u], vnew)
                kd = k_[u] * ekds[gi][:, h:h + 1]
                glast = jnp.exp(gtots[gi][0:1, h:h + 1])
                cur[h] = cur[h] * glast + _bdot_tn(kd, vnew)
        for h in hs:
            s_ref[h] = cur[h]
        sout_ref[0] = s_ref[...]
    else:
        segs = [(gi, j, h) for gi in range(groups) for j in range(nseg) for h in hs]
        wq = {}
        for (gi, j, h) in segs:
            u = (gi, h)
            sl = slice(j * c, (j + 1) * c)
            lhs = jnp.concatenate([sol[u][sl, dv:], qs_[u][sl] * egcs[gi][sl, h:h + 1]], axis=0)
            wq[(gi, j, h)] = _bdot(lhs, s0_ref[gi * nseg + j, h])
        vn = {}
        for (gi, j, h) in segs:
            u = (gi, h)
            sl = slice(j * c, (j + 1) * c)
            vn[(gi, j, h)] = sol[u][sl, :dv] - wq[(gi, j, h)][:c]
            kd = k_[u][sl] * ekds[gi][sl, h:h + 1]
            glast = jnp.exp(gtots[gi][j * c:j * c + 1, h:h + 1])
            sout_ref[gi * nseg + j, h] = s0_ref[gi * nseg + j, h] * glast + _bdot_tn(kd, vn[(gi, j, h)])
        for u in units:
            gi, h = u
            vnew = jnp.concatenate([vn[(gi, j, h)] for j in range(nseg)], axis=0)
            os_ = jnp.concatenate([wq[(gi, j, h)][c:] for j in range(nseg)], axis=0)
            o_[u] = os_ + _bdot(qk_[u], vnew)

    ys = []
    for gi in range(groups):
        row = []
        for h in hs:
            zz = z[gi * gr:(gi + 1) * gr, h * dv:(h + 1) * dv]
            row.append(_rms(o_[(gi, h)], ng) * (zz * _sigmoid(zz)))
        ys.append(jnp.concatenate(row, axis=1))
    y_ref[...] = jnp.concatenate(ys, axis=0).astype(BF16)


def _gdn(proj, col0, row0, batch, seq, c, groups, gr, carry, ba, tail0, s0, cw, alog, dtb, ng, heads, dk, dv):
    kdim = heads * dk
    rows = groups * gr
    if carry:
        nb, nsteps, sb = batch, seq // rows, 1
    else:
        sb = rows // seq
        nb, nsteps = batch // sb, 1
    crow = rows // sb
    rb0 = row0 // rows
    cb0 = col0 // kdim
    rblk = lambda b, ci: rb0 + b * nsteps + ci
    small = lambda shape: pl.BlockSpec(shape, lambda b, ci: (0,) * len(shape))
    bat = ba[row0:row0 + batch * seq].reshape(nb, nsteps, groups, gr, 2 * heads)
    bat = jnp.swapaxes(bat, 3, 4)
    return pl.pallas_call(
        functools.partial(_gdn_kernel, groups=groups, gr=gr, c=c, carry=carry, heads=heads, dk=dk, dv=dv),
        grid=(nb, nsteps),
        in_specs=[pl.BlockSpec((rows, kdim), lambda b, ci: (rblk(b, ci), cb0)),
                  pl.BlockSpec((rows, kdim), lambda b, ci: (rblk(b, ci), cb0 + 1)),
                  pl.BlockSpec((rows, kdim), lambda b, ci: (rblk(b, ci), cb0 + 2)),
                  pl.BlockSpec((rows, kdim), lambda b, ci: (rblk(b, ci), cb0 + 3)),
                  pl.BlockSpec((rows, 2 * heads), lambda b, ci: (rblk(b, ci), 0)),
                  pl.BlockSpec((1, 1, groups, 2 * heads, gr), lambda b, ci: (b, ci, 0, 0, 0)),
                  pl.BlockSpec((sb, SUBLANES, 3 * kdim), lambda b, ci: (b, 0, 0)),
                  pl.BlockSpec((sb, heads, dk, dv), lambda b, ci: (b, 0, 0, 0)),
                  small((CONV_W, 3 * kdim)),
                  small((1, heads)), small((1, heads)), small((heads, 1)), small((heads, 1)),
                  small((1, dv))],
        out_specs=[pl.BlockSpec((rows, heads * dv), lambda b, ci: (b * nsteps + ci, 0)),
                   pl.BlockSpec((sb, heads, dk, dv), lambda b, ci: (b, 0, 0, 0))],
        out_shape=[jax.ShapeDtypeStruct((batch * seq, heads * dv), BF16),
                   jax.ShapeDtypeStruct((batch, heads, dk, dv), F32)],
        scratch_shapes=[pltpu.VMEM((sb, SUBLANES + crow, 3 * kdim), F32),
                        pltpu.VMEM((heads, dk, dv), F32)],
        compiler_params=_params("parallel", "arbitrary"),
        name="gdn",
    )(proj, proj, proj, proj, ba, bat, tail0, s0, cw, alog, dtb, alog.T, dtb.T, ng)


def _merge_kernel(h_ref, ya1_ref, ya2_ref, yb1_ref, yb2_ref, wga_ref, wgb_ref, wa_ref, wb_ref, u_ref, *, na):
    d = functools.partial(jnp.dot, preferred_element_type=F32)
    first = pl.program_id(0) < na
    h = h_ref[...]
    ya = jnp.where(first, ya1_ref[...], ya2_ref[...]).astype(BF16)
    yb = jnp.where(first, yb1_ref[...], yb2_ref[...]).astype(BF16)
    u = _sigmoid(_dot_nt(h, wga_ref[...])) * d(ya, wa_ref[...])
    u = u + _sigmoid(_dot_nt(h, wgb_ref[...])) * d(yb, wb_ref[...])
    u_ref[...] = u.astype(BF16)


def _merge(h, ya1, ya2, yb1, yb2, w_g_t, w_a, w_b, tm, tn):
    m, d = h.shape
    n = w_a.shape[1]
    nj = n // tn
    return pl.pallas_call(
        functools.partial(_merge_kernel, na=ya1.shape[0] // tm),
        grid=(m // tm, nj),
        in_specs=[pl.BlockSpec((tm, d), lambda i, j: (i, 0)),
                  *_split_rows(ya1, ya2, tm),
                  *_split_rows(yb1, yb2, tm),
                  pl.BlockSpec((tn, d), lambda i, j: (j, 0)),
                  pl.BlockSpec((tn, d), lambda i, j: (nj + j, 0)),
                  pl.BlockSpec((w_a.shape[0], tn), lambda i, j: (0, j)),
                  pl.BlockSpec((w_b.shape[0], tn), lambda i, j: (0, j))],
        out_specs=pl.BlockSpec((tm, tn), lambda i, j: (i, j)),
        out_shape=jax.ShapeDtypeStruct((m, n), BF16),
        compiler_params=_params("parallel", "parallel"),
        name="merge",
    )(h, ya1, ya2, yb1, yb2, w_g_t, w_g_t, w_a, w_b)


def _out_norm_kernel(u_ref, w_ref, xa_ref, xb_ref, g_ref, x1_ref, h_ref, *, na):
    x = jnp.where(pl.program_id(0) < na, xa_ref[...], xb_ref[...])
    x1 = x + jnp.dot(u_ref[...], w_ref[...], preferred_element_type=F32)
    x1_ref[...] = x1
    h_ref[...] = _rms(x1, g_ref[...]).astype(BF16)


def _out_norm(u, w, xa, xb, g, tm):
    m, k = u.shape
    d = xa.shape[1]
    return pl.pallas_call(
        functools.partial(_out_norm_kernel, na=xa.shape[0] // tm),
        grid=(m // tm,),
        in_specs=[pl.BlockSpec((tm, k), lambda i: (i, 0)),
                  pl.BlockSpec((k, d), lambda i: (0, 0)),
                  *_split_rows(xa, xb, tm),
                  pl.BlockSpec((1, d), lambda i: (0, 0))],
        out_specs=[pl.BlockSpec((tm, d), lambda i: (i, 0)),
                   pl.BlockSpec((tm, d), lambda i: (i, 0))],
        out_shape=[jax.ShapeDtypeStruct((m, d), F32), jax.ShapeDtypeStruct((m, d), BF16)],
        compiler_params=_params("parallel"),
        name="out_norm",
    )(u, w, xa, xb, g)


def _memkv_kernel(m_ref, g_ref, wk_ref, wv_ref, k_ref, v_ref):
    mn = _rms(m_ref[...], g_ref[...]).astype(BF16)
    k_ref[...] = jnp.dot(mn, wk_ref[...], preferred_element_type=F32)
    v_ref[...] = jnp.dot(mn, wv_ref[...], preferred_element_type=F32)


def _memkv(mem, g, wk, wv, tm):
    m, d = mem.shape
    n = wk.shape[1]
    return pl.pallas_call(
        _memkv_kernel,
        grid=(m // tm,),
        in_specs=[pl.BlockSpec((tm, d), lambda i: (i, 0)),
                  pl.BlockSpec((1, d), lambda i: (0, 0)),
                  pl.BlockSpec((d, n), lambda i: (0, 0)),
                  pl.BlockSpec((d, n), lambda i: (0, 0))],
        out_specs=[pl.BlockSpec((tm, n), lambda i: (i, 0)), pl.BlockSpec((tm, n), lambda i: (i, 0))],
        out_shape=[jax.ShapeDtypeStruct((m, n), F32), jax.ShapeDtypeStruct((m, n), F32)],
        compiler_params=_params("parallel"),
        name="memkv",
    )(mem, g, wk, wv)


def _attn_kernel(q_ref, k_ref, v_ref, o_ref, *, nb, tq, heads, hd, interleaved):
    mem = k_ref.shape[1] // heads if interleaved else k_ref.shape[1]
    scale = hd ** -0.5
    units = [(b, h) for b in range(nb) for h in range(heads)]

    def head(ref, b, h):
        if interleaved:
            return ref[b, pl.ds(h, mem, stride=heads), :].astype(BF16)
        return ref[b, :, h * hd:(h + 1) * hd].astype(BF16)

    q = q_ref[...].astype(BF16)
    s = {(b, h): lax.dot_general(q[b * tq:(b + 1) * tq, h * hd:(h + 1) * hd], head(k_ref, b, h),
                                 (((1,), (1,)), ((), ())), preferred_element_type=F32) * scale for (b, h) in units}
    p = {u: jnp.exp(s[u] - jnp.max(s[u], axis=-1, keepdims=True)) for u in units}
    o = {(b, h): jnp.dot(p[(b, h)].astype(BF16), head(v_ref, b, h), preferred_element_type=F32)
         / jnp.sum(p[(b, h)], axis=-1, keepdims=True) for (b, h) in units}
    o_ref[...] = jnp.concatenate([jnp.concatenate([o[(b, h)] for h in range(heads)], axis=1) for b in range(nb)],
                                 axis=0).astype(BF16)


def _attn(q, row0, batch, seq, nb, tq, k, v, heads, hd, interleaved):
    nt = seq // tq
    rows = nb * tq
    rb0 = row0 // rows
    n = heads * hd
    return pl.pallas_call(
        functools.partial(_attn_kernel, nb=nb, tq=tq, heads=heads, hd=hd, interleaved=interleaved),
        grid=(batch // nb, nt),
        in_specs=[pl.BlockSpec((rows, n), lambda b, t: (rb0 + b * nt + t, 0)),
                  pl.BlockSpec((nb,) + k.shape[1:], lambda b, t: (b, 0, 0)),
                  pl.BlockSpec((nb,) + v.shape[1:], lambda b, t: (b, 0, 0))],
        out_specs=pl.BlockSpec((rows, n), lambda b, t: (b * nt + t, 0)),
        out_shape=jax.ShapeDtypeStruct((batch * seq, n), BF16),
        compiler_params=_params("parallel", "parallel"),
        name="attn",
    )(q, k, v)


def _xo_router_kernel(oa_ref, ob_ref, wo_ref, x1_ref, g_ref, rw_ref, rb_ref, x2_ref, h_ref, lg_ref, *, na):
    o = jnp.where(pl.program_id(0) < na, oa_ref[...], ob_ref[...]).astype(BF16)
    x2 = x1_ref[...] + jnp.dot(o, wo_ref[...], preferred_element_type=F32)
    x2_ref[...] = x2
    h = _rms(x2, g_ref[...])
    h_ref[...] = h
    lg_ref[...] = _dot3(h, rw_ref[...]) + rb_ref[...]


def _xo_router(oa, ob, wo, x1, g, rw, rb, tm):
    m, d = x1.shape
    ne = rw.shape[1]
    row = lambda w: pl.BlockSpec((tm, w), lambda i: (i, 0))
    full = lambda a: pl.BlockSpec(a.shape, lambda i: (0, 0))
    return pl.pallas_call(
        functools.partial(_xo_router_kernel, na=oa.shape[0] // tm),
        grid=(m // tm,),
        in_specs=[*_split_rows(oa, ob, tm), full(wo), row(d), full(g), full(rw), full(rb)],
        out_specs=[row(d), row(d), row(ne)],
        out_shape=[jax.ShapeDtypeStruct((m, d), F32), jax.ShapeDtypeStruct((m, d), F32),
                   jax.ShapeDtypeStruct((m, ne), F32)],
        compiler_params=_params("parallel"),
        name="xo_router",
    )(oa, ob, wo, x1, g, rw, rb)


def _moe_kernel(e_ref, r0_ref, n_ref, tok_ref, dst_ref, h_hbm, wg_ref, wu_ref, wd_ref, bg_ref, bu_ref, bd_ref,
                slots_hbm, xbuf, acc, gsem, ssem, *, sub, big, nf):
    s = pl.program_id(0)
    f = pl.program_id(1)
    n = n_ref[s]
    r0 = r0_ref[s]
    d = acc.shape[2]
    s_next = jnp.minimum(s + 1, pl.num_programs(0) - 1)
    n_next = jnp.where(s + 1 < pl.num_programs(0), n_ref[s_next], 0)
    r0_next = r0_ref[s_next]

    @pl.when(jnp.logical_and(s == 0, f == 0))
    def _():
        xbuf[...] = jnp.zeros_like(xbuf)

    def row_in(base, c, k):
        tok = tok_ref[base + c * SUBLANES + k]
        return pltpu.make_async_copy(h_hbm.at[pl.ds(tok, 1), :], xbuf.at[c, pl.ds(k, 1), :], gsem)

    def row_out(c, k):
        dst = dst_ref[r0 + c * SUBLANES + k]
        return pltpu.make_async_copy(acc.at[c, pl.ds(k, 1), :], slots_hbm.at[pl.ds(dst, 1), :], ssem)

    def for_rows(count, fn):
        nfull = count // SUBLANES

        def group(c, carry):
            for k in range(SUBLANES):
                fn(c, k)
            return carry

        lax.fori_loop(0, nfull, group, 0)
        lax.fori_loop(0, count - nfull * SUBLANES, lambda k, carry: (fn(nfull, k), carry)[1], 0)

    @pl.when(n > 0)
    def _active():
        @pl.when(f == 0)
        def _gather():
            @pl.when(s == 0)
            def _first():
                for_rows(n, lambda c, k: row_in(r0, c, k).start())

            for_rows(n, lambda c, k: row_in(r0, c, k).wait())

        bg = bg_ref[0]
        bu = bu_ref[0]
        bd = bd_ref[0]

        nsub = (n + sub - 1) // sub
        sub8 = sub // SUBLANES
        big8 = big // SUBLANES

        @pl.when(f == 0)
        def _init():
            def fill(j, carry):
                acc[pl.ds(pl.multiple_of(j * sub8, sub8), sub8)] = jnp.broadcast_to(bd, (sub8, SUBLANES, d))
                return carry

            lax.fori_loop(0, nsub, fill, 0)

        def block(off8, m):
            m8 = m // SUBLANES
            x = xbuf[pl.ds(off8, m8)].reshape(m, d).astype(BF16)
            gate = jnp.dot(x, wg_ref[0].astype(BF16), preferred_element_type=F32) + bg
            up = jnp.dot(x, wu_ref[0].astype(BF16), preferred_element_type=F32) + bu
            gate = jnp.minimum(gate, SWIGLU_LIMIT)
            up = jnp.clip(up, -SWIGLU_LIMIT, SWIGLU_LIMIT)
            act = gate * _sigmoid(SWIGLU_ALPHA * gate) * (up + 1.0)
            part = jnp.dot(act.astype(BF16), wd_ref[0].astype(BF16), preferred_element_type=F32)
            acc[pl.ds(off8, m8)] += part.reshape(m8, SUBLANES, d)

        per_big = big // sub
        nbig = nsub // per_big
        tail = nsub - nbig * per_big

        def big_block(j, carry):
            block(pl.multiple_of(j * big8, big8), big)
            return carry

        lax.fori_loop(0, nbig, big_block, 0)
        for t in range(1, per_big):
            @pl.when(tail == t)
            def _tail(t=t):
                block(pl.multiple_of(nbig * big8, sub8), t * sub)

        @pl.when(f == nf - 1)
        def _scatter():
            for_rows(n, lambda c, k: row_out(c, k).start())
            for_rows(n_next, lambda c, k: row_in(r0_next, c, k).start())
            for_rows(n, lambda c, k: row_out(c, k).wait())


def _moe(h, sb_e, sb_r0, sb_n, tok, dst, wg, wu, wd, bg, bu, bd, cap, sub, big, tf):
    n_tok, d = h.shape
    ne, _, dff = wg.shape
    nf = dff // tf
    g = sb_e.shape[0]
    fi = lambda s, f, n_ref: jnp.where(n_ref[s] > 0, f, nf - 1)
    grid_spec = pltpu.PrefetchScalarGridSpec(
        num_scalar_prefetch=5,
        grid=(g, nf),
        in_specs=[pl.BlockSpec(memory_space=pl.ANY),
                  pl.BlockSpec((1, d, tf), lambda s, f, e, r, n, t, o: (e[s], 0, fi(s, f, n))),
                  pl.BlockSpec((1, d, tf), lambda s, f, e, r, n, t, o: (e[s], 0, fi(s, f, n))),
                  pl.BlockSpec((1, tf, d), lambda s, f, e, r, n, t, o: (e[s], fi(s, f, n), 0)),
                  pl.BlockSpec((1, 1, tf), lambda s, f, e, r, n, t, o: (e[s], 0, fi(s, f, n))),
                  pl.BlockSpec((1, 1, tf), lambda s, f, e, r, n, t, o: (e[s], 0, fi(s, f, n))),
                  pl.BlockSpec((1, 1, d), lambda s, f, e, r, n, t, o: (e[s], 0, 0))],
        out_specs=pl.BlockSpec(memory_space=pl.ANY),
        scratch_shapes=[pltpu.VMEM((cap // SUBLANES, SUBLANES, d), F32),
                        pltpu.VMEM((cap // SUBLANES, SUBLANES, d), F32),
                        pltpu.SemaphoreType.DMA(()), pltpu.SemaphoreType.DMA(())],
    )
    return pl.pallas_call(
        functools.partial(_moe_kernel, sub=sub, big=big, nf=nf),
        grid_spec=grid_spec,
        out_shape=jax.ShapeDtypeStruct((n_tok * TOP_K, d), F32),
        compiler_params=_params("arbitrary", "arbitrary"),
        name="moe",
    )(sb_e, sb_r0, sb_n, tok, dst, h, wg, wu, wd, bg.reshape(ne, 1, dff), bu.reshape(ne, 1, dff),
      bd.reshape(ne, 1, d))


def _route(logits, cap, n_sb_max):
    n_tok, ne = logits.shape
    top_val, top_idx = lax.top_k(logits, TOP_K)
    gates = jax.nn.softmax(top_val, axis=-1)
    e_flat = top_idx.reshape(-1).astype(jnp.int32)
    codes = jnp.argsort(e_flat, stable=True).astype(jnp.int32)
    counts = jnp.zeros((ne,), jnp.int32).at[e_flat].add(1)
    start = jnp.cumsum(counts) - counts
    n_sb = (counts + cap - 1) // cap
    sb_end = jnp.cumsum(n_sb)
    sb_start = sb_end - n_sb
    total = sb_end[-1]
    s = jnp.arange(n_sb_max, dtype=jnp.int32)
    e_s = jnp.minimum(jnp.searchsorted(sb_end, s, side='right'), ne - 1).astype(jnp.int32)
    e_last = e_s[jnp.maximum(total - 1, 0)]
    active = s < total
    e_s = jnp.where(active, e_s, e_last)
    within = s - sb_start[e_s]
    r0 = jnp.where(active, start[e_s] + within * cap, 0).astype(jnp.int32)
    n = jnp.where(active, jnp.clip(counts[e_s] - within * cap, 0, cap), 0).astype(jnp.int32)
    tok = lax.shift_right_logical(codes, TOP_K_SHIFT)
    dst = (codes & (TOP_K - 1)) * n_tok + tok
    return gates, tok, dst, e_s, r0, n


def _final_kernel(x2_ref, s0_ref, s1_ref, s2_ref, s3_ref, gates_ref, g_ref, y_ref):
    x = x2_ref[...]
    gt = gates_ref[...]
    for k, s_ref in enumerate((s0_ref, s1_ref, s2_ref, s3_ref)):
        x = x + s_ref[...] * gt[:, k:k + 1]
    y_ref[...] = _rms(x, g_ref[...])


def _final(x2, slots, gates, g, row0, rows, tm):
    n_tok, d = x2.shape
    rb0 = row0 // tm
    nblk = n_tok // tm
    slot = lambda k: pl.BlockSpec((tm, d), lambda i: (k * nblk + rb0 + i, 0))
    return pl.pallas_call(
        _final_kernel,
        grid=(rows // tm,),
        in_specs=[pl.BlockSpec((tm, d), lambda i: (rb0 + i, 0)),
                  slot(0), slot(1), slot(2), slot(3),
                  pl.BlockSpec((tm, TOP_K), lambda i: (rb0 + i, 0)),
                  pl.BlockSpec((1, d), lambda i: (0, 0))],
        out_specs=pl.BlockSpec((tm, d), lambda i: (i, 0)),
        out_shape=jax.ShapeDtypeStruct((rows, d), F32),
        compiler_params=_params("parallel"),
        name="final",
    )(x2, slots, slots, slots, slots, gates, g)


def _pad_tail(buf):
    return jnp.pad(buf, ((0, 0), (SUBLANES - (CONV_W - 1), 0), (0, 0)))


def kernel(x_prompt, x_sample, mem_prompt, state_lru_conv, state_lru_h, state_gdn_conv, state_gdn_S, cache_mem_k, cache_mem_v, norm_mix_g, w_in, lru_conv_w, lru_conv_b, lru_wx, lru_bx, lru_wa, lru_ba, lru_a_param, gdn_conv_w, gdn_A_log, gdn_dt_bias, gdn_norm_g, w_branch_a, w_branch_b, w_out, norm_xa_g, norm_mem_g, xa_wq, xa_wk, xa_wv, xa_wo, norm_moe_g, router_w, router_b, moe_w_gate, moe_b_gate, moe_w_up, moe_b_up, moe_w_down, moe_b_down, norm_final_g):
    depth = w_in.shape[0]
    assert depth == 1
    bp, tp, d = x_prompt.shape
    bs, ts, _ = x_sample.shape
    np_, ns = bp * tp, bs * ts
    n_tok = np_ + ns
    lw = lru_conv_w.shape[2]
    heads, dk, dv = state_gdn_S.shape[2:]
    kdim = heads * dk
    mem_len = mem_prompt.shape[1]
    xa_heads, xa_hd = cache_mem_k.shape[3:]
    xa_dim = xa_heads * xa_hd
    ne = router_w.shape[2]
    past_len = 16384
    gdn_chunk = 64

    row = lambda v: v.reshape(1, -1)
    xp = x_prompt.reshape(np_, d)
    xs = x_sample.reshape(ns, d)

    c_main = 2 * lw + 3 * kdim + heads * dv
    w_in_t = jnp.swapaxes(w_in[0], 0, 1)
    w_ba_t = w_in_t[c_main:c_main + 2 * heads].astype(BF16)
    w_g_t = w_in_t[c_main + 2 * heads:].astype(BF16)
    w_gates = jnp.concatenate([lru_wx[0], lru_wa[0]], axis=-1).astype(BF16)

    rows_gcd = math.gcd(np_, ns)
    tm_s = _tile(rows_gcd, 256)
    tm_m = _tile(rows_gcd, 512)
    tm_l = _tile(rows_gcd, 1024)
    h1, ba = _norm_ba(xp, xs, row(norm_mix_g[0]), w_ba_t, tm_m)
    proj = _matmul(h1, w_in_t, c_main, tm_l, _tile(c_main, 1024), F32, "in_proj", w_is_t=True)

    lru_args = (lru_conv_w[0], row(lru_conv_b[0]), w_gates, row(lru_bx[0]), row(lru_ba[0]), row(lru_a_param[0]))
    ya_p, lru_h_p = _lru_seq(proj, 0, bp, tp, _tile(tp, 256), jnp.zeros((bp, SUBLANES, lw), F32),
                             jnp.zeros((bp, 1, lw), F32), *lru_args, pos0=0)
    if ts == SUBLANES:
        ya_s, lru_h_s = _lru_short(proj, np_, bs, math.gcd(bs, 32), _pad_tail(state_lru_conv[0]),
                                   state_lru_h[0][:, None, :], *lru_args)
    else:
        ya_s, lru_h_s = _lru_seq(proj, np_, bs, ts, ts, _pad_tail(state_lru_conv[0]), state_lru_h[0][:, None, :],
                                 *lru_args, pos0=past_len)

    gdn_args = (gdn_conv_w[0], row(gdn_A_log[0]), row(gdn_dt_bias[0]), row(gdn_norm_g[0]), heads, dk, dv)
    cp = min(gdn_chunk, tp)
    assert tp % cp == 0 and ts <= gdn_chunk
    gp = 4 if (tp // cp) % 4 == 0 else 1
    yb_p, gdn_s_p = _gdn(proj, 2 * lw, 0, bp, tp, cp, gp, cp, True, ba,
                         jnp.zeros((bp, SUBLANES, 3 * kdim), F32), jnp.zeros((bp, heads, dk, dv), F32), *gdn_args)
    gr_s = ts * math.gcd(bs, max(gdn_chunk // ts, 1))
    gs = 2 if (ns // gr_s) % 2 == 0 else 1
    yb_s, gdn_s_s = _gdn(proj, 2 * lw, np_, bs, ts, ts, gs, gr_s, False, ba,
                         _pad_tail(state_gdn_conv[0]), state_gdn_S[0], *gdn_args)

    u = _merge(h1, ya_p, ya_s, yb_p, yb_s, w_g_t, w_branch_a[0].astype(BF16), w_branch_b[0].astype(BF16), tm_l,
               _tile(d, 512))
    x1, h2 = _out_norm(u, w_out[0].astype(BF16), xp, xs, row(norm_xa_g[0]), tm_m)

    k_p, v_p = _memkv(mem_prompt.reshape(bp * mem_len, d), row(norm_mem_g[0]), xa_wk[0].astype(BF16),
                      xa_wv[0].astype(BF16), _tile(bp * mem_len, 256))
    q = _matmul(h2, xa_wq[0], xa_dim, tm_l, xa_dim, BF16, "xa_q")
    o_p = _attn(q, 0, bp, tp, 1, _tile(tp, 512), k_p.reshape(bp, mem_len, xa_dim), v_p.reshape(bp, mem_len, xa_dim),
                xa_heads, xa_hd, False)
    nb_s = math.gcd(bs, 4)
    o_s = _attn(q, np_, bs, ts, nb_s, ts, cache_mem_k[0].reshape(bs, mem_len * xa_heads, xa_hd),
                cache_mem_v[0].reshape(bs, mem_len * xa_heads, xa_hd), xa_heads, xa_hd, True)
    x2, h3, logits = _xo_router(o_p, o_s, xa_wo[0].astype(BF16), x1, row(norm_moe_g[0]), router_w[0],
                                row(router_b[0]), tm_m)

    cap, sub, big, tf = 1536, 128, 512, _tile(moe_w_gate.shape[3], 256)
    n_sb_max = (n_tok * TOP_K + ne * (cap - 1)) // cap
    gates, tok, dst, sb_e, sb_r0, sb_n = _route(logits, cap, n_sb_max)
    slots = _moe(h3, sb_e, sb_r0, sb_n, tok, dst, moe_w_gate[0], moe_w_up[0], moe_w_down[0],
                 moe_b_gate[0], moe_b_up[0], moe_b_down[0], cap, sub, big, tf)
    y_p =_final(x2, slots, gates, row(norm_final_g), 0, np_, tm_s)
    y_s = _final(x2, slots, gates, row(norm_final_g), np_, ns, tm_s)

    keep = CONV_W - 1
    pp = jnp.stack([lax.slice(proj, ((b + 1) * tp - keep, 0), ((b + 1) * tp, c_main)) for b in range(bp)])
    ps = proj[np_:].reshape(bs, ts, c_main)
    return (y_p.reshape(bp, tp, d), y_s.reshape(bs, ts, d),
            pp[:, :, :lw][None], lru_h_p.reshape(1, bp, lw),
            pp[:, :, 2 * lw:2 * lw + 3 * kdim][None], gdn_s_p[None],
            k_p.reshape(1, bp, mem_len, xa_heads, xa_hd), v_p.reshape(1, bp, mem_len, xa_heads, xa_hd),
            ps[:, ts - keep:, :lw][None], lru_h_s.reshape(1, bs, lw),
            ps[:, ts - keep:, 2 * lw:2 * lw + 3 * kdim][None], gdn_s_s[None])
```

```python
import functools
import math

import jax
import jax.numpy as jnp
from jax import lax
from jax.experimental import pallas as pl
from jax.experimental.pallas import tpu as pltpu

F32 = jnp.float32
BF16 = jnp.bfloat16
NORM_EPS = 1e-6
LRU_C = 8.0
CONV_W = 4
SWIGLU_ALPHA = 1.702
SWIGLU_LIMIT = 7.0
TOP_K = 4
TOP_K_SHIFT = 2
VMEM_LIMIT_BYTES = 56 * 1024 * 1024
SUBLANES = 8
LANES = 128


def _params(*sem):
    return pltpu.CompilerParams(dimension_semantics=sem, vmem_limit_bytes=VMEM_LIMIT_BYTES)


def _bdot(a, b):
    return jnp.dot(a.astype(BF16), b.astype(BF16), preferred_element_type=F32)


def _bdot_nt(a, b):
    return lax.dot_general(a.astype(BF16), b.astype(BF16), (((1,), (1,)), ((), ())), preferred_element_type=F32)


def _dot_nt(a, b):
    return lax.dot_general(a, b, (((1,), (1,)), ((), ())), preferred_element_type=F32)


def _bdot_tn(a, b):
    return lax.dot_general(a.astype(BF16), b.astype(BF16), (((0,), (0,)), ((), ())), preferred_element_type=F32)


def _split2(x):
    hi = x.astype(BF16)
    lo = (x - hi.astype(F32)).astype(BF16)
    return hi, lo


def _split3(x):
    p1 = x.astype(BF16)
    r1 = x - p1.astype(F32)
    p2 = r1.astype(BF16)
    p3 = (r1 - p2.astype(F32)).astype(BF16)
    return p1, p2, p3


def _dot3(a, b):
    ah, al = _split2(a)
    bh, bl = _split2(b)
    d = functools.partial(jnp.dot, preferred_element_type=F32)
    return d(ah, bh) + (d(ah, bl) + d(al, bh))


def _dot_exact_lhs(m_bf16, x):
    d = functools.partial(jnp.dot, preferred_element_type=F32)
    p1, p2, p3 = _split3(x)
    return d(m_bf16, p1) + (d(m_bf16, p2) + d(m_bf16, p3))


def _dot_exact_rhs(x, m_bf16):
    d = functools.partial(jnp.dot, preferred_element_type=F32)
    p1, p2, p3 = _split3(x)
    return d(p1, m_bf16) + (d(p2, m_bf16) + d(p3, m_bf16))


def _sigmoid(x):
    return jax.nn.sigmoid(x)


def _tile(n, pref):
    if n <= pref:
        return n
    t = pref - pref % SUBLANES
    while n % t:
        t -= SUBLANES
    return t


def _softplus(x):
    return jnp.maximum(x, 0.0) + jnp.log1p(jnp.exp(-jnp.abs(x)))


def _gelu_tanh(x):
    return 0.5 * x * (1.0 + jnp.tanh(0.7978845608028654 * (x + 0.044715 * (x * x * x))))


def _rms(x, g):
    r = lax.rsqrt(jnp.mean(x * x, axis=-1, keepdims=True) + NORM_EPS)
    return x * r * g


def _split_rows(a, b, tm):
    na = a.shape[0] // tm
    w = a.shape[1]
    return (pl.BlockSpec((tm, w), lambda i, *_: (jnp.minimum(i, na - 1), 0)),
            pl.BlockSpec((tm, w), lambda i, *_: (jnp.maximum(i - na, 0), 0)))


def _norm_ba_kernel(xa_ref, xb_ref, g_ref, wba_ref, h_ref, ba_ref, *, na):
    x = jnp.where(pl.program_id(0) < na, xa_ref[...], xb_ref[...])
    h = _rms(x, g_ref[...]).astype(BF16)
    h_ref[...] = h
    ba_ref[...] = _dot_nt(h, wba_ref[...])


def _norm_ba(xa, xb, g, w_ba_t, tm):
    d = xa.shape[1]
    m = xa.shape[0] + xb.shape[0]
    nb = w_ba_t.shape[0]
    return pl.pallas_call(
        functools.partial(_norm_ba_kernel, na=xa.shape[0] // tm),
        grid=(m // tm,),
        in_specs=[*_split_rows(xa, xb, tm),
                  pl.BlockSpec((1, d), lambda i: (0, 0)),
                  pl.BlockSpec((nb, d), lambda i: (0, 0))],
        out_specs=[pl.BlockSpec((tm, d), lambda i: (i, 0)),
                   pl.BlockSpec((tm, nb), lambda i: (i, 0))],
        out_shape=[jax.ShapeDtypeStruct((m, d), BF16), jax.ShapeDtypeStruct((m, nb), F32)],
        compiler_params=_params("parallel"),
        name="norm_ba",
    )(xa, xb, g, w_ba_t)


def _mm_kernel(a_ref, w_ref, o_ref, *, w_is_t):
    w = w_ref[...].astype(BF16)
    acc = _dot_nt(a_ref[...], w) if w_is_t else jnp.dot(a_ref[...], w, preferred_element_type=F32)
    o_ref[...] = acc.astype(o_ref.dtype)


def _matmul(a, w, n_cols, tm, tn, out_dtype, name, w_is_t=False):
    m, k = a.shape
    w_spec = pl.BlockSpec((tn, k), lambda j, i: (j, 0)) if w_is_t else pl.BlockSpec((k, tn), lambda j, i: (0, j))
    return pl.pallas_call(
        functools.partial(_mm_kernel, w_is_t=w_is_t),
        grid=(n_cols // tn, m // tm),
        in_specs=[pl.BlockSpec((tm, k), lambda j, i: (i, 0)), w_spec],
        out_specs=pl.BlockSpec((tm, tn), lambda j, i: (i, j)),
        out_shape=jax.ShapeDtypeStruct((m, n_cols), out_dtype),
        compiler_params=_params("parallel", "parallel"),
        name=name,
    )(a, w)


def _conv4_into(xe_ref, rows, cw):
    y = xe_ref[SUBLANES:SUBLANES + rows, :] * cw[CONV_W - 1:CONV_W]
    for s in range(1, CONV_W):
        y = y + xe_ref[SUBLANES - s:SUBLANES - s + rows, :] * cw[CONV_W - 1 - s:CONV_W - s]
    return y


def _scan8(a3, b3):
    row = lax.broadcasted_iota(jnp.int32, a3.shape, 1)
    for s in (1, 2, 4):
        a_sh = pltpu.roll(a3, s, axis=1)
        b_sh = pltpu.roll(b3, s, axis=1)
        m = row >= s
        b3 = jnp.where(m, a3 * b_sh + b3, b3)
        a3 = jnp.where(m, a3 * a_sh, a3)
    return a3, b3


def _lru_gates(xc, wg_ref, bx, bga, ap, first_row_is_pos0):
    nblk = wg_ref.shape[0]
    bw = wg_ref.shape[1]
    xcb = xc.astype(BF16)
    gx, ga = [], []
    for n in range(nblk):
        r = jnp.dot(xcb[:, n * bw:(n + 1) * bw], wg_ref[n], preferred_element_type=F32)
        gx.append(r[:, :bw])
        ga.append(r[:, bw:])
    gate_x = _sigmoid(jnp.concatenate(gx, axis=1) + bx)
    gate_a = _sigmoid(jnp.concatenate(ga, axis=1) + bga)
    log_a = (-LRU_C) * gate_a * _softplus(ap)
    a = jnp.exp(log_a)
    mult = jnp.sqrt(-jnp.tanh(log_a) * (a * a + 1.0))
    if first_row_is_pos0 is not None:
        mult = jnp.where(first_row_is_pos0, 1.0, mult)
    return a, xc * gate_x * mult


def _lru_seq_kernel(xa_ref, ya_ref, tail0_ref, h0_ref, cw_ref, cb_ref, wg_ref, bx_ref, bga_ref, ap_ref,
                    y_ref, hl_ref, xe_ref, a_ref, b_ref, h_ref, *, tt, pos0):
    t = pl.program_id(1)
    w = xa_ref.shape[1]

    @pl.when(t == 0)
    def _():
        xe_ref[0:SUBLANES, :] = tail0_ref[0]
        h_ref[...] = h0_ref[0]

    xe_ref[SUBLANES:SUBLANES + tt, :] = xa_ref[...]
    xc = _conv4_into(xe_ref, tt, cw_ref[...]) + cb_ref[...]
    xe_ref[0:SUBLANES, :] = xe_ref[tt:tt + SUBLANES, :]
    first = None
    if pos0 == 0:
        first = (lax.broadcasted_iota(jnp.int32, (tt, 1), 0) + t * tt) == 0
    a, b = _lru_gates(xc, wg_ref, bx_ref[...], bga_ref[...], ap_ref[...], first)
    a3, b3 = _scan8(a.reshape(tt // SUBLANES, SUBLANES, w), b.reshape(tt // SUBLANES, SUBLANES, w))
    a_ref[...] = a3
    b_ref[...] = b3

    def body(g, h):
        hg = a_ref[g] * h + b_ref[g]
        b_ref[g] = hg
        return hg[SUBLANES - 1:SUBLANES, :]

    h = lax.fori_loop(0, tt // SUBLANES, body, h_ref[...])
    h_ref[...] = h
    hl_ref[0] = h
    hs = b_ref[...].reshape(tt, w)
    y_ref[...] = (hs * _gelu_tanh(ya_ref[...])).astype(BF16)


def _lru_seq(proj, row0, batch, seq, tt, tail0, h0, cw, cb, wg, bx, bga, ap, pos0):
    w = cw.shape[1]
    nt = seq // tt
    rb0 = row0 // tt
    vec = lambda: pl.BlockSpec((1, w), lambda b, t: (0, 0))
    return pl.pallas_call(
        functools.partial(_lru_seq_kernel, tt=tt, pos0=pos0),
        grid=(batch, nt),
        in_specs=[pl.BlockSpec((tt, w), lambda b, t: (rb0 + b * nt + t, 0)),
                  pl.BlockSpec((tt, w), lambda b, t: (rb0 + b * nt + t, 1)),
                  pl.BlockSpec((1, SUBLANES, w), lambda b, t: (b, 0, 0)),
                  pl.BlockSpec((1, 1, w), lambda b, t: (b, 0, 0)),
                  pl.BlockSpec((CONV_W, w), lambda b, t: (0, 0)),
                  vec(),
                  pl.BlockSpec(wg.shape, lambda b, t: (0, 0, 0)),
                  vec(), vec(), vec()],
        out_specs=[pl.BlockSpec((tt, w), lambda b, t: (b * nt + t, 0)),
                   pl.BlockSpec((1, 1, w), lambda b, t: (b, 0, 0))],
        out_shape=[jax.ShapeDtypeStruct((batch * seq, w), BF16),
                   jax.ShapeDtypeStruct((batch, 1, w), F32)],
        scratch_shapes=[pltpu.VMEM((SUBLANES + tt, w), F32),
                        pltpu.VMEM((tt // SUBLANES, SUBLANES, w), F32),
                        pltpu.VMEM((tt // SUBLANES, SUBLANES, w), F32),
                        pltpu.VMEM((1, w), F32)],
        compiler_params=_params("parallel", "arbitrary"),
        name="lru_seq",
    )(proj, proj, tail0, h0, cw, cb, wg, bx, bga, ap)


def _lru_short_kernel(xa_ref, ya_ref, tail0_ref, h0_ref, cw_ref, cb_ref, wg_ref, bx_ref, bga_ref, ap_ref,
                      y_ref, hl_ref, xe_ref):
    nb = xe_ref.shape[0]
    w = xa_ref.shape[1]
    xe_ref[:, 0:SUBLANES, :] = tail0_ref[...]
    xe_ref[:, SUBLANES:, :] = xa_ref[...].reshape(nb, SUBLANES, w)
    cw = cw_ref[...]
    xc = xe_ref[:, SUBLANES:, :] * cw[CONV_W - 1:CONV_W]
    for s in range(1, CONV_W):
        xc = xc + xe_ref[:, SUBLANES - s:2 * SUBLANES - s, :] * cw[CONV_W - 1 - s:CONV_W - s]
    xc = xc.reshape(nb * SUBLANES, w) + cb_ref[...]
    a, b = _lru_gates(xc, wg_ref, bx_ref[...], bga_ref[...], ap_ref[...], None)
    a3, b3 = _scan8(a.reshape(nb, SUBLANES, w), b.reshape(nb, SUBLANES, w))
    hs = a3 * h0_ref[...] + b3
    hl_ref[...] = hs[:, SUBLANES - 1:SUBLANES, :]
    y_ref[...] = (hs.reshape(nb * SUBLANES, w) * _gelu_tanh(ya_ref[...])).astype(BF16)


def _lru_short(proj, row0, batch, nb, tail0, h0, cw, cb, wg, bx, bga, ap):
    w = cw.shape[1]
    rows = nb * SUBLANES
    rb0 = row0 // rows
    vec = lambda: pl.BlockSpec((1, w), lambda b: (0, 0))
    return pl.pallas_call(
        _lru_short_kernel,
        grid=(batch // nb,),
        in_specs=[pl.BlockSpec((rows, w), lambda b: (rb0 + b, 0)),
                  pl.BlockSpec((rows, w), lambda b: (rb0 + b, 1)),
                  pl.BlockSpec((nb, SUBLANES, w), lambda b: (b, 0, 0)),
                  pl.BlockSpec((nb, 1, w), lambda b: (b, 0, 0)),
                  pl.BlockSpec((CONV_W, w), lambda b: (0, 0)),
                  vec(),
                  pl.BlockSpec(wg.shape, lambda b: (0, 0, 0)),
                  vec(), vec(), vec()],
        out_specs=[pl.BlockSpec((rows, w), lambda b: (b, 0)),
                   pl.BlockSpec((nb, 1, w), lambda b: (b, 0, 0))],
        out_shape=[jax.ShapeDtypeStruct((batch * SUBLANES, w), BF16),
                   jax.ShapeDtypeStruct((batch, 1, w), F32)],
        scratch_shapes=[pltpu.VMEM((nb, 2 * SUBLANES, w), F32)],
        compiler_params=_params("parallel"),
        name="lru_short",
    )(proj, proj, tail0, h0, cw, cb, wg, bx, bga, ap)


def _gdn_kernel(q_ref, k_ref, v_ref, z_ref, ba_ref, bat_ref, tail0_ref, s0_ref, cw_ref, alog_ref, dtb_ref,
                alogt_ref, dtbt_ref, ng_ref, y_ref, sout_ref, xe_ref, s_ref, *, groups, gr, c, carry, heads, dk, dv):
    ci = pl.program_id(1)
    kdim = heads * dk
    rows = groups * gr
    nseg = gr // c
    nconv, crow = xe_ref.shape[0], xe_ref.shape[1] - SUBLANES
    hs = range(heads)
    units = [(gi, h) for gi in range(groups) for h in hs]

    @pl.when(ci == 0)
    def _():
        xe_ref[:, 0:SUBLANES, :] = tail0_ref[...]
        if carry:
            s_ref[...] = s0_ref[0]

    xe_ref[:, SUBLANES:, 0:kdim] = q_ref[...].reshape(nconv, crow, kdim)
    xe_ref[:, SUBLANES:, kdim:2 * kdim] = k_ref[...].reshape(nconv, crow, kdim)
    xe_ref[:, SUBLANES:, 2 * kdim:] = v_ref[...].reshape(nconv, crow, kdim)
    cw = cw_ref[...]
    pre = xe_ref[:, SUBLANES:, :] * cw[CONV_W - 1:CONV_W]
    for s in range(1, CONV_W):
        pre = pre + xe_ref[:, SUBLANES - s:SUBLANES - s + crow, :] * cw[CONV_W - 1 - s:CONV_W - s]
    xe_ref[:, 0:SUBLANES, :] = xe_ref[:, crow:crow + SUBLANES, :]
    pre = pre.reshape(rows, 3 * kdim)
    qkv = pre * _sigmoid(pre)

    ba = ba_ref[...]
    beta = _sigmoid(ba[:, 0:heads])
    g = -jnp.exp(alog_ref[...]) * _softplus(ba[:, heads:2 * heads] + dtb_ref[...])
    ii = lax.broadcasted_iota(jnp.int32, (gr, gr), 0)
    jj = lax.broadcasted_iota(jnp.int32, (gr, gr), 1)
    same = (ii // c) == (jj // c)
    incl = jnp.logical_and(same, ii >= jj)
    strict = jnp.logical_and(same, ii > jj)
    eye = (ii == jj).astype(F32)
    incl_b = incl.astype(BF16)
    inclt_b = jnp.logical_and(same, ii <= jj).astype(BF16)
    same_b = same.astype(BF16)
    z = z_ref[...]
    ng = ng_ref[...]
    scale = dk ** -0.5

    gcs, gcts, egcs, ekds, gtots = [], [], [], [], []
    for gi in range(groups):
        gg = g[gi * gr:(gi + 1) * gr]
        gt = -jnp.exp(alogt_ref[...]) * _softplus(bat_ref[0, 0, gi][heads:2 * heads, :] + dtbt_ref[...])
        gc = _dot_exact_lhs(incl_b, gg)
        gtot = _dot_exact_lhs(same_b, gg)
        gcs.append(gc)
        gcts.append(_dot_exact_rhs(gt, inclt_b))
        egcs.append(jnp.exp(gc))
        ekds.append(jnp.exp(gtot - gc))
        gtots.append(gtot)

    k_, kb_, dec_, rhs_, qs_ = {}, {}, {}, {}, {}
    for (gi, h) in units:
        r0 = gi * gr
        qh = qkv[r0:r0 + gr, h * dk:(h + 1) * dk]
        kh = qkv[r0:r0 + gr, kdim + h * dk:kdim + (h + 1) * dk]
        vh = qkv[r0:r0 + gr, 2 * kdim + h * dv:2 * kdim + (h + 1) * dv]
        qh = qh * lax.rsqrt(jnp.sum(qh * qh, axis=-1, keepdims=True) + NORM_EPS)
        kh = kh * lax.rsqrt(jnp.sum(kh * kh, axis=-1, keepdims=True) + NORM_EPS)
        bh = beta[r0:r0 + gr, h:h + 1]
        diff = gcs[gi][:, h:h + 1] - gcts[gi][h:h + 1, :]
        u = (gi, h)
        dec_[u] = jnp.where(incl, jnp.exp(jnp.where(incl, diff, 0.0)), 0.0)
        k_[u] = kh
        kb_[u] = kh * bh
        qs_[u] = qh * scale
        rhs_[u] = jnp.concatenate([vh * bh, kb_[u] * egcs[gi][:, h:h + 1]], axis=1)
    npow = {u: jnp.where(strict, -(_bdot_nt(kb_[u], k_[u]) * dec_[u]), 0.0) for u in units}
    qk_ = {u: _bdot_nt(qs_[u], k_[u]) * dec_[u] for u in units}
    p = {u: eye + npow[u] for u in units}
    lvl = 1
    while 2 * lvl < c:
        npow = {u: _dot3(npow[u], npow[u]) for u in units}
        p = {u: p[u] + _dot3(p[u], npow[u]) for u in units}
        lvl *= 2
    sol = {u: _dot3(p[u], rhs_[u]) for u in units}

    o_ = {}
    if carry:
        assert nseg == 1
        cur = {h: s_ref[h] for h in hs}
        for gi in range(groups):
            wq = {}
            for h in hs:
                u = (gi, h)
                lhs = jnp.concatenate([sol[u][:, dv:], qs_[u] * egcs[gi][:, h:h + 1]], axis=0)
                wq[h] = _bdot(lhs, cur[h])
            for h in hs:
                u = (gi, h)
                vnew = sol[u][:, :dv] - wq[h][:gr]
                o_[u] = wq[h][gr:] + _bdot(qk_[u], vnew)
                kd = k_[u] * ekds[gi][:, h:h + 1]
                glast = jnp.exp(gtots[gi][0:1, h:h + 1])
                cur[h] = cur[h] * glast + _bdot_tn(kd, vnew)
        for h in hs:
            s_ref[h] = cur[h]
        sout_ref[0] = s_ref[...]
    else:
        segs = [(gi, j, h) for gi in range(groups) for j in range(nseg) for h in hs]
        wq = {}
        for (gi, j, h) in segs:
            u = (gi, h)
            sl = slice(j * c, (j + 1) * c)
            lhs = jnp.concatenate([sol[u][sl, dv:], qs_[u][sl] * egcs[gi][sl, h:h + 1]], axis=0)
            wq[(gi, j, h)] = _bdot(lhs, s0_ref[gi * nseg + j, h])
        vn = {}
        for (gi, j, h) in segs:
            u = (gi, h)
            sl = slice(j * c, (j + 1) * c)
            vn[(gi, j, h)] = sol[u][sl, :dv] - wq[(gi, j, h)][:c]
            kd = k_[u][sl] * ekds[gi][sl, h:h + 1]
            glast = jnp.exp(gtots[gi][j * c:j * c + 1, h:h + 1])
            sout_ref[gi * nseg + j, h] = s0_ref[gi * nseg + j, h] * glast + _bdot_tn(kd, vn[(gi, j, h)])
        for u in units:
            gi, h = u
            vnew = jnp.concatenate([vn[(gi, j, h)] for j in range(nseg)], axis=0)
            os_ = jnp.concatenate([wq[(gi, j, h)][c:] for j in range(nseg)], axis=0)
            o_[u] = os_ + _bdot(qk_[u], vnew)

    ys = []
    for gi in range(groups):
        row = []
        for h in hs:
            zz = z[gi * gr:(gi + 1) * gr, h * dv:(h + 1) * dv]
            row.append(_rms(o_[(gi, h)], ng) * (zz * _sigmoid(zz)))
        ys.append(jnp.concatenate(row, axis=1))
    y_ref[...] = jnp.concatenate(ys, axis=0).astype(BF16)


def _gdn(proj, col0, row0, batch, seq, c, groups, gr, carry, ba, tail0, s0, cw, alog, dtb, ng, heads, dk, dv):
    kdim = heads * dk
    rows = groups * gr
    if carry:
        nb, nsteps, sb = batch, seq // rows, 1
    else:
        sb = rows // seq
        nb, nsteps = batch // sb, 1
    crow = rows // sb
    rb0 = row0 // rows
    cb0 = col0 // kdim
    rblk = lambda b, ci: rb0 + b * nsteps + ci
    small = lambda shape: pl.BlockSpec(shape, lambda b, ci: (0,) * len(shape))
    bat = ba[row0:row0 + batch * seq].reshape(nb, nsteps, groups, gr, 2 * heads)
    bat = jnp.swapaxes(bat, 3, 4)
    return pl.pallas_call(
        functools.partial(_gdn_kernel, groups=groups, gr=gr, c=c, carry=carry, heads=heads, dk=dk, dv=dv),
        grid=(nb, nsteps),
        in_specs=[pl.BlockSpec((rows, kdim), lambda b, ci: (rblk(b, ci), cb0)),
                  pl.BlockSpec((rows, kdim), lambda b, ci: (rblk(b, ci), cb0 + 1)),
                  pl.BlockSpec((rows, kdim), lambda b, ci: (rblk(b, ci), cb0 + 2)),
                  pl.BlockSpec((rows, kdim), lambda b, ci: (rblk(b, ci), cb0 + 3)),
                  pl.BlockSpec((rows, 2 * heads), lambda b, ci: (rblk(b, ci), 0)),
                  pl.BlockSpec((1, 1, groups, 2 * heads, gr), lambda b, ci: (b, ci, 0, 0, 0)),
                  pl.BlockSpec((sb, SUBLANES, 3 * kdim), lambda b, ci: (b, 0, 0)),
                  pl.BlockSpec((sb, heads, dk, dv), lambda b, ci: (b, 0, 0, 0)),
                  small((CONV_W, 3 * kdim)),
                  small((1, heads)), small((1, heads)), small((heads, 1)), small((heads, 1)),
                  small((1, dv))],
        out_specs=[pl.BlockSpec((rows, heads * dv), lambda b, ci: (b * nsteps + ci, 0)),
                   pl.BlockSpec((sb, heads, dk, dv), lambda b, ci: (b, 0, 0, 0))],
        out_shape=[jax.ShapeDtypeStruct((batch * seq, heads * dv), BF16),
                   jax.ShapeDtypeStruct((batch, heads, dk, dv), F32)],
        scratch_shapes=[pltpu.VMEM((sb, SUBLANES + crow, 3 * kdim), F32),
                        pltpu.VMEM((heads, dk, dv), F32)],
        compiler_params=_params("parallel", "arbitrary"),
        name="gdn",
    )(proj, proj, proj, proj, ba, bat, tail0, s0, cw, alog, dtb, alog.T, dtb.T, ng)


def _merge_kernel(h_ref, ya1_ref, ya2_ref, yb1_ref, yb2_ref, wga_ref, wgb_ref, wa_ref, wb_ref, u_ref, *, na):
    d = functools.partial(jnp.dot, preferred_element_type=F32)
    first = pl.program_id(0) < na
    h = h_ref[...]
    ya = jnp.where(first, ya1_ref[...], ya2_ref[...]).astype(BF16)
    yb = jnp.where(first, yb1_ref[...], yb2_ref[...]).astype(BF16)
    u = _sigmoid(_dot_nt(h, wga_ref[...].astype(BF16))) * d(ya, wa_ref[...])
    u = u + _sigmoid(_dot_nt(h, wgb_ref[...].astype(BF16))) * d(yb, wb_ref[...])
    u_ref[...] = u.astype(BF16)


def _merge(h, ya1, ya2, yb1, yb2, w_t, g_row0, w_a, w_b, tm, tn):
    m, d = h.shape
    n = w_a.shape[1]
    nj = n // tn
    assert g_row0 % SUBLANES == 0
    gate = lambda off: pl.BlockSpec((pl.Element(tn), pl.Element(d)),
                                    lambda i, j: (pl.multiple_of(g_row0 + off + j * tn, SUBLANES), 0))
    return pl.pallas_call(
        functools.partial(_merge_kernel, na=ya1.shape[0] // tm),
        grid=(m // tm, nj),
        in_specs=[pl.BlockSpec((tm, d), lambda i, j: (i, 0)),
                  *_split_rows(ya1, ya2, tm),
                  *_split_rows(yb1, yb2, tm),
                  gate(0),
                  gate(n),
                  pl.BlockSpec((w_a.shape[0], tn), lambda i, j: (0, j)),
                  pl.BlockSpec((w_b.shape[0], tn), lambda i, j: (0, j))],
        out_specs=pl.BlockSpec((tm, tn), lambda i, j: (i, j)),
        out_shape=jax.ShapeDtypeStruct((m, n), BF16),
        compiler_params=_params("parallel", "parallel"),
        name="merge",
    )(h, ya1, ya2, yb1, yb2, w_t, w_t, w_a, w_b)


def _out_norm_kernel(u_ref, w_ref, xa_ref, xb_ref, g_ref, x1_ref, h_ref, *, na):
    x = jnp.where(pl.program_id(0) < na, xa_ref[...], xb_ref[...])
    x1 = x + jnp.dot(u_ref[...], w_ref[...], preferred_element_type=F32)
    x1_ref[...] = x1
    h_ref[...] = _rms(x1, g_ref[...]).astype(BF16)


def _out_norm(u, w, xa, xb, g, tm):
    m, k = u.shape
    d = xa.shape[1]
    return pl.pallas_call(
        functools.partial(_out_norm_kernel, na=xa.shape[0] // tm),
        grid=(m // tm,),
        in_specs=[pl.BlockSpec((tm, k), lambda i: (i, 0)),
                  pl.BlockSpec((k, d), lambda i: (0, 0)),
                  *_split_rows(xa, xb, tm),
                  pl.BlockSpec((1, d), lambda i: (0, 0))],
        out_specs=[pl.BlockSpec((tm, d), lambda i: (i, 0)),
                   pl.BlockSpec((tm, d), lambda i: (i, 0))],
        out_shape=[jax.ShapeDtypeStruct((m, d), F32), jax.ShapeDtypeStruct((m, d), BF16)],
        compiler_params=_params("parallel"),
        name="out_norm",
    )(u, w, xa, xb, g)


def _memkv_kernel(m_ref, g_ref, wk_ref, wv_ref, k_ref, v_ref):
    mn = _rms(m_ref[...], g_ref[...]).astype(BF16)
    k_ref[...] = jnp.dot(mn, wk_ref[...], preferred_element_type=F32)
    v_ref[...] = jnp.dot(mn, wv_ref[...], preferred_element_type=F32)


def _memkv(mem, g, wk, wv, tm):
    m, d = mem.shape
    n = wk.shape[1]
    return pl.pallas_call(
        _memkv_kernel,
        grid=(m // tm,),
        in_specs=[pl.BlockSpec((tm, d), lambda i: (i, 0)),
                  pl.BlockSpec((1, d), lambda i: (0, 0)),
                  pl.BlockSpec((d, n), lambda i: (0, 0)),
                  pl.BlockSpec((d, n), lambda i: (0, 0))],
        out_specs=[pl.BlockSpec((tm, n), lambda i: (i, 0)), pl.BlockSpec((tm, n), lambda i: (i, 0))],
        out_shape=[jax.ShapeDtypeStruct((m, n), F32), jax.ShapeDtypeStruct((m, n), F32)],
        compiler_params=_params("parallel"),
        name="memkv",
    )(mem, g, wk, wv)


def _attn_kernel(q_ref, k_ref, v_ref, o_ref, *, nb, tq, heads, hd, interleaved):
    mem = k_ref.shape[1] // heads if interleaved else k_ref.shape[1]
    scale = hd ** -0.5
    units = [(b, h) for b in range(nb) for h in range(heads)]

    def head(ref, b, h):
        if interleaved:
            return ref[b, pl.ds(h, mem, stride=heads), :].astype(BF16)
        return ref[b, :, h * hd:(h + 1) * hd].astype(BF16)

    q = q_ref[...].astype(BF16)
    s = {(b, h): lax.dot_general(q[b * tq:(b + 1) * tq, h * hd:(h + 1) * hd], head(k_ref, b, h),
                                 (((1,), (1,)), ((), ())), preferred_element_type=F32) * scale for (b, h) in units}
    p = {u: jnp.exp(s[u] - jnp.max(s[u], axis=-1, keepdims=True)) for u in units}
    o = {(b, h): jnp.dot(p[(b, h)].astype(BF16), head(v_ref, b, h), preferred_element_type=F32)
         / jnp.sum(p[(b, h)], axis=-1, keepdims=True) for (b, h) in units}
    o_ref[...] = jnp.concatenate([jnp.concatenate([o[(b, h)] for h in range(heads)], axis=1) for b in range(nb)],
                                 axis=0).astype(BF16)


def _attn(q, row0, batch, seq, nb, tq, k, v, heads, hd, interleaved):
    nt = seq // tq
    rows = nb * tq
    rb0 = row0 // rows
    n = heads * hd
    return pl.pallas_call(
        functools.partial(_attn_kernel, nb=nb, tq=tq, heads=heads, hd=hd, interleaved=interleaved),
        grid=(batch // nb, nt),
        in_specs=[pl.BlockSpec((rows, n), lambda b, t: (rb0 + b * nt + t, 0)),
                  pl.BlockSpec((nb,) + k.shape[1:], lambda b, t: (b, 0, 0)),
                  pl.BlockSpec((nb,) + v.shape[1:], lambda b, t: (b, 0, 0))],
        out_specs=pl.BlockSpec((rows, n), lambda b, t: (b * nt + t, 0)),
        out_shape=jax.ShapeDtypeStruct((batch * seq, n), BF16),
        compiler_params=_params("parallel", "parallel"),
        name="attn",
    )(q, k, v)


def _xo_router_kernel(oa_ref, ob_ref, wo_ref, x1_ref, g_ref, rw_ref, rb_ref, x2_ref, h_ref, gate_ref, idx_ref,
                      *, na):
    o = jnp.where(pl.program_id(0) < na, oa_ref[...], ob_ref[...]).astype(BF16)
    x2 = x1_ref[...] + jnp.dot(o, wo_ref[...], preferred_element_type=F32)
    x2_ref[...] = x2
    h = _rms(x2, g_ref[...])
    h_ref[...] = h
    logits = _dot3(h, rw_ref[...]) + rb_ref[...]
    ne = logits.shape[1]
    lane = lax.broadcasted_iota(jnp.int32, logits.shape, 1)
    vals, ids = [], []
    for _ in range(TOP_K):
        m = jnp.max(logits, axis=-1, keepdims=True)
        i = jnp.min(jnp.where(logits == m, lane, ne), axis=-1, keepdims=True)
        vals.append(m)
        ids.append(i)
        logits = jnp.where(lane == i, -jnp.inf, logits)
    e = jnp.exp(jnp.concatenate(vals, axis=1) - vals[0])
    gate_ref[...] = e / jnp.sum(e, axis=-1, keepdims=True)
    idx_ref[...] = jnp.concatenate(ids, axis=1)


def _xo_router(oa, ob, wo, x1, g, rw, rb, tm):
    m, d = x1.shape
    row = lambda w: pl.BlockSpec((tm, w), lambda i: (i, 0))
    full = lambda a: pl.BlockSpec(a.shape, lambda i: (0, 0))
    return pl.pallas_call(
        functools.partial(_xo_router_kernel, na=oa.shape[0] // tm),
        grid=(m // tm,),
        in_specs=[*_split_rows(oa, ob, tm), full(wo), row(d), full(g), full(rw), full(rb)],
        out_specs=[row(d), row(d), row(TOP_K), row(TOP_K)],
        out_shape=[jax.ShapeDtypeStruct((m, d), F32), jax.ShapeDtypeStruct((m, d), F32),
                   jax.ShapeDtypeStruct((m, TOP_K), F32), jax.ShapeDtypeStruct((m, TOP_K), jnp.int32)],
        compiler_params=_params("parallel"),
        name="xo_router",
    )(oa, ob, wo, x1, g, rw, rb)


def _moe_kernel(e_ref, r0_ref, n_ref, tok_ref, dst_ref, h_hbm, wg_ref, wu_ref, wd_ref, bg_ref, bu_ref, bd_ref,
                slots_hbm, xbuf, acc, gsem, ssem, *, sub, big, nf):
    s = pl.program_id(0)
    f = pl.program_id(1)
    n = n_ref[s]
    r0 = r0_ref[s]
    d = acc.shape[2]
    s_next = jnp.minimum(s + 1, pl.num_programs(0) - 1)
    n_next = jnp.where(s + 1 < pl.num_programs(0), n_ref[s_next], 0)
    r0_next = r0_ref[s_next]

    @pl.when(jnp.logical_and(s == 0, f == 0))
    def _():
        xbuf[...] = jnp.zeros_like(xbuf)

    def row_in(base, c, k):
        tok = tok_ref[base + c * SUBLANES + k]
        return pltpu.make_async_copy(h_hbm.at[pl.ds(tok, 1), :], xbuf.at[c, pl.ds(k, 1), :], gsem)

    def row_out(c, k):
        dst = dst_ref[r0 + c * SUBLANES + k]
        return pltpu.make_async_copy(acc.at[c, pl.ds(k, 1), :], slots_hbm.at[pl.ds(dst, 1), :], ssem)

    def for_rows(count, fn):
        nfull = count // SUBLANES

        def group(c, carry):
            for k in range(SUBLANES):
                fn(c, k)
            return carry

        lax.fori_loop(0, nfull, group, 0)
        lax.fori_loop(0, count - nfull * SUBLANES, lambda k, carry: (fn(nfull, k), carry)[1], 0)

    @pl.when(n > 0)
    def _active():
        @pl.when(f == 0)
        def _gather():
            @pl.when(s == 0)
            def _first():
                for_rows(n, lambda c, k: row_in(r0, c, k).start())

            for_rows(n, lambda c, k: row_in(r0, c, k).wait())

        bg = bg_ref[0]
        bu = bu_ref[0]
        bd = bd_ref[0]

        nsub = (n + sub - 1) // sub
        sub8 = sub // SUBLANES
        big8 = big // SUBLANES

        @pl.when(f == 0)
        def _init():
            def fill(j, carry):
                acc[pl.ds(pl.multiple_of(j * sub8, sub8), sub8)] = jnp.broadcast_to(bd, (sub8, SUBLANES, d))
                return carry

            lax.fori_loop(0, nsub, fill, 0)

        def block(off8, m):
            m8 = m // SUBLANES
            x = xbuf[pl.ds(off8, m8)].reshape(m, d).astype(BF16)
            gate = jnp.dot(x, wg_ref[0].astype(BF16), preferred_element_type=F32) + bg
            up = jnp.dot(x, wu_ref[0].astype(BF16), preferred_element_type=F32) + bu
            gate = jnp.minimum(gate, SWIGLU_LIMIT)
            up = jnp.clip(up, -SWIGLU_LIMIT, SWIGLU_LIMIT)
            act = gate * _sigmoid(SWIGLU_ALPHA * gate) * (up + 1.0)
            part = jnp.dot(act.astype(BF16), wd_ref[0].astype(BF16), preferred_element_type=F32)
            acc[pl.ds(off8, m8)] += part.reshape(m8, SUBLANES, d)

        per_big = big // sub
        nbig = nsub // per_big
        tail = nsub - nbig * per_big

        def big_block(j, carry):
            block(pl.multiple_of(j * big8, big8), big)
            return carry

        lax.fori_loop(0, nbig, big_block, 0)
        for t in range(1, per_big):
            @pl.when(tail == t)
            def _tail(t=t):
                block(pl.multiple_of(nbig * big8, sub8), t * sub)

        @pl.when(f == nf - 1)
        def _scatter():
            for_rows(n, lambda c, k: row_out(c, k).start())
            for_rows(n_next, lambda c, k: row_in(r0_next, c, k).start())
            for_rows(n, lambda c, k: row_out(c, k).wait())


def _moe(h, sb_e, sb_r0, sb_n, tok, dst, wg, wu, wd, bg, bu, bd, cap, sub, big, tf):
    n_tok, d = h.shape
    ne, _, dff = wg.shape
    nf = dff // tf
    g = sb_e.shape[0]
    fi = lambda s, f, n_ref: jnp.where(n_ref[s] > 0, f, nf - 1)
    grid_spec = pltpu.PrefetchScalarGridSpec(
        num_scalar_prefetch=5,
        grid=(g, nf),
        in_specs=[pl.BlockSpec(memory_space=pl.ANY),
                  pl.BlockSpec((1, d, tf), lambda s, f, e, r, n, t, o: (e[s], 0, fi(s, f, n))),
                  pl.BlockSpec((1, d, tf), lambda s, f, e, r, n, t, o: (e[s], 0, fi(s, f, n))),
                  pl.BlockSpec((1, tf, d), lambda s, f, e, r, n, t, o: (e[s], fi(s, f, n), 0)),
                  pl.BlockSpec((1, 1, tf), lambda s, f, e, r, n, t, o: (e[s], 0, fi(s, f, n))),
                  pl.BlockSpec((1, 1, tf), lambda s, f, e, r, n, t, o: (e[s], 0, fi(s, f, n))),
                  pl.BlockSpec((1, 1, d), lambda s, f, e, r, n, t, o: (e[s], 0, 0))],
        out_specs=pl.BlockSpec(memory_space=pl.ANY),
        scratch_shapes=[pltpu.VMEM((cap // SUBLANES, SUBLANES, d), F32),
                        pltpu.VMEM((cap // SUBLANES, SUBLANES, d), F32),
                        pltpu.SemaphoreType.DMA(()), pltpu.SemaphoreType.DMA(())],
    )
    return pl.pallas_call(
        functools.partial(_moe_kernel, sub=sub, big=big, nf=nf),
        grid_spec=grid_spec,
        out_shape=jax.ShapeDtypeStruct((n_tok * TOP_K, d), F32),
        compiler_params=_params("arbitrary", "arbitrary"),
        name="moe",
    )(sb_e, sb_r0, sb_n, tok, dst, h, wg, wu, wd, bg.reshape(ne, 1, dff), bu.reshape(ne, 1, dff),
      bd.reshape(ne, 1, d))


def _route(top_idx, ne, cap, n_sb_max):
    n_tok = top_idx.shape[0]
    e_flat = top_idx.reshape(-1)
    codes = jnp.argsort(e_flat, stable=True).astype(jnp.int32)
    counts = jnp.zeros((ne,), jnp.int32).at[e_flat].add(1)
    start = jnp.cumsum(counts) - counts
    n_sb = (counts + cap - 1) // cap
    sb_end = jnp.cumsum(n_sb)
    sb_start = sb_end - n_sb
    total = sb_end[-1]
    s = jnp.arange(n_sb_max, dtype=jnp.int32)
    e_s = jnp.minimum(jnp.searchsorted(sb_end, s, side='right'), ne - 1).astype(jnp.int32)
    e_last = e_s[jnp.maximum(total - 1, 0)]
    active = s < total
    e_s = jnp.where(active, e_s, e_last)
    within = s - sb_start[e_s]
    r0 = jnp.where(active, start[e_s] + within * cap, 0).astype(jnp.int32)
    n = jnp.where(active, jnp.clip(counts[e_s] - within * cap, 0, cap), 0).astype(jnp.int32)
    tok = lax.shift_right_logical(codes, TOP_K_SHIFT)
    dst = (codes & (TOP_K - 1)) * n_tok + tok
    return tok, dst, e_s, r0, n


def _final_kernel(x2_ref, s0_ref, s1_ref, s2_ref, s3_ref, gates_ref, g_ref, y_ref):
    x = x2_ref[...]
    gt = gates_ref[...]
    for k, s_ref in enumerate((s0_ref, s1_ref, s2_ref, s3_ref)):
        x = x + s_ref[...] * gt[:, k:k + 1]
    y_ref[...] = _rms(x, g_ref[...])


def _final(x2, slots, gates, g, row0, rows, tm):
    n_tok, d = x2.shape
    rb0 = row0 // tm
    nblk = n_tok // tm
    slot = lambda k: pl.BlockSpec((tm, d), lambda i: (k * nblk + rb0 + i, 0))
    return pl.pallas_call(
        _final_kernel,
        grid=(rows // tm,),
        in_specs=[pl.BlockSpec((tm, d), lambda i: (rb0 + i, 0)),
                  slot(0), slot(1), slot(2), slot(3),
                  pl.BlockSpec((tm, TOP_K), lambda i: (rb0 + i, 0)),
                  pl.BlockSpec((1, d), lambda i: (0, 0))],
        out_specs=pl.BlockSpec((tm, d), lambda i: (i, 0)),
        out_shape=jax.ShapeDtypeStruct((rows, d), F32),
        compiler_params=_params("parallel"),
        name="final",
    )(x2, slots, slots, slots, slots, gates, g)


def _pad_tail(buf):
    return jnp.pad(buf, ((0, 0), (SUBLANES - (CONV_W - 1), 0), (0, 0)))


def kernel(x_prompt, x_sample, mem_prompt, state_lru_conv, state_lru_h, state_gdn_conv, state_gdn_S, cache_mem_k, cache_mem_v, norm_mix_g, w_in, lru_conv_w, lru_conv_b, lru_wx, lru_bx, lru_wa, lru_ba, lru_a_param, gdn_conv_w, gdn_A_log, gdn_dt_bias, gdn_norm_g, w_branch_a, w_branch_b, w_out, norm_xa_g, norm_mem_g, xa_wq, xa_wk, xa_wv, xa_wo, norm_moe_g, router_w, router_b, moe_w_gate, moe_b_gate, moe_w_up, moe_b_up, moe_w_down, moe_b_down, norm_final_g):
    depth = w_in.shape[0]
    assert depth == 1
    bp, tp, d = x_prompt.shape
    bs, ts, _ = x_sample.shape
    np_, ns = bp * tp, bs * ts
    n_tok = np_ + ns
    lw = lru_conv_w.shape[2]
    heads, dk, dv = state_gdn_S.shape[2:]
    kdim = heads * dk
    mem_len = mem_prompt.shape[1]
    xa_heads, xa_hd = cache_mem_k.shape[3:]
    xa_dim = xa_heads * xa_hd
    ne = router_w.shape[2]
    past_len = 16384
    gdn_chunk = 64

    row = lambda v: v.reshape(1, -1)
    xp = x_prompt.reshape(np_, d)
    xs = x_sample.reshape(ns, d)

    c_main = 2 * lw + 3 * kdim + heads * dv
    w_in_t = jnp.swapaxes(w_in[0], 0, 1)
    w_ba_t = w_in_t[c_main:c_main + 2 * heads].astype(BF16)
    w_gates = jnp.concatenate([lru_wx[0], lru_wa[0]], axis=-1).astype(BF16)

    rows_gcd = math.gcd(np_, ns)
    tm_s = _tile(rows_gcd, 256)
    tm_m = _tile(rows_gcd, 512)
    tm_l = _tile(rows_gcd, 1024)
    h1, ba = _norm_ba(xp, xs, row(norm_mix_g[0]), w_ba_t, tm_m)
    proj = _matmul(h1, w_in_t, c_main, tm_l, _tile(c_main, 1024), F32, "in_proj", w_is_t=True)

    lru_args = (lru_conv_w[0], row(lru_conv_b[0]), w_gates, row(lru_bx[0]), row(lru_ba[0]), row(lru_a_param[0]))
    ya_p, lru_h_p = _lru_seq(proj, 0, bp, tp, _tile(tp, 256), jnp.zeros((bp, SUBLANES, lw), F32),
                             jnp.zeros((bp, 1, lw), F32), *lru_args, pos0=0)
    if ts == SUBLANES:
        ya_s, lru_h_s = _lru_short(proj, np_, bs, math.gcd(bs, 32), _pad_tail(state_lru_conv[0]),
                                   state_lru_h[0][:, None, :], *lru_args)
    else:
        ya_s, lru_h_s = _lru_seq(proj, np_, bs, ts, ts, _pad_tail(state_lru_conv[0]), state_lru_h[0][:, None, :],
                                 *lru_args, pos0=past_len)

    gdn_args = (gdn_conv_w[0], row(gdn_A_log[0]), row(gdn_dt_bias[0]), row(gdn_norm_g[0]), heads, dk, dv)
    cp = min(gdn_chunk, tp)
    assert tp % cp == 0 and ts <= gdn_chunk
    gp = 4 if (tp // cp) % 4 == 0 else 1
    yb_p, gdn_s_p = _gdn(proj, 2 * lw, 0, bp, tp, cp, gp, cp, True, ba,
                         jnp.zeros((bp, SUBLANES, 3 * kdim), F32), jnp.zeros((bp, heads, dk, dv), F32), *gdn_args)
    gr_s = ts * math.gcd(bs, max(gdn_chunk // ts, 1))
    gs = 2 if (ns // gr_s) % 2 == 0 else 1
    yb_s, gdn_s_s = _gdn(proj, 2 * lw, np_, bs, ts, ts, gs, gr_s, False, ba,
                         _pad_tail(state_gdn_conv[0]), state_gdn_S[0], *gdn_args)

    g_row0 = c_main + 2 * heads
    w_gate_src = w_in_t
    if g_row0 % SUBLANES:
        w_gate_src, g_row0 = w_in_t[g_row0:], 0
    u = _merge(h1, ya_p, ya_s, yb_p, yb_s, w_gate_src, g_row0, w_branch_a[0].astype(BF16),
               w_branch_b[0].astype(BF16), tm_l, _tile(d, 256))
    x1, h2 = _out_norm(u, w_out[0].astype(BF16), xp, xs, row(norm_xa_g[0]), tm_m)

    k_p, v_p = _memkv(mem_prompt.reshape(bp * mem_len, d), row(norm_mem_g[0]), xa_wk[0].astype(BF16),
                      xa_wv[0].astype(BF16), _tile(bp * mem_len, 256))
    q = _matmul(h2, xa_wq[0], xa_dim, tm_l, xa_dim, BF16, "xa_q")
    o_p = _attn(q, 0, bp, tp, 1, _tile(tp, 512), k_p.reshape(bp, mem_len, xa_dim), v_p.reshape(bp, mem_len, xa_dim),
                xa_heads, xa_hd, False)
    nb_s = math.gcd(bs, 4)
    o_s = _attn(q, np_, bs, ts, nb_s, ts, cache_mem_k[0].reshape(bs, mem_len * xa_heads, xa_hd),
                cache_mem_v[0].reshape(bs, mem_len * xa_heads, xa_hd), xa_heads, xa_hd, True)
    x2, h3, gates, top_idx = _xo_router(o_p, o_s, xa_wo[0].astype(BF16), x1, row(norm_moe_g[0]), router_w[0],
                                row(router_b[0]), tm_m)

    cap, sub, big, tf = 1536, 128, 512, _tile(moe_w_gate.shape[3], 256)
    n_sb_max = (n_tok * TOP_K + ne * (cap - 1)) // cap
    tok, dst, sb_e, sb_r0, sb_n = _route(top_idx, ne, cap, n_sb_max)
    slots = _moe(h3, sb_e, sb_r0, sb_n, tok, dst, moe_w_gate[0], moe_w_up[0], moe_w_down[0],
                 moe_b_gate[0], moe_b_up[0], moe_b_down[0], cap, sub, big, tf)
    y_p =_final(x2, slots, gates, row(norm_final_g), 0, np_, tm_s)
    y_s = _final(x2, slots, gates, row(norm_final_g), np_, ns, tm_s)

    keep = CONV_W - 1
    pp = jnp.stack([lax.slice(proj, ((b + 1) * tp - keep, 0), ((b + 1) * tp, c_main)) for b in range(bp)])
    ps = proj[np_:].reshape(bs, ts, c_main)
    return (y_p.reshape(bp, tp, d), y_s.reshape(bs, ts, d),
            pp[:, :, :lw][None], lru_h_p.reshape(1, bp, lw),
            pp[:, :, 2 * lw:2 * lw + 3 * kdim][None], gdn_s_p[None],
            k_p.reshape(1, bp, mem_len, xa_heads, xa_hd), v_p.reshape(1, bp, mem_len, xa_heads, xa_hd),
            ps[:, ts - keep:, :lw][None], lru_h_s.reshape(1, bs, lw),
            ps[:, ts - keep:, 2 * lw:2 * lw + 3 * kdim][None], gdn_s_s[None])
```

```python
import functools
import math

import jax
import jax.numpy as jnp
from jax import lax
from jax.experimental import pallas as pl
from jax.experimental.pallas import tpu as pltpu

F32 = jnp.float32
BF16 = jnp.bfloat16
NORM_EPS = 1e-6
LRU_C = 8.0
CONV_W = 4
SWIGLU_ALPHA = 1.702
SWIGLU_LIMIT = 7.0
TOP_K = 4
TOP_K_SHIFT = 2
HIGH_HALF_MASK = -65536
VMEM_LIMIT_BYTES = 56 * 1024 * 1024
SUBLANES = 8
LANES = 128


def _params(*sem):
    return pltpu.CompilerParams(dimension_semantics=sem, vmem_limit_bytes=VMEM_LIMIT_BYTES)


def _bdot(a, b):
    return jnp.dot(a.astype(BF16), b.astype(BF16), preferred_element_type=F32)


def _bdot_nt(a, b):
    return lax.dot_general(a.astype(BF16), b.astype(BF16), (((1,), (1,)), ((), ())), preferred_element_type=F32)


def _dot_nt(a, b):
    return lax.dot_general(a, b, (((1,), (1,)), ((), ())), preferred_element_type=F32)


def _bdot_tn(a, b):
    return lax.dot_general(a.astype(BF16), b.astype(BF16), (((0,), (0,)), ((), ())), preferred_element_type=F32)


def _split2(x):
    hi = x.astype(BF16)
    lo = (x - hi.astype(F32)).astype(BF16)
    return hi, lo


def _split3(x):
    p1 = x.astype(BF16)
    r1 = x - p1.astype(F32)
    p2 = r1.astype(BF16)
    p3 = (r1 - p2.astype(F32)).astype(BF16)
    return p1, p2, p3


def _dot3(a, b):
    ah, al = _split2(a)
    bh, bl = _split2(b)
    d = functools.partial(jnp.dot, preferred_element_type=F32)
    return d(ah, bh) + (d(ah, bl) + d(al, bh))


def _dot_exact_lhs(m_bf16, x):
    d = functools.partial(jnp.dot, preferred_element_type=F32)
    p1, p2, p3 = _split3(x)
    return d(m_bf16, p1) + (d(m_bf16, p2) + d(m_bf16, p3))


def _dot_exact_rhs(x, m_bf16):
    d = functools.partial(jnp.dot, preferred_element_type=F32)
    p1, p2, p3 = _split3(x)
    return d(p1, m_bf16) + (d(p2, m_bf16) + d(p3, m_bf16))


def _sigmoid(x):
    return jax.nn.sigmoid(x)


def _tile(n, pref):
    if n <= pref:
        return n
    t = pref - pref % SUBLANES
    while n % t:
        t -= SUBLANES
    return t


def _softplus(x):
    return jnp.maximum(x, 0.0) + jnp.log1p(jnp.exp(-jnp.abs(x)))


def _gelu_tanh(x):
    return 0.5 * x * (1.0 + jnp.tanh(0.7978845608028654 * (x + 0.044715 * (x * x * x))))


def _rms(x, g):
    r = lax.rsqrt(jnp.mean(x * x, axis=-1, keepdims=True) + NORM_EPS)
    return x * r * g


def _split_rows(a, b, tm):
    na = a.shape[0] // tm
    w = a.shape[1]
    return (pl.BlockSpec((tm, w), lambda i, *_: (jnp.minimum(i, na - 1), 0)),
            pl.BlockSpec((tm, w), lambda i, *_: (jnp.maximum(i - na, 0), 0)))


def _norm_ba_kernel(xa_ref, xb_ref, g_ref, wba_ref, h_ref, ba_ref, *, na):
    x = jnp.where(pl.program_id(0) < na, xa_ref[...], xb_ref[...])
    h = _rms(x, g_ref[...]).astype(BF16)
    h_ref[...] = h
    ba_ref[...] = _dot_nt(h, wba_ref[...])


def _norm_ba(xa, xb, g, w_ba_t, tm):
    d = xa.shape[1]
    m = xa.shape[0] + xb.shape[0]
    nb = w_ba_t.shape[0]
    return pl.pallas_call(
        functools.partial(_norm_ba_kernel, na=xa.shape[0] // tm),
        grid=(m // tm,),
        in_specs=[*_split_rows(xa, xb, tm),
                  pl.BlockSpec((1, d), lambda i: (0, 0)),
                  pl.BlockSpec((nb, d), lambda i: (0, 0))],
        out_specs=[pl.BlockSpec((tm, d), lambda i: (i, 0)),
                   pl.BlockSpec((tm, nb), lambda i: (i, 0))],
        out_shape=[jax.ShapeDtypeStruct((m, d), BF16), jax.ShapeDtypeStruct((m, nb), F32)],
        compiler_params=_params("parallel"),
        name="norm_ba",
    )(xa, xb, g, w_ba_t)


def _mm_kernel(a_ref, w_ref, o_ref, *, w_is_t):
    w = w_ref[...].astype(BF16)
    acc = _dot_nt(a_ref[...], w) if w_is_t else jnp.dot(a_ref[...], w, preferred_element_type=F32)
    o_ref[...] = acc.astype(o_ref.dtype)


def _matmul(a, w, n_cols, tm, tn, out_dtype, name, w_is_t=False):
    m, k = a.shape
    w_spec = pl.BlockSpec((tn, k), lambda j, i: (j, 0)) if w_is_t else pl.BlockSpec((k, tn), lambda j, i: (0, j))
    return pl.pallas_call(
        functools.partial(_mm_kernel, w_is_t=w_is_t),
        grid=(n_cols // tn, m // tm),
        in_specs=[pl.BlockSpec((tm, k), lambda j, i: (i, 0)), w_spec],
        out_specs=pl.BlockSpec((tm, tn), lambda j, i: (i, j)),
        out_shape=jax.ShapeDtypeStruct((m, n_cols), out_dtype),
        compiler_params=_params("parallel", "parallel"),
        name=name,
    )(a, w)


def _conv4_into(xe_ref, rows, cw):
    y = xe_ref[SUBLANES:SUBLANES + rows, :] * cw[CONV_W - 1:CONV_W]
    for s in range(1, CONV_W):
        y = y + xe_ref[SUBLANES - s:SUBLANES - s + rows, :] * cw[CONV_W - 1 - s:CONV_W - s]
    return y


def _scan8(a3, b3):
    row = lax.broadcasted_iota(jnp.int32, a3.shape, 1)
    for s in (1, 2, 4):
        a_sh = pltpu.roll(a3, s, axis=1)
        b_sh = pltpu.roll(b3, s, axis=1)
        m = row >= s
        b3 = jnp.where(m, a3 * b_sh + b3, b3)
        a3 = jnp.where(m, a3 * a_sh, a3)
    return a3, b3


def _lru_gates(xc, wg_ref, bx, bga, ap, first_row_is_pos0):
    nblk = wg_ref.shape[0]
    bw = wg_ref.shape[1]
    xcb = xc.astype(BF16)
    gx, ga = [], []
    for n in range(nblk):
        r = jnp.dot(xcb[:, n * bw:(n + 1) * bw], wg_ref[n], preferred_element_type=F32)
        gx.append(r[:, :bw])
        ga.append(r[:, bw:])
    gate_x = _sigmoid(jnp.concatenate(gx, axis=1) + bx)
    gate_a = _sigmoid(jnp.concatenate(ga, axis=1) + bga)
    log_a = (-LRU_C) * gate_a * _softplus(ap)
    a = jnp.exp(log_a)
    mult = jnp.sqrt(-jnp.tanh(log_a) * (a * a + 1.0))
    if first_row_is_pos0 is not None:
        mult = jnp.where(first_row_is_pos0, 1.0, mult)
    return a, xc * gate_x * mult


def _lru_seq_kernel(xa_ref, ya_ref, tail0_ref, h0_ref, cw_ref, cb_ref, wg_ref, bx_ref, bga_ref, ap_ref,
                    y_ref, hl_ref, xe_ref, a_ref, b_ref, h_ref, *, tt, pos0):
    t = pl.program_id(1)
    w = xa_ref.shape[1]

    @pl.when(t == 0)
    def _():
        xe_ref[0:SUBLANES, :] = tail0_ref[0]
        h_ref[...] = h0_ref[0]

    xe_ref[SUBLANES:SUBLANES + tt, :] = xa_ref[...]
    xc = _conv4_into(xe_ref, tt, cw_ref[...]) + cb_ref[...]
    xe_ref[0:SUBLANES, :] = xe_ref[tt:tt + SUBLANES, :]
    first = None
    if pos0 == 0:
        first = (lax.broadcasted_iota(jnp.int32, (tt, 1), 0) + t * tt) == 0
    a, b = _lru_gates(xc, wg_ref, bx_ref[...], bga_ref[...], ap_ref[...], first)
    a3, b3 = _scan8(a.reshape(tt // SUBLANES, SUBLANES, w), b.reshape(tt // SUBLANES, SUBLANES, w))
    a_ref[...] = a3
    b_ref[...] = b3

    def body(g, h):
        hg = a_ref[g] * h + b_ref[g]
        b_ref[g] = hg
        return hg[SUBLANES - 1:SUBLANES, :]

    h = lax.fori_loop(0, tt // SUBLANES, body, h_ref[...])
    h_ref[...] = h
    hl_ref[0] = h
    hs = b_ref[...].reshape(tt, w)
    y_ref[...] = (hs * _gelu_tanh(ya_ref[...])).astype(BF16)


def _lru_seq(proj, row0, batch, seq, tt, tail0, h0, cw, cb, wg, bx, bga, ap, pos0):
    w = cw.shape[1]
    nt = seq // tt
    rb0 = row0 // tt
    vec = lambda: pl.BlockSpec((1, w), lambda b, t: (0, 0))
    return pl.pallas_call(
        functools.partial(_lru_seq_kernel, tt=tt, pos0=pos0),
        grid=(batch, nt),
        in_specs=[pl.BlockSpec((tt, w), lambda b, t: (rb0 + b * nt + t, 0)),
                  pl.BlockSpec((tt, w), lambda b, t: (rb0 + b * nt + t, 1)),
                  pl.BlockSpec((1, SUBLANES, w), lambda b, t: (b, 0, 0)),
                  pl.BlockSpec((1, 1, w), lambda b, t: (b, 0, 0)),
                  pl.BlockSpec((CONV_W, w), lambda b, t: (0, 0)),
                  vec(),
                  pl.BlockSpec(wg.shape, lambda b, t: (0, 0, 0)),
                  vec(), vec(), vec()],
        out_specs=[pl.BlockSpec((tt, w), lambda b, t: (b * nt + t, 0)),
                   pl.BlockSpec((1, 1, w), lambda b, t: (b, 0, 0))],
        out_shape=[jax.ShapeDtypeStruct((batch * seq, w), BF16),
                   jax.ShapeDtypeStruct((batch, 1, w), F32)],
        scratch_shapes=[pltpu.VMEM((SUBLANES + tt, w), F32),
                        pltpu.VMEM((tt // SUBLANES, SUBLANES, w), F32),
                        pltpu.VMEM((tt // SUBLANES, SUBLANES, w), F32),
                        pltpu.VMEM((1, w), F32)],
        compiler_params=_params("parallel", "arbitrary"),
        name="lru_seq",
    )(proj, proj, tail0, h0, cw, cb, wg, bx, bga, ap)


def _lru_short_kernel(xa_ref, ya_ref, tail0_ref, h0_ref, cw_ref, cb_ref, wg_ref, bx_ref, bga_ref, ap_ref,
                      y_ref, hl_ref, xe_ref):
    nb = xe_ref.shape[0]
    w = xa_ref.shape[1]
    xe_ref[:, 0:SUBLANES, :] = tail0_ref[...]
    xe_ref[:, SUBLANES:, :] = xa_ref[...].reshape(nb, SUBLANES, w)
    cw = cw_ref[...]
    xc = xe_ref[:, SUBLANES:, :] * cw[CONV_W - 1:CONV_W]
    for s in range(1, CONV_W):
        xc = xc + xe_ref[:, SUBLANES - s:2 * SUBLANES - s, :] * cw[CONV_W - 1 - s:CONV_W - s]
    xc = xc.reshape(nb * SUBLANES, w) + cb_ref[...]
    a, b = _lru_gates(xc, wg_ref, bx_ref[...], bga_ref[...], ap_ref[...], None)
    a3, b3 = _scan8(a.reshape(nb, SUBLANES, w), b.reshape(nb, SUBLANES, w))
    hs = a3 * h0_ref[...] + b3
    hl_ref[...] = hs[:, SUBLANES - 1:SUBLANES, :]
    y_ref[...] = (hs.reshape(nb * SUBLANES, w) * _gelu_tanh(ya_ref[...])).astype(BF16)


def _lru_short(proj, row0, batch, nb, tail0, h0, cw, cb, wg, bx, bga, ap):
    w = cw.shape[1]
    rows = nb * SUBLANES
    rb0 = row0 // rows
    vec = lambda: pl.BlockSpec((1, w), lambda b: (0, 0))
    return pl.pallas_call(
        _lru_short_kernel,
        grid=(batch // nb,),
        in_specs=[pl.BlockSpec((rows, w), lambda b: (rb0 + b, 0)),
                  pl.BlockSpec((rows, w), lambda b: (rb0 + b, 1)),
                  pl.BlockSpec((nb, SUBLANES, w), lambda b: (b, 0, 0)),
                  pl.BlockSpec((nb, 1, w), lambda b: (b, 0, 0)),
                  pl.BlockSpec((CONV_W, w), lambda b: (0, 0)),
                  vec(),
                  pl.BlockSpec(wg.shape, lambda b: (0, 0, 0)),
                  vec(), vec(), vec()],
        out_specs=[pl.BlockSpec((rows, w), lambda b: (b, 0)),
                   pl.BlockSpec((nb, 1, w), lambda b: (b, 0, 0))],
        out_shape=[jax.ShapeDtypeStruct((batch * SUBLANES, w), BF16),
                   jax.ShapeDtypeStruct((batch, 1, w), F32)],
        scratch_shapes=[pltpu.VMEM((nb, 2 * SUBLANES, w), F32)],
        compiler_params=_params("parallel"),
        name="lru_short",
    )(proj, proj, tail0, h0, cw, cb, wg, bx, bga, ap)


def _gdn_kernel(q_ref, k_ref, v_ref, z_ref, ba_ref, bat_ref, tail0_ref, s0_ref, cw_ref, alog_ref, dtb_ref,
                alogt_ref, dtbt_ref, ng_ref, y_ref, sout_ref, xe_ref, s_ref, *, groups, gr, c, carry, heads, dk, dv):
    ci = pl.program_id(1)
    kdim = heads * dk
    rows = groups * gr
    nseg = gr // c
    nconv, crow = xe_ref.shape[0], xe_ref.shape[1] - SUBLANES
    hs = range(heads)
    units = [(gi, h) for gi in range(groups) for h in hs]

    @pl.when(ci == 0)
    def _():
        xe_ref[:, 0:SUBLANES, :] = tail0_ref[...]
        if carry:
            s_ref[...] = s0_ref[0]

    xe_ref[:, SUBLANES:, 0:kdim] = q_ref[...].reshape(nconv, crow, kdim)
    xe_ref[:, SUBLANES:, kdim:2 * kdim] = k_ref[...].reshape(nconv, crow, kdim)
    xe_ref[:, SUBLANES:, 2 * kdim:] = v_ref[...].reshape(nconv, crow, kdim)
    cw = cw_ref[...]
    pre = xe_ref[:, SUBLANES:, :] * cw[CONV_W - 1:CONV_W]
    for s in range(1, CONV_W):
        pre = pre + xe_ref[:, SUBLANES - s:SUBLANES - s + crow, :] * cw[CONV_W - 1 - s:CONV_W - s]
    xe_ref[:, 0:SUBLANES, :] = xe_ref[:, crow:crow + SUBLANES, :]
    pre = pre.reshape(rows, 3 * kdim)
    qkv = pre * _sigmoid(pre)

    ba = ba_ref[...]
    beta = _sigmoid(ba[:, 0:heads])
    g = -jnp.exp(alog_ref[...]) * _softplus(ba[:, heads:2 * heads] + dtb_ref[...])
    ii = lax.broadcasted_iota(jnp.int32, (gr, gr), 0)
    jj = lax.broadcasted_iota(jnp.int32, (gr, gr), 1)
    same = (ii // c) == (jj // c)
    incl = jnp.logical_and(same, ii >= jj)
    strict = jnp.logical_and(same, ii > jj)
    eye = (ii == jj).astype(F32)
    incl_b = incl.astype(BF16)
    inclt_b = jnp.logical_and(same, ii <= jj).astype(BF16)
    same_b = same.astype(BF16)
    z = z_ref[...]
    ng = ng_ref[...]
    scale = dk ** -0.5

    gcs, gcts, egcs, ekds, gtots = [], [], [], [], []
    for gi in range(groups):
        gg = g[gi * gr:(gi + 1) * gr]
        gt = -jnp.exp(alogt_ref[...]) * _softplus(bat_ref[0, 0, gi][heads:2 * heads, :] + dtbt_ref[...])
        gc = _dot_exact_lhs(incl_b, gg)
        gtot = _dot_exact_lhs(same_b, gg)
        gcs.append(gc)
        gcts.append(_dot_exact_rhs(gt, inclt_b))
        egcs.append(jnp.exp(gc))
        ekds.append(jnp.exp(gtot - gc))
        gtots.append(gtot)

    k_, kb_, dec_, rhs_, qs_ = {}, {}, {}, {}, {}
    for (gi, h) in units:
        r0 = gi * gr
        qh = qkv[r0:r0 + gr, h * dk:(h + 1) * dk]
        kh = qkv[r0:r0 + gr, kdim + h * dk:kdim + (h + 1) * dk]
        vh = qkv[r0:r0 + gr, 2 * kdim + h * dv:2 * kdim + (h + 1) * dv]
        qh = qh * lax.rsqrt(jnp.sum(qh * qh, axis=-1, keepdims=True) + NORM_EPS)
        kh = kh * lax.rsqrt(jnp.sum(kh * kh, axis=-1, keepdims=True) + NORM_EPS)
        bh = beta[r0:r0 + gr, h:h + 1]
        diff = gcs[gi][:, h:h + 1] - gcts[gi][h:h + 1, :]
        u = (gi, h)
        dec_[u] = jnp.where(incl, jnp.exp(jnp.where(incl, diff, 0.0)), 0.0)
        k_[u] = kh
        kb_[u] = kh * bh
        qs_[u] = qh * scale
        rhs_[u] = jnp.concatenate([vh * bh, kb_[u] * egcs[gi][:, h:h + 1]], axis=1)
    npow = {u: jnp.where(strict, -(_bdot_nt(kb_[u], k_[u]) * dec_[u]), 0.0) for u in units}
    qk_ = {u: _bdot_nt(qs_[u], k_[u]) * dec_[u] for u in units}
    p = {u: eye + npow[u] for u in units}
    lvl = 1
    while 2 * lvl < c:
        npow = {u: _dot3(npow[u], npow[u]) for u in units}
        p = {u: p[u] + _dot3(p[u], npow[u]) for u in units}
        lvl *= 2
    sol = {u: _dot3(p[u], rhs_[u]) for u in units}

    o_ = {}
    if carry:
        assert nseg == 1
        cur = {h: s_ref[h] for h in hs}
        for gi in range(groups):
            wq = {}
            for h in hs:
                u = (gi, h)
                lhs = jnp.concatenate([sol[u][:, dv:], qs_[u] * egcs[gi][:, h:h + 1]], axis=0)
                wq[h] = _bdot(lhs, cur[h])
            for h in hs:
                u = (gi, h)
                vnew = sol[u][:, :dv] - wq[h][:gr]
                o_[u] = wq[h][gr:] + _bdot(qk_[u], vnew)
                kd = k_[u] * ekds[gi][:, h:h + 1]
                glast = jnp.exp(gtots[gi][0:1, h:h + 1])
                cur[h] = cur[h] * glast + _bdot_tn(kd, vnew)
        for h in hs:
            s_ref[h] = cur[h]
        sout_ref[0] = s_ref[...]
    else:
        segs = [(gi, j, h) for gi in range(groups) for j in range(nseg) for h in hs]
        wq = {}
        for (gi, j, h) in segs:
            u = (gi, h)
            sl = slice(j * c, (j + 1) * c)
            lhs = jnp.concatenate([sol[u][sl, dv:], qs_[u][sl] * egcs[gi][sl, h:h + 1]], axis=0)
            wq[(gi, j, h)] = _bdot(lhs, s0_ref[gi * nseg + j, h])
        vn = {}
        for (gi, j, h) in segs:
            u = (gi, h)
            sl = slice(j * c, (j + 1) * c)
            vn[(gi, j, h)] = sol[u][sl, :dv] - wq[(gi, j, h)][:c]
            kd = k_[u][sl] * ekds[gi][sl, h:h + 1]
            glast = jnp.exp(gtots[gi][j * c:j * c + 1, h:h + 1])
            sout_ref[gi * nseg + j, h] = s0_ref[gi * nseg + j, h] * glast + _bdot_tn(kd, vn[(gi, j, h)])
        for u in units:
            gi, h = u
            vnew = jnp.concatenate([vn[(gi, j, h)] for j in range(nseg)], axis=0)
            os_ = jnp.concatenate([wq[(gi, j, h)][c:] for j in range(nseg)], axis=0)
            o_[u] = os_ + _bdot(qk_[u], vnew)

    ys = []
    for gi in range(groups):
        row = []
        for h in hs:
            zz = z[gi * gr:(gi + 1) * gr, h * dv:(h + 1) * dv]
            row.append(_rms(o_[(gi, h)], ng) * (zz * _sigmoid(zz)))
        ys.append(jnp.concatenate(row, axis=1))
    y_ref[...] = jnp.concatenate(ys, axis=0).astype(BF16)


def _gdn(proj, col0, row0, batch, seq, c, groups, gr, carry, ba, tail0, s0, cw, alog, dtb, ng, heads, dk, dv):
    kdim = heads * dk
    rows = groups * gr
    if carry:
        nb, nsteps, sb = batch, seq // rows, 1
    else:
        sb = rows // seq
        nb, nsteps = batch // sb, 1
    crow = rows // sb
    rb0 = row0 // rows
    cb0 = col0 // kdim
    rblk = lambda b, ci: rb0 + b * nsteps + ci
    small = lambda shape: pl.BlockSpec(shape, lambda b, ci: (0,) * len(shape))
    bat = ba[row0:row0 + batch * seq].reshape(nb, nsteps, groups, gr, 2 * heads)
    bat = jnp.swapaxes(bat, 3, 4)
    return pl.pallas_call(
        functools.partial(_gdn_kernel, groups=groups, gr=gr, c=c, carry=carry, heads=heads, dk=dk, dv=dv),
        grid=(nb, nsteps),
        in_specs=[pl.BlockSpec((rows, kdim), lambda b, ci: (rblk(b, ci), cb0)),
                  pl.BlockSpec((rows, kdim), lambda b, ci: (rblk(b, ci), cb0 + 1)),
                  pl.BlockSpec((rows, kdim), lambda b, ci: (rblk(b, ci), cb0 + 2)),
                  pl.BlockSpec((rows, kdim), lambda b, ci: (rblk(b, ci), cb0 + 3)),
                  pl.BlockSpec((rows, 2 * heads), lambda b, ci: (rblk(b, ci), 0)),
                  pl.BlockSpec((1, 1, groups, 2 * heads, gr), lambda b, ci: (b, ci, 0, 0, 0)),
                  pl.BlockSpec((sb, SUBLANES, 3 * kdim), lambda b, ci: (b, 0, 0)),
                  pl.BlockSpec((sb, heads, dk, dv), lambda b, ci: (b, 0, 0, 0)),
                  small((CONV_W, 3 * kdim)),
                  small((1, heads)), small((1, heads)), small((heads, 1)), small((heads, 1)),
                  small((1, dv))],
        out_specs=[pl.BlockSpec((rows, heads * dv), lambda b, ci: (b * nsteps + ci, 0)),
                   pl.BlockSpec((sb, heads, dk, dv), lambda b, ci: (b, 0, 0, 0))],
        out_shape=[jax.ShapeDtypeStruct((batch * seq, heads * dv), BF16),
                   jax.ShapeDtypeStruct((batch, heads, dk, dv), F32)],
        scratch_shapes=[pltpu.VMEM((sb, SUBLANES + crow, 3 * kdim), F32),
                        pltpu.VMEM((heads, dk, dv), F32)],
        compiler_params=_params("parallel", "arbitrary"),
        name="gdn",
    )(proj, proj, proj, proj, ba, bat, tail0, s0, cw, alog, dtb, alog.T, dtb.T, ng)


def _merge_kernel(h_ref, ya1_ref, ya2_ref, yb1_ref, yb2_ref, wga_ref, wgb_ref, wa_ref, wb_ref, u_ref, *, na):
    d = functools.partial(jnp.dot, preferred_element_type=F32)
    first = pl.program_id(0) < na
    h = h_ref[...]
    ya = jnp.where(first, ya1_ref[...], ya2_ref[...]).astype(BF16)
    yb = jnp.where(first, yb1_ref[...], yb2_ref[...]).astype(BF16)
    u = _sigmoid(_dot_nt(h, wga_ref[...].astype(BF16))) * d(ya, wa_ref[...])
    u = u + _sigmoid(_dot_nt(h, wgb_ref[...].astype(BF16))) * d(yb, wb_ref[...])
    u_ref[...] = u.astype(BF16)


def _merge(h, ya1, ya2, yb1, yb2, w_t, g_row0, w_a, w_b, tm, tn):
    m, d = h.shape
    n = w_a.shape[1]
    nj = n // tn
    assert g_row0 % SUBLANES == 0
    gate = lambda off: pl.BlockSpec((pl.Element(tn), pl.Element(d)),
                                    lambda i, j: (pl.multiple_of(g_row0 + off + j * tn, SUBLANES), 0))
    return pl.pallas_call(
        functools.partial(_merge_kernel, na=ya1.shape[0] // tm),
        grid=(m // tm, nj),
        in_specs=[pl.BlockSpec((tm, d), lambda i, j: (i, 0)),
                  *_split_rows(ya1, ya2, tm),
                  *_split_rows(yb1, yb2, tm),
                  gate(0),
                  gate(n),
                  pl.BlockSpec((w_a.shape[0], tn), lambda i, j: (0, j)),
                  pl.BlockSpec((w_b.shape[0], tn), lambda i, j: (0, j))],
        out_specs=pl.BlockSpec((tm, tn), lambda i, j: (i, j)),
        out_shape=jax.ShapeDtypeStruct((m, n), BF16),
        compiler_params=_params("parallel", "parallel"),
        name="merge",
    )(h, ya1, ya2, yb1, yb2, w_t, w_t, w_a, w_b)


def _out_norm_kernel(u_ref, w_ref, xa_ref, xb_ref, g_ref, x1_ref, h_ref, *, na):
    x = jnp.where(pl.program_id(0) < na, xa_ref[...], xb_ref[...])
    x1 = x + jnp.dot(u_ref[...], w_ref[...], preferred_element_type=F32)
    x1_ref[...] = x1
    h_ref[...] = _rms(x1, g_ref[...]).astype(BF16)


def _out_norm(u, w, xa, xb, g, tm):
    m, k = u.shape
    d = xa.shape[1]
    return pl.pallas_call(
        functools.partial(_out_norm_kernel, na=xa.shape[0] // tm),
        grid=(m // tm,),
        in_specs=[pl.BlockSpec((tm, k), lambda i: (i, 0)),
                  pl.BlockSpec((k, d), lambda i: (0, 0)),
                  *_split_rows(xa, xb, tm),
                  pl.BlockSpec((1, d), lambda i: (0, 0))],
        out_specs=[pl.BlockSpec((tm, d), lambda i: (i, 0)),
                   pl.BlockSpec((tm, d), lambda i: (i, 0))],
        out_shape=[jax.ShapeDtypeStruct((m, d), F32), jax.ShapeDtypeStruct((m, d), BF16)],
        compiler_params=_params("parallel"),
        name="out_norm",
    )(u, w, xa, xb, g)


def _memkv_kernel(m_ref, g_ref, wk_ref, wv_ref, k_ref, v_ref):
    mn = _rms(m_ref[...], g_ref[...]).astype(BF16)
    k_ref[...] = jnp.dot(mn, wk_ref[...], preferred_element_type=F32)
    v_ref[...] = jnp.dot(mn, wv_ref[...], preferred_element_type=F32)


def _memkv(mem, g, wk, wv, tm):
    m, d = mem.shape
    n = wk.shape[1]
    return pl.pallas_call(
        _memkv_kernel,
        grid=(m // tm,),
        in_specs=[pl.BlockSpec((tm, d), lambda i: (i, 0)),
                  pl.BlockSpec((1, d), lambda i: (0, 0)),
                  pl.BlockSpec((d, n), lambda i: (0, 0)),
                  pl.BlockSpec((d, n), lambda i: (0, 0))],
        out_specs=[pl.BlockSpec((tm, n), lambda i: (i, 0)), pl.BlockSpec((tm, n), lambda i: (i, 0))],
        out_shape=[jax.ShapeDtypeStruct((m, n), F32), jax.ShapeDtypeStruct((m, n), F32)],
        compiler_params=_params("parallel"),
        name="memkv",
    )(mem, g, wk, wv)


def _attn_kernel(q_ref, k_ref, v_ref, o_ref, *, nb, tq, heads, hd, interleaved):
    mem = k_ref.shape[1] // heads if interleaved else k_ref.shape[1]
    scale = hd ** -0.5
    units = [(b, h) for b in range(nb) for h in range(heads)]

    def head(ref, b, h):
        if interleaved:
            return ref[b, pl.ds(h, mem, stride=heads), :].astype(BF16)
        return ref[b, :, h * hd:(h + 1) * hd].astype(BF16)

    q = q_ref[...].astype(BF16)
    s = {(b, h): lax.dot_general(q[b * tq:(b + 1) * tq, h * hd:(h + 1) * hd], head(k_ref, b, h),
                                 (((1,), (1,)), ((), ())), preferred_element_type=F32) * scale for (b, h) in units}
    p = {u: jnp.exp(s[u] - jnp.max(s[u], axis=-1, keepdims=True)) for u in units}
    o = {(b, h): jnp.dot(p[(b, h)].astype(BF16), head(v_ref, b, h), preferred_element_type=F32)
         / jnp.sum(p[(b, h)], axis=-1, keepdims=True) for (b, h) in units}
    o_ref[...] = jnp.concatenate([jnp.concatenate([o[(b, h)] for h in range(heads)], axis=1) for b in range(nb)],
                                 axis=0).astype(BF16)


def _attn(q, row0, batch, seq, nb, tq, k, v, heads, hd, interleaved):
    nt = seq // tq
    rows = nb * tq
    rb0 = row0 // rows
    n = heads * hd
    return pl.pallas_call(
        functools.partial(_attn_kernel, nb=nb, tq=tq, heads=heads, hd=hd, interleaved=interleaved),
        grid=(batch // nb, nt),
        in_specs=[pl.BlockSpec((rows, n), lambda b, t: (rb0 + b * nt + t, 0)),
                  pl.BlockSpec((nb,) + k.shape[1:], lambda b, t: (b, 0, 0)),
                  pl.BlockSpec((nb,) + v.shape[1:], lambda b, t: (b, 0, 0))],
        out_specs=pl.BlockSpec((rows, n), lambda b, t: (b * nt + t, 0)),
        out_shape=jax.ShapeDtypeStruct((batch * seq, n), BF16),
        compiler_params=_params("parallel", "parallel"),
        name="attn",
    )(q, k, v)


def _xo_router_kernel(oa_ref, ob_ref, wo_ref, x1_ref, g_ref, rw_ref, rb_ref, x2_ref, h_ref, gate_ref, idx_ref,
                      *, na):
    o = jnp.where(pl.program_id(0) < na, oa_ref[...], ob_ref[...]).astype(BF16)
    x2 = x1_ref[...] + jnp.dot(o, wo_ref[...], preferred_element_type=F32)
    x2_ref[...] = x2
    h = _rms(x2, g_ref[...])
    half = h.shape[1] // 2
    hb = h.astype(BF16).astype(F32)
    lo = lax.shift_right_logical(lax.bitcast_convert_type(hb[:, :half], jnp.int32), 16)
    h_ref[...] = lax.bitcast_convert_type(hb[:, half:], jnp.int32) | lo
    logits = _dot3(h, rw_ref[...]) + rb_ref[...]
    ne = logits.shape[1]
    lane = lax.broadcasted_iota(jnp.int32, logits.shape, 1)
    vals, ids = [], []
    for _ in range(TOP_K):
        m = jnp.max(logits, axis=-1, keepdims=True)
        i = jnp.min(jnp.where(logits == m, lane, ne), axis=-1, keepdims=True)
        vals.append(m)
        ids.append(i)
        logits = jnp.where(lane == i, -jnp.inf, logits)
    e = jnp.exp(jnp.concatenate(vals, axis=1) - vals[0])
    gate_ref[...] = e / jnp.sum(e, axis=-1, keepdims=True)
    idx_ref[...] = jnp.concatenate(ids, axis=1)


def _xo_router(oa, ob, wo, x1, g, rw, rb, tm):
    m, d = x1.shape
    row = lambda w: pl.BlockSpec((tm, w), lambda i: (i, 0))
    full = lambda a: pl.BlockSpec(a.shape, lambda i: (0, 0))
    return pl.pallas_call(
        functools.partial(_xo_router_kernel, na=oa.shape[0] // tm),
        grid=(m // tm,),
        in_specs=[*_split_rows(oa, ob, tm), full(wo), row(d), full(g), full(rw), full(rb)],
        out_specs=[row(d), row(d // 2), row(TOP_K), row(TOP_K)],
        out_shape=[jax.ShapeDtypeStruct((m, d), F32), jax.ShapeDtypeStruct((m, d // 2), jnp.int32),
                   jax.ShapeDtypeStruct((m, TOP_K), F32), jax.ShapeDtypeStruct((m, TOP_K), jnp.int32)],
        compiler_params=_params("parallel"),
        name="xo_router",
    )(oa, ob, wo, x1, g, rw, rb)


def _moe_kernel(e_ref, r0_ref, n_ref, tok_ref, dst_ref, h_hbm, wg_ref, wu_ref, wd_ref, bg_ref, bu_ref, bd_ref,
                slots_hbm, xbuf, acc, gsem, ssem, *, sub, big, nf):
    s = pl.program_id(0)
    f = pl.program_id(1)
    n = n_ref[s]
    r0 = r0_ref[s]
    d = acc.shape[2]
    slot = s % 2
    s_next = jnp.minimum(s + 1, pl.num_programs(0) - 1)
    n_next = jnp.where(s + 1 < pl.num_programs(0), n_ref[s_next], 0)
    r0_next = r0_ref[s_next]

    @pl.when(jnp.logical_and(s == 0, f == 0))
    def _():
        xbuf[...] = jnp.zeros_like(xbuf)

    def row_in(base, sl, c, k):
        tok = tok_ref[base + c * SUBLANES + k]
        return pltpu.make_async_copy(h_hbm.at[pl.ds(tok, 1), :], xbuf.at[sl, c, pl.ds(k, 1), :], gsem.at[sl])

    def row_out(c, k):
        dst = dst_ref[r0 + c * SUBLANES + k]
        return pltpu.make_async_copy(acc.at[c, pl.ds(k, 1), :], slots_hbm.at[pl.ds(dst, 1), :], ssem)

    def for_rows(count, fn):
        nfull = count // SUBLANES

        def group(c, carry):
            for k in range(SUBLANES):
                fn(c, k)
            return carry

        lax.fori_loop(0, nfull, group, 0)
        lax.fori_loop(0, count - nfull * SUBLANES, lambda k, carry: (fn(nfull, k), carry)[1], 0)

    @pl.when(n > 0)
    def _active():
        @pl.when(f == 0)
        def _gather():
            @pl.when(s == 0)
            def _first():
                for_rows(n, lambda c, k: row_in(r0, slot, c, k).start())

            for_rows(n, lambda c, k: row_in(r0, slot, c, k).wait())
            for_rows(n_next, lambda c, k: row_in(r0_next, 1 - slot, c, k).start())

        bg = bg_ref[0]
        bu = bu_ref[0]
        bd = bd_ref[0]

        nsub = (n + sub - 1) // sub
        sub8 = sub // SUBLANES
        big8 = big // SUBLANES

        @pl.when(f == 0)
        def _init():
            def fill(j, carry):
                acc[pl.ds(pl.multiple_of(j * sub8, sub8), sub8)] = jnp.broadcast_to(bd, (sub8, SUBLANES, d))
                return carry

            lax.fori_loop(0, nsub, fill, 0)

        def block(off8, m):
            m8 = m // SUBLANES
            w = xbuf[slot, pl.ds(off8, m8)].reshape(m, d // 2)
            x = jnp.concatenate([lax.bitcast_convert_type(w << 16, F32),
                                 lax.bitcast_convert_type(w & HIGH_HALF_MASK, F32)], axis=1).astype(BF16)
            gate = jnp.dot(x, wg_ref[0].astype(BF16), preferred_element_type=F32) + bg
            up = jnp.dot(x, wu_ref[0].astype(BF16), preferred_element_type=F32) + bu
            gate = jnp.minimum(gate, SWIGLU_LIMIT)
            up = jnp.clip(up, -SWIGLU_LIMIT, SWIGLU_LIMIT)
            act = gate * _sigmoid(SWIGLU_ALPHA * gate) * (up + 1.0)
            part = jnp.dot(act.astype(BF16), wd_ref[0].astype(BF16), preferred_element_type=F32)
            acc[pl.ds(off8, m8)] += part.reshape(m8, SUBLANES, d)

        per_big = big // sub
        nbig = nsub // per_big
        tail = nsub - nbig * per_big

        def big_block(j, carry):
            block(pl.multiple_of(j * big8, big8), big)
            return carry

        lax.fori_loop(0, nbig, big_block, 0)
        for t in range(1, per_big):
            @pl.when(tail == t)
            def _tail(t=t):
                block(pl.multiple_of(nbig * big8, sub8), t * sub)

        @pl.when(f == nf - 1)
        def _scatter():
            for_rows(n, lambda c, k: row_out(c, k).start())
            for_rows(n, lambda c, k: row_out(c, k).wait())


def _moe(h, sb_e, sb_r0, sb_n, tok, dst, wg, wu, wd, bg, bu, bd, cap, sub, big, tf):
    n_tok = h.shape[0]
    ne, d, dff = wg.shape
    assert h.shape[1] * 2 == d and h.dtype == jnp.int32
    nf = dff // tf
    g = sb_e.shape[0]
    fi = lambda s, f, n_ref: jnp.where(n_ref[s] > 0, f, nf - 1)
    grid_spec = pltpu.PrefetchScalarGridSpec(
        num_scalar_prefetch=5,
        grid=(g, nf),
        in_specs=[pl.BlockSpec(memory_space=pl.ANY),
                  pl.BlockSpec((1, d, tf), lambda s, f, e, r, n, t, o: (e[s], 0, fi(s, f, n))),
                  pl.BlockSpec((1, d, tf), lambda s, f, e, r, n, t, o: (e[s], 0, fi(s, f, n))),
                  pl.BlockSpec((1, tf, d), lambda s, f, e, r, n, t, o: (e[s], fi(s, f, n), 0)),
                  pl.BlockSpec((1, 1, tf), lambda s, f, e, r, n, t, o: (e[s], 0, fi(s, f, n))),
                  pl.BlockSpec((1, 1, tf), lambda s, f, e, r, n, t, o: (e[s], 0, fi(s, f, n))),
                  pl.BlockSpec((1, 1, d), lambda s, f, e, r, n, t, o: (e[s], 0, 0))],
        out_specs=pl.BlockSpec(memory_space=pl.ANY),
        scratch_shapes=[pltpu.VMEM((2, cap // SUBLANES, SUBLANES, d // 2), jnp.int32),
                        pltpu.VMEM((cap // SUBLANES, SUBLANES, d), F32),
                        pltpu.SemaphoreType.DMA((2,)), pltpu.SemaphoreType.DMA(())],
    )
    return pl.pallas_call(
        functools.partial(_moe_kernel, sub=sub, big=big, nf=nf),
        grid_spec=grid_spec,
        out_shape=jax.ShapeDtypeStruct((n_tok * TOP_K, d), F32),
        compiler_params=_params("arbitrary", "arbitrary"),
        name="moe",
    )(sb_e, sb_r0, sb_n, tok, dst, h, wg, wu, wd, bg.reshape(ne, 1, dff), bu.reshape(ne, 1, dff),
      bd.reshape(ne, 1, d))


def _route(top_idx, ne, cap, n_sb_max):
    n_tok = top_idx.shape[0]
    e_flat = top_idx.reshape(-1)
    codes = jnp.argsort(e_flat, stable=True).astype(jnp.int32)
    counts = jnp.zeros((ne,), jnp.int32).at[e_flat].add(1)
    start = jnp.cumsum(counts) - counts
    n_sb = (counts + cap - 1) // cap
    sb_end = jnp.cumsum(n_sb)
    sb_start = sb_end - n_sb
    total = sb_end[-1]
    s = jnp.arange(n_sb_max, dtype=jnp.int32)
    e_s = jnp.minimum(jnp.searchsorted(sb_end, s, side='right'), ne - 1).astype(jnp.int32)
    e_last = e_s[jnp.maximum(total - 1, 0)]
    active = s < total
    e_s = jnp.where(active, e_s, e_last)
    within = s - sb_start[e_s]
    r0 = jnp.where(active, start[e_s] + within * cap, 0).astype(jnp.int32)
    n = jnp.where(active, jnp.clip(counts[e_s] - within * cap, 0, cap), 0).astype(jnp.int32)
    tok = lax.shift_right_logical(codes, TOP_K_SHIFT)
    dst = (codes & (TOP_K - 1)) * n_tok + tok
    return tok, dst, e_s, r0, n


def _final_kernel(x2_ref, s0_ref, s1_ref, s2_ref, s3_ref, gates_ref, g_ref, y_ref):
    x = x2_ref[...]
    gt = gates_ref[...]
    for k, s_ref in enumerate((s0_ref, s1_ref, s2_ref, s3_ref)):
        x = x + s_ref[...] * gt[:, k:k + 1]
    y_ref[...] = _rms(x, g_ref[...])


def _final(x2, slots, gates, g, row0, rows, tm):
    n_tok, d = x2.shape
    rb0 = row0 // tm
    nblk = n_tok // tm
    slot = lambda k: pl.BlockSpec((tm, d), lambda i: (k * nblk + rb0 + i, 0))
    return pl.pallas_call(
        _final_kernel,
        grid=(rows // tm,),
        in_specs=[pl.BlockSpec((tm, d), lambda i: (rb0 + i, 0)),
                  slot(0), slot(1), slot(2), slot(3),
                  pl.BlockSpec((tm, TOP_K), lambda i: (rb0 + i, 0)),
                  pl.BlockSpec((1, d), lambda i: (0, 0))],
        out_specs=pl.BlockSpec((tm, d), lambda i: (i, 0)),
        out_shape=jax.ShapeDtypeStruct((rows, d), F32),
        compiler_params=_params("parallel"),
        name="final",
    )(x2, slots, slots, slots, slots, gates, g)


def _pad_tail(buf):
    return jnp.pad(buf, ((0, 0), (SUBLANES - (CONV_W - 1), 0), (0, 0)))


def kernel(x_prompt, x_sample, mem_prompt, state_lru_conv, state_lru_h, state_gdn_conv, state_gdn_S, cache_mem_k, cache_mem_v, norm_mix_g, w_in, lru_conv_w, lru_conv_b, lru_wx, lru_bx, lru_wa, lru_ba, lru_a_param, gdn_conv_w, gdn_A_log, gdn_dt_bias, gdn_norm_g, w_branch_a, w_branch_b, w_out, norm_xa_g, norm_mem_g, xa_wq, xa_wk, xa_wv, xa_wo, norm_moe_g, router_w, router_b, moe_w_gate, moe_b_gate, moe_w_up, moe_b_up, moe_w_down, moe_b_down, norm_final_g):
    depth = w_in.shape[0]
    assert depth == 1
    bp, tp, d = x_prompt.shape
    bs, ts, _ = x_sample.shape
    np_, ns = bp * tp, bs * ts
    n_tok = np_ + ns
    lw = lru_conv_w.shape[2]
    heads, dk, dv = state_gdn_S.shape[2:]
    kdim = heads * dk
    mem_len = mem_prompt.shape[1]
    xa_heads, xa_hd = cache_mem_k.shape[3:]
    xa_dim = xa_heads * xa_hd
    ne = router_w.shape[2]
    past_len = 16384
    gdn_chunk = 64

    row = lambda v: v.reshape(1, -1)
    xp = x_prompt.reshape(np_, d)
    xs = x_sample.reshape(ns, d)

    c_main = 2 * lw + 3 * kdim + heads * dv
    w_in_t = jnp.swapaxes(w_in[0], 0, 1)
    w_ba_t = w_in_t[c_main:c_main + 2 * heads].astype(BF16)
    w_gates = jnp.concatenate([lru_wx[0], lru_wa[0]], axis=-1).astype(BF16)

    rows_gcd = math.gcd(np_, ns)
    tm_s = _tile(rows_gcd, 256)
    tm_m = _tile(rows_gcd, 512)
    tm_l = _tile(rows_gcd, 1024)
    h1, ba = _norm_ba(xp, xs, row(norm_mix_g[0]), w_ba_t, tm_m)
    proj = _matmul(h1, w_in_t, c_main, tm_l, _tile(c_main, 1024), F32, "in_proj", w_is_t=True)

    lru_args = (lru_conv_w[0], row(lru_conv_b[0]), w_gates, row(lru_bx[0]), row(lru_ba[0]), row(lru_a_param[0]))
    ya_p, lru_h_p = _lru_seq(proj, 0, bp, tp, _tile(tp, 256), jnp.zeros((bp, SUBLANES, lw), F32),
                             jnp.zeros((bp, 1, lw), F32), *lru_args, pos0=0)
    if ts == SUBLANES:
        ya_s, lru_h_s = _lru_short(proj, np_, bs, math.gcd(bs, 32), _pad_tail(state_lru_conv[0]),
                                   state_lru_h[0][:, None, :], *lru_args)
    else:
        ya_s, lru_h_s = _lru_seq(proj, np_, bs, ts, ts, _pad_tail(state_lru_conv[0]), state_lru_h[0][:, None, :],
                                 *lru_args, pos0=past_len)

    gdn_args = (gdn_conv_w[0], row(gdn_A_log[0]), row(gdn_dt_bias[0]), row(gdn_norm_g[0]), heads, dk, dv)
    cp = min(gdn_chunk, tp)
    assert tp % cp == 0 and ts <= gdn_chunk
    gp = 4 if (tp // cp) % 4 == 0 else 1
    yb_p, gdn_s_p = _gdn(proj, 2 * lw, 0, bp, tp, cp, gp, cp, True, ba,
                         jnp.zeros((bp, SUBLANES, 3 * kdim), F32), jnp.zeros((bp, heads, dk, dv), F32), *gdn_args)
    gr_s = ts * math.gcd(bs, max(gdn_chunk // ts, 1))
    gs = 2 if (ns // gr_s) % 2 == 0 else 1
    yb_s, gdn_s_s = _gdn(proj, 2 * lw, np_, bs, ts, ts, gs, gr_s, False, ba,
                         _pad_tail(state_gdn_conv[0]), state_gdn_S[0], *gdn_args)

    g_row0 = c_main + 2 * heads
    w_gate_src = w_in_t
    if g_row0 % SUBLANES:
        w_gate_src, g_row0 = w_in_t[g_row0:], 0
    u = _merge(h1, ya_p, ya_s, yb_p, yb_s, w_gate_src, g_row0, w_branch_a[0].astype(BF16),
               w_branch_b[0].astype(BF16), tm_l, _tile(d, 256))
    x1, h2 = _out_norm(u, w_out[0].astype(BF16), xp, xs, row(norm_xa_g[0]), tm_m)

    k_p, v_p = _memkv(mem_prompt.reshape(bp * mem_len, d), row(norm_mem_g[0]), xa_wk[0].astype(BF16),
                      xa_wv[0].astype(BF16), _tile(bp * mem_len, 256))
    q = _matmul(h2, xa_wq[0], xa_dim, tm_l, xa_dim, BF16, "xa_q")
    o_p = _attn(q, 0, bp, tp, 1, _tile(tp, 512), k_p.reshape(bp, mem_len, xa_dim), v_p.reshape(bp, mem_len, xa_dim),
                xa_heads, xa_hd, False)
    nb_s = math.gcd(bs, 4)
    o_s = _attn(q, np_, bs, ts, nb_s, ts, cache_mem_k[0].reshape(bs, mem_len * xa_heads, xa_hd),
                cache_mem_v[0].reshape(bs, mem_len * xa_heads, xa_hd), xa_heads, xa_hd, True)
    x2, h3, gates, top_idx = _xo_router(o_p, o_s, xa_wo[0].astype(BF16), x1, row(norm_moe_g[0]), router_w[0],
                                row(router_b[0]), tm_m)

    cap, sub, big, tf = 1536, 128, 512, _tile(moe_w_gate.shape[3], 256)
    n_sb_max = (n_tok * TOP_K + ne * (cap - 1)) // cap
    tok, dst, sb_e, sb_r0, sb_n = _route(top_idx, ne, cap, n_sb_max)
    slots = _moe(h3, sb_e, sb_r0, sb_n, tok, dst, moe_w_gate[0], moe_w_up[0], moe_w_down[0],
                 moe_b_gate[0], moe_b_up[0], moe_b_down[0], cap, sub, big, tf)
    y_p =_final(x2, slots, gates, row(norm_final_g), 0, np_, tm_s)
    y_s = _final(x2, slots, gates, row(norm_final_g), np_, ns, tm_s)

    keep = CONV_W - 1
    pp = jnp.stack([lax.slice(proj, ((b + 1) * tp - keep, 0), ((b + 1) * tp, c_main)) for b in range(bp)])
    ps = proj[np_:].reshape(bs, ts, c_main)
    return (y_p.reshape(bp, tp, d), y_s.reshape(bs, ts, d),
            pp[:, :, :lw][None], lru_h_p.reshape(1, bp, lw),
            pp[:, :, 2 * lw:2 * lw + 3 * kdim][None], gdn_s_p[None],
            k_p.reshape(1, bp, mem_len, xa_heads, xa_hd), v_p.reshape(1, bp, mem_len, xa_heads, xa_hd),
            ps[:, ts - keep:, :lw][None], lru_h_s.reshape(1, bs, lw),
            ps[:, ts - keep:, 2 * lw:2 * lw + 3 * kdim][None], gdn_s_s[None])
```

```python
import functools
import math

import jax
import jax.numpy as jnp
from jax import lax
from jax.experimental import pallas as pl
from jax.experimental.pallas import tpu as pltpu

F32 = jnp.float32
BF16 = jnp.bfloat16
NORM_EPS = 1e-6
LRU_C = 8.0
CONV_W = 4
SWIGLU_ALPHA = 1.702
SWIGLU_LIMIT = 7.0
TOP_K = 4
TOP_K_SHIFT = 2
VMEM_LIMIT_BYTES = 56 * 1024 * 1024
SUBLANES = 8
LANES = 128


def _params(*sem):
    return pltpu.CompilerParams(dimension_semantics=sem, vmem_limit_bytes=VMEM_LIMIT_BYTES)


def _bdot(a, b):
    return jnp.dot(a.astype(BF16), b.astype(BF16), preferred_element_type=F32)


def _bdot_nt(a, b):
    return lax.dot_general(a.astype(BF16), b.astype(BF16), (((1,), (1,)), ((), ())), preferred_element_type=F32)


def _dot_nt(a, b):
    return lax.dot_general(a, b, (((1,), (1,)), ((), ())), preferred_element_type=F32)


def _bdot_tn(a, b):
    return lax.dot_general(a.astype(BF16), b.astype(BF16), (((0,), (0,)), ((), ())), preferred_element_type=F32)


def _split2(x):
    hi = x.astype(BF16)
    lo = (x - hi.astype(F32)).astype(BF16)
    return hi, lo


def _split3(x):
    p1 = x.astype(BF16)
    r1 = x - p1.astype(F32)
    p2 = r1.astype(BF16)
    p3 = (r1 - p2.astype(F32)).astype(BF16)
    return p1, p2, p3


def _dot3(a, b):
    ah, al = _split2(a)
    bh, bl = _split2(b)
    d = functools.partial(jnp.dot, preferred_element_type=F32)
    return d(ah, bh) + (d(ah, bl) + d(al, bh))


def _dot_exact_lhs(m_bf16, x):
    d = functools.partial(jnp.dot, preferred_element_type=F32)
    p1, p2, p3 = _split3(x)
    return d(m_bf16, p1) + (d(m_bf16, p2) + d(m_bf16, p3))


def _dot_exact_rhs(x, m_bf16):
    d = functools.partial(jnp.dot, preferred_element_type=F32)
    p1, p2, p3 = _split3(x)
    return d(p1, m_bf16) + (d(p2, m_bf16) + d(p3, m_bf16))


def _sigmoid(x):
    return jax.nn.sigmoid(x)


def _tile(n, pref):
    if n <= pref:
        return n
    t = pref - pref % SUBLANES
    while n % t:
        t -= SUBLANES
    return t


def _softplus(x):
    return jnp.maximum(x, 0.0) + jnp.log1p(jnp.exp(-jnp.abs(x)))


def _gelu_tanh(x):
    return 0.5 * x * (1.0 + jnp.tanh(0.7978845608028654 * (x + 0.044715 * (x * x * x))))


def _rms(x, g):
    r = lax.rsqrt(jnp.mean(x * x, axis=-1, keepdims=True) + NORM_EPS)
    return x * r * g


def _split_rows(a, b, tm):
    na = a.shape[0] // tm
    w = a.shape[1]
    return (pl.BlockSpec((tm, w), lambda i, *_: (jnp.minimum(i, na - 1), 0)),
            pl.BlockSpec((tm, w), lambda i, *_: (jnp.maximum(i - na, 0), 0)))


def _norm_ba_kernel(xa_ref, xb_ref, g_ref, wba_ref, h_ref, ba_ref, *, na):
    x = jnp.where(pl.program_id(0) < na, xa_ref[...], xb_ref[...])
    h = _rms(x, g_ref[...]).astype(BF16)
    h_ref[...] = h
    ba_ref[...] = _dot_nt(h, wba_ref[...])


def _norm_ba(xa, xb, g, w_ba_t, tm):
    d = xa.shape[1]
    m = xa.shape[0] + xb.shape[0]
    nb = w_ba_t.shape[0]
    return pl.pallas_call(
        functools.partial(_norm_ba_kernel, na=xa.shape[0] // tm),
        grid=(m // tm,),
        in_specs=[*_split_rows(xa, xb, tm),
                  pl.BlockSpec((1, d), lambda i: (0, 0)),
                  pl.BlockSpec((nb, d), lambda i: (0, 0))],
        out_specs=[pl.BlockSpec((tm, d), lambda i: (i, 0)),
                   pl.BlockSpec((tm, nb), lambda i: (i, 0))],
        out_shape=[jax.ShapeDtypeStruct((m, d), BF16), jax.ShapeDtypeStruct((m, nb), F32)],
        compiler_params=_params("parallel"),
        name="norm_ba",
    )(xa, xb, g, w_ba_t)


def _mm_kernel(a_ref, w_ref, o_ref, *, w_is_t):
    w = w_ref[...].astype(BF16)
    acc = _dot_nt(a_ref[...], w) if w_is_t else jnp.dot(a_ref[...], w, preferred_element_type=F32)
    o_ref[...] = acc.astype(o_ref.dtype)


def _matmul(a, w, n_cols, tm, tn, out_dtype, name, w_is_t=False):
    m, k = a.shape
    w_spec = pl.BlockSpec((tn, k), lambda j, i: (j, 0)) if w_is_t else pl.BlockSpec((k, tn), lambda j, i: (0, j))
    return pl.pallas_call(
        functools.partial(_mm_kernel, w_is_t=w_is_t),
        grid=(n_cols // tn, m // tm),
        in_specs=[pl.BlockSpec((tm, k), lambda j, i: (i, 0)), w_spec],
        out_specs=pl.BlockSpec((tm, tn), lambda j, i: (i, j)),
        out_shape=jax.ShapeDtypeStruct((m, n_cols), out_dtype),
        compiler_params=_params("parallel", "parallel"),
        name=name,
    )(a, w)


def _conv4_into(xe_ref, rows, cw):
    y = xe_ref[SUBLANES:SUBLANES + rows, :] * cw[CONV_W - 1:CONV_W]
    for s in range(1, CONV_W):
        y = y + xe_ref[SUBLANES - s:SUBLANES - s + rows, :] * cw[CONV_W - 1 - s:CONV_W - s]
    return y


def _scan8(a3, b3):
    row = lax.broadcasted_iota(jnp.int32, a3.shape, 1)
    for s in (1, 2, 4):
        a_sh = pltpu.roll(a3, s, axis=1)
        b_sh = pltpu.roll(b3, s, axis=1)
        m = row >= s
        b3 = jnp.where(m, a3 * b_sh + b3, b3)
        a3 = jnp.where(m, a3 * a_sh, a3)
    return a3, b3


def _lru_gates(xc, wg_ref, bx, bga, ap, first_row_is_pos0):
    nblk = wg_ref.shape[0]
    bw = wg_ref.shape[1]
    xcb = xc.astype(BF16)
    gx, ga = [], []
    for n in range(nblk):
        r = jnp.dot(xcb[:, n * bw:(n + 1) * bw], wg_ref[n], preferred_element_type=F32)
        gx.append(r[:, :bw])
        ga.append(r[:, bw:])
    gate_x = _sigmoid(jnp.concatenate(gx, axis=1) + bx)
    gate_a = _sigmoid(jnp.concatenate(ga, axis=1) + bga)
    log_a = (-LRU_C) * gate_a * _softplus(ap)
    a = jnp.exp(log_a)
    mult = jnp.sqrt(-jnp.tanh(log_a) * (a * a + 1.0))
    if first_row_is_pos0 is not None:
        mult = jnp.where(first_row_is_pos0, 1.0, mult)
    return a, xc * gate_x * mult


def _lru_seq_kernel(xa_ref, ya_ref, tail0_ref, h0_ref, cw_ref, cb_ref, wg_ref, bx_ref, bga_ref, ap_ref,
                    y_ref, hl_ref, xe_ref, a_ref, b_ref, h_ref, *, tt, pos0):
    t = pl.program_id(1)
    w = xa_ref.shape[1]

    @pl.when(t == 0)
    def _():
        xe_ref[0:SUBLANES, :] = tail0_ref[0]
        h_ref[...] = h0_ref[0]

    xe_ref[SUBLANES:SUBLANES + tt, :] = xa_ref[...]
    xc = _conv4_into(xe_ref, tt, cw_ref[...]) + cb_ref[...]
    xe_ref[0:SUBLANES, :] = xe_ref[tt:tt + SUBLANES, :]
    first = None
    if pos0 == 0:
        first = (lax.broadcasted_iota(jnp.int32, (tt, 1), 0) + t * tt) == 0
    a, b = _lru_gates(xc, wg_ref, bx_ref[...], bga_ref[...], ap_ref[...], first)
    a3, b3 = _scan8(a.reshape(tt // SUBLANES, SUBLANES, w), b.reshape(tt // SUBLANES, SUBLANES, w))
    a_ref[...] = a3
    b_ref[...] = b3

    def body(g, h):
        hg = a_ref[g] * h + b_ref[g]
        b_ref[g] = hg
        return hg[SUBLANES - 1:SUBLANES, :]

    h = lax.fori_loop(0, tt // SUBLANES, body, h_ref[...])
    h_ref[...] = h
    hl_ref[0] = h
    hs = b_ref[...].reshape(tt, w)
    y_ref[...] = (hs * _gelu_tanh(ya_ref[...])).astype(BF16)


def _lru_seq(proj, row0, batch, seq, tt, tail0, h0, cw, cb, wg, bx, bga, ap, pos0):
    w = cw.shape[1]
    nt = seq // tt
    rb0 = row0 // tt
    vec = lambda: pl.BlockSpec((1, w), lambda b, t: (0, 0))
    return pl.pallas_call(
        functools.partial(_lru_seq_kernel, tt=tt, pos0=pos0),
        grid=(batch, nt),
        in_specs=[pl.BlockSpec((tt, w), lambda b, t: (rb0 + b * nt + t, 0)),
                  pl.BlockSpec((tt, w), lambda b, t: (rb0 + b * nt + t, 1)),
                  pl.BlockSpec((1, SUBLANES, w), lambda b, t: (b, 0, 0)),
                  pl.BlockSpec((1, 1, w), lambda b, t: (b, 0, 0)),
                  pl.BlockSpec((CONV_W, w), lambda b, t: (0, 0)),
                  vec(),
                  pl.BlockSpec(wg.shape, lambda b, t: (0, 0, 0)),
                  vec(), vec(), vec()],
        out_specs=[pl.BlockSpec((tt, w), lambda b, t: (b * nt + t, 0)),
                   pl.BlockSpec((1, 1, w), lambda b, t: (b, 0, 0))],
        out_shape=[jax.ShapeDtypeStruct((batch * seq, w), BF16),
                   jax.ShapeDtypeStruct((batch, 1, w), F32)],
        scratch_shapes=[pltpu.VMEM((SUBLANES + tt, w), F32),
                        pltpu.VMEM((tt // SUBLANES, SUBLANES, w), F32),
                        pltpu.VMEM((tt // SUBLANES, SUBLANES, w), F32),
                        pltpu.VMEM((1, w), F32)],
        compiler_params=_params("parallel", "arbitrary"),
        name="lru_seq",
    )(proj, proj, tail0, h0, cw, cb, wg, bx, bga, ap)


def _lru_short_kernel(xa_ref, ya_ref, tail0_ref, h0_ref, cw_ref, cb_ref, wg_ref, bx_ref, bga_ref, ap_ref,
                      y_ref, hl_ref, xe_ref):
    nb = xe_ref.shape[0]
    w = xa_ref.shape[1]
    xe_ref[:, 0:SUBLANES, :] = tail0_ref[...]
    xe_ref[:, SUBLANES:, :] = xa_ref[...].reshape(nb, SUBLANES, w)
    cw = cw_ref[...]
    xc = xe_ref[:, SUBLANES:, :] * cw[CONV_W - 1:CONV_W]
    for s in range(1, CONV_W):
        xc = xc + xe_ref[:, SUBLANES - s:2 * SUBLANES - s, :] * cw[CONV_W - 1 - s:CONV_W - s]
    xc = xc.reshape(nb * SUBLANES, w) + cb_ref[...]
    a, b = _lru_gates(xc, wg_ref, bx_ref[...], bga_ref[...], ap_ref[...], None)
    a3, b3 = _scan8(a.reshape(nb, SUBLANES, w), b.reshape(nb, SUBLANES, w))
    hs = a3 * h0_ref[...] + b3
    hl_ref[...] = hs[:, SUBLANES - 1:SUBLANES, :]
    y_ref[...] = (hs.reshape(nb * SUBLANES, w) * _gelu_tanh(ya_ref[...])).astype(BF16)


def _lru_short(proj, row0, batch, nb, tail0, h0, cw, cb, wg, bx, bga, ap):
    w = cw.shape[1]
    rows = nb * SUBLANES
    rb0 = row0 // rows
    vec = lambda: pl.BlockSpec((1, w), lambda b: (0, 0))
    return pl.pallas_call(
        _lru_short_kernel,
        grid=(batch // nb,),
        in_specs=[pl.BlockSpec((rows, w), lambda b: (rb0 + b, 0)),
                  pl.BlockSpec((rows, w), lambda b: (rb0 + b, 1)),
                  pl.BlockSpec((nb, SUBLANES, w), lambda b: (b, 0, 0)),
                  pl.BlockSpec((nb, 1, w), lambda b: (b, 0, 0)),
                  pl.BlockSpec((CONV_W, w), lambda b: (0, 0)),
                  vec(),
                  pl.BlockSpec(wg.shape, lambda b: (0, 0, 0)),
                  vec(), vec(), vec()],
        out_specs=[pl.BlockSpec((rows, w), lambda b: (b, 0)),
                   pl.BlockSpec((nb, 1, w), lambda b: (b, 0, 0))],
        out_shape=[jax.ShapeDtypeStruct((batch * SUBLANES, w), BF16),
                   jax.ShapeDtypeStruct((batch, 1, w), F32)],
        scratch_shapes=[pltpu.VMEM((nb, 2 * SUBLANES, w), F32)],
        compiler_params=_params("parallel"),
        name="lru_short",
    )(proj, proj, tail0, h0, cw, cb, wg, bx, bga, ap)


def _gdn_kernel(q_ref, k_ref, v_ref, z_ref, ba_ref, bat_ref, tail0_ref, s0_ref, cw_ref, alog_ref, dtb_ref,
                alogt_ref, dtbt_ref, ng_ref, y_ref, sout_ref, xe_ref, s_ref, *, groups, gr, c, carry, heads, dk, dv):
    ci = pl.program_id(1)
    kdim = heads * dk
    rows = groups * gr
    nseg = gr // c
    nconv, crow = xe_ref.shape[0], xe_ref.shape[1] - SUBLANES
    hs = range(heads)
    units = [(gi, h) for gi in range(groups) for h in hs]

    @pl.when(ci == 0)
    def _():
        xe_ref[:, 0:SUBLANES, :] = tail0_ref[...]
        if carry:
            s_ref[...] = s0_ref[0]

    xe_ref[:, SUBLANES:, 0:kdim] = q_ref[...].reshape(nconv, crow, kdim)
    xe_ref[:, SUBLANES:, kdim:2 * kdim] = k_ref[...].reshape(nconv, crow, kdim)
    xe_ref[:, SUBLANES:, 2 * kdim:] = v_ref[...].reshape(nconv, crow, kdim)
    cw = cw_ref[...]
    pre = xe_ref[:, SUBLANES:, :] * cw[CONV_W - 1:CONV_W]
    for s in range(1, CONV_W):
        pre = pre + xe_ref[:, SUBLANES - s:SUBLANES - s + crow, :] * cw[CONV_W - 1 - s:CONV_W - s]
    xe_ref[:, 0:SUBLANES, :] = xe_ref[:, crow:crow + SUBLANES, :]
    pre = pre.reshape(rows, 3 * kdim)
    qkv = pre * _sigmoid(pre)

    ba = ba_ref[...]
    beta = _sigmoid(ba[:, 0:heads])
    g = -jnp.exp(alog_ref[...]) * _softplus(ba[:, heads:2 * heads] + dtb_ref[...])
    ii = lax.broadcasted_iota(jnp.int32, (gr, gr), 0)
    jj = lax.broadcasted_iota(jnp.int32, (gr, gr), 1)
    same = (ii // c) == (jj // c)
    incl = jnp.logical_and(same, ii >= jj)
    strict = jnp.logical_and(same, ii > jj)
    eye = (ii == jj).astype(F32)
    incl_b = incl.astype(BF16)
    inclt_b = jnp.logical_and(same, ii <= jj).astype(BF16)
    same_b = same.astype(BF16)
    z = z_ref[...]
    ng = ng_ref[...]
    scale = dk ** -0.5

    gcs, gcts, egcs, ekds, gtots = [], [], [], [], []
    for gi in range(groups):
        gg = g[gi * gr:(gi + 1) * gr]
        gt = -jnp.exp(alogt_ref[...]) * _softplus(bat_ref[0, 0, gi][heads:2 * heads, :] + dtbt_ref[...])
        gc = _dot_exact_lhs(incl_b, gg)
        gtot = _dot_exact_lhs(same_b, gg)
        gcs.append(gc)
        gcts.append(_dot_exact_rhs(gt, inclt_b))
        egcs.append(jnp.exp(gc))
        ekds.append(jnp.exp(gtot - gc))
        gtots.append(gtot)

    k_, kb_, dec_, rhs_, qs_ = {}, {}, {}, {}, {}
    for (gi, h) in units:
        r0 = gi * gr
        qh = qkv[r0:r0 + gr, h * dk:(h + 1) * dk]
        kh = qkv[r0:r0 + gr, kdim + h * dk:kdim + (h + 1) * dk]
        vh = qkv[r0:r0 + gr, 2 * kdim + h * dv:2 * kdim + (h + 1) * dv]
        qh = qh * lax.rsqrt(jnp.sum(qh * qh, axis=-1, keepdims=True) + NORM_EPS)
        kh = kh * lax.rsqrt(jnp.sum(kh * kh, axis=-1, keepdims=True) + NORM_EPS)
        bh = beta[r0:r0 + gr, h:h + 1]
        diff = gcs[gi][:, h:h + 1] - gcts[gi][h:h + 1, :]
        u = (gi, h)
        dec_[u] = jnp.where(incl, jnp.exp(jnp.where(incl, diff, 0.0)), 0.0)
        k_[u] = kh
        kb_[u] = kh * bh
        qs_[u] = qh * scale
        rhs_[u] = jnp.concatenate([vh * bh, kb_[u] * egcs[gi][:, h:h + 1]], axis=1)
    npow = {u: jnp.where(strict, -(_bdot_nt(kb_[u], k_[u]) * dec_[u]), 0.0) for u in units}
    qk_ = {u: _bdot_nt(qs_[u], k_[u]) * dec_[u] for u in units}
    p = {u: eye + npow[u] for u in units}
    lvl = 1
    while 2 * lvl < c:
        npow = {u: _dot3(npow[u], npow[u]) for u in units}
        p = {u: p[u] + _dot3(p[u], npow[u]) for u in units}
        lvl *= 2
    sol = {u: _dot3(p[u], rhs_[u]) for u in units}

    o_ = {}
    if carry:
        assert nseg == 1
        cur = {h: s_ref[h] for h in hs}
        for gi in range(groups):
            wq = {}
            for h in hs:
                u = (gi, h)
                lhs = jnp.concatenate([sol[u][:, dv:], qs_[u] * egcs[gi][:, h:h + 1]], axis=0)
                wq[h] = _bdot(lhs, cur[h])
            for h in hs:
                u = (gi, h)
                vnew = sol[u][:, :dv] - wq[h][:gr]
                o_[u] = wq[h][gr:] + _bdot(qk_[u], vnew)
                kd = k_[u] * ekds[gi][:, h:h + 1]
                glast = jnp.exp(gtots[gi][0:1, h:h + 1])
                cur[h] = cur[h] * glast + _bdot_tn(kd, vnew)
        for h in hs:
            s_ref[h] = cur[h]
        sout_ref[0] = s_ref[...]
    else:
        segs = [(gi, j, h) for gi in range(groups) for j in range(nseg) for h in hs]
        wq = {}
        for (gi, j, h) in segs:
            u = (gi, h)
            sl = slice(j * c, (j + 1) * c)
            lhs = jnp.concatenate([sol[u][sl, dv:], qs_[u][sl] * egcs[gi][sl, h:h + 1]], axis=0)
            wq[(gi, j, h)] = _bdot(lhs, s0_ref[gi * nseg + j, h])
        vn = {}
        for (gi, j, h) in segs:
            u = (gi, h)
            sl = slice(j * c, (j + 1) * c)
            vn[(gi, j, h)] = sol[u][sl, :dv] - wq[(gi, j, h)][:c]
            kd = k_[u][sl] * ekds[gi][sl, h:h + 1]
            glast = jnp.exp(gtots[gi][j * c:j * c + 1, h:h + 1])
            sout_ref[gi * nseg + j, h] = s0_ref[gi * nseg + j, h] * glast + _bdot_tn(kd, vn[(gi, j, h)])
        for u in units:
            gi, h = u
            vnew = jnp.concatenate([vn[(gi, j, h)] for j in range(nseg)], axis=0)
            os_ = jnp.concatenate([wq[(gi, j, h)][c:] for j in range(nseg)], axis=0)
            o_[u] = os_ + _bdot(qk_[u], vnew)

    ys = []
    for gi in range(groups):
        row = []
        for h in hs:
            zz = z[gi * gr:(gi + 1) * gr, h * dv:(h + 1) * dv]
            row.append(_rms(o_[(gi, h)], ng) * (zz * _sigmoid(zz)))
        ys.append(jnp.concatenate(row, axis=1))
    y_ref[...] = jnp.concatenate(ys, axis=0).astype(BF16)


def _gdn(proj, col0, row0, batch, seq, c, groups, gr, carry, ba, tail0, s0, cw, alog, dtb, ng, heads, dk, dv):
    kdim = heads * dk
    rows = groups * gr
    if carry:
        nb, nsteps, sb = batch, seq // rows, 1
    else:
        sb = rows // seq
        nb, nsteps = batch // sb, 1
    crow = rows // sb
    rb0 = row0 // rows
    cb0 = col0 // kdim
    rblk = lambda b, ci: rb0 + b * nsteps + ci
    small = lambda shape: pl.BlockSpec(shape, lambda b, ci: (0,) * len(shape))
    bat = ba[row0:row0 + batch * seq].reshape(nb, nsteps, groups, gr, 2 * heads)
    bat = jnp.swapaxes(bat, 3, 4)
    return pl.pallas_call(
        functools.partial(_gdn_kernel, groups=groups, gr=gr, c=c, carry=carry, heads=heads, dk=dk, dv=dv),
        grid=(nb, nsteps),
        in_specs=[pl.BlockSpec((rows, kdim), lambda b, ci: (rblk(b, ci), cb0)),
                  pl.BlockSpec((rows, kdim), lambda b, ci: (rblk(b, ci), cb0 + 1)),
                  pl.BlockSpec((rows, kdim), lambda b, ci: (rblk(b, ci), cb0 + 2)),
                  pl.BlockSpec((rows, kdim), lambda b, ci: (rblk(b, ci), cb0 + 3)),
                  pl.BlockSpec((rows, 2 * heads), lambda b, ci: (rblk(b, ci), 0)),
                  pl.BlockSpec((1, 1, groups, 2 * heads, gr), lambda b, ci: (b, ci, 0, 0, 0)),
                  pl.BlockSpec((sb, SUBLANES, 3 * kdim), lambda b, ci: (b, 0, 0)),
                  pl.BlockSpec((sb, heads, dk, dv), lambda b, ci: (b, 0, 0, 0)),
                  small((CONV_W, 3 * kdim)),
                  small((1, heads)), small((1, heads)), small((heads, 1)), small((heads, 1)),
                  small((1, dv))],
        out_specs=[pl.BlockSpec((rows, heads * dv), lambda b, ci: (b * nsteps + ci, 0)),
                   pl.BlockSpec((sb, heads, dk, dv), lambda b, ci: (b, 0, 0, 0))],
        out_shape=[jax.ShapeDtypeStruct((batch * seq, heads * dv), BF16),
                   jax.ShapeDtypeStruct((batch, heads, dk, dv), F32)],
        scratch_shapes=[pltpu.VMEM((sb, SUBLANES + crow, 3 * kdim), F32),
                        pltpu.VMEM((heads, dk, dv), F32)],
        compiler_params=_params("parallel", "arbitrary"),
        name="gdn",
    )(proj, proj, proj, proj, ba, bat, tail0, s0, cw, alog, dtb, alog.T, dtb.T, ng)


def _merge_kernel(h_ref, ya1_ref, ya2_ref, yb1_ref, yb2_ref, wga_ref, wgb_ref, wa_ref, wb_ref, u_ref, *, na):
    d = functools.partial(jnp.dot, preferred_element_type=F32)
    first = pl.program_id(0) < na
    h = h_ref[...]
    ya = jnp.where(first, ya1_ref[...], ya2_ref[...]).astype(BF16)
    yb = jnp.where(first, yb1_ref[...], yb2_ref[...]).astype(BF16)
    u = _sigmoid(_dot_nt(h, wga_ref[...].astype(BF16))) * d(ya, wa_ref[...])
    u = u + _sigmoid(_dot_nt(h, wgb_ref[...].astype(BF16))) * d(yb, wb_ref[...])
    u_ref[...] = u.astype(BF16)


def _merge(h, ya1, ya2, yb1, yb2, w_t, g_row0, w_a, w_b, tm, tn):
    m, d = h.shape
    n = w_a.shape[1]
    nj = n // tn
    assert g_row0 % SUBLANES == 0
    gate = lambda off: pl.BlockSpec((pl.Element(tn), pl.Element(d)),
                                    lambda i, j: (pl.multiple_of(g_row0 + off + j * tn, SUBLANES), 0))
    return pl.pallas_call(
        functools.partial(_merge_kernel, na=ya1.shape[0] // tm),
        grid=(m // tm, nj),
        in_specs=[pl.BlockSpec((tm, d), lambda i, j: (i, 0)),
                  *_split_rows(ya1, ya2, tm),
                  *_split_rows(yb1, yb2, tm),
                  gate(0),
                  gate(n),
                  pl.BlockSpec((w_a.shape[0], tn), lambda i, j: (0, j)),
                  pl.BlockSpec((w_b.shape[0], tn), lambda i, j: (0, j))],
        out_specs=pl.BlockSpec((tm, tn), lambda i, j: (i, j)),
        out_shape=jax.ShapeDtypeStruct((m, n), BF16),
        compiler_params=_params("parallel", "parallel"),
        name="merge",
    )(h, ya1, ya2, yb1, yb2, w_t, w_t, w_a, w_b)


def _out_norm_kernel(u_ref, w_ref, xa_ref, xb_ref, g_ref, x1_ref, h_ref, *, na):
    x = jnp.where(pl.program_id(0) < na, xa_ref[...], xb_ref[...])
    x1 = x + jnp.dot(u_ref[...], w_ref[...], preferred_element_type=F32)
    x1_ref[...] = x1
    h_ref[...] = _rms(x1, g_ref[...]).astype(BF16)


def _out_norm(u, w, xa, xb, g, tm):
    m, k = u.shape
    d = xa.shape[1]
    return pl.pallas_call(
        functools.partial(_out_norm_kernel, na=xa.shape[0] // tm),
        grid=(m // tm,),
        in_specs=[pl.BlockSpec((tm, k), lambda i: (i, 0)),
                  pl.BlockSpec((k, d), lambda i: (0, 0)),
                  *_split_rows(xa, xb, tm),
                  pl.BlockSpec((1, d), lambda i: (0, 0))],
        out_specs=[pl.BlockSpec((tm, d), lambda i: (i, 0)),
                   pl.BlockSpec((tm, d), lambda i: (i, 0))],
        out_shape=[jax.ShapeDtypeStruct((m, d), F32), jax.ShapeDtypeStruct((m, d), BF16)],
        compiler_params=_params("parallel"),
        name="out_norm",
    )(u, w, xa, xb, g)


def _memkv_kernel(m_ref, g_ref, wk_ref, wv_ref, k_ref, v_ref):
    mn = _rms(m_ref[...], g_ref[...]).astype(BF16)
    k_ref[...] = jnp.dot(mn, wk_ref[...], preferred_element_type=F32)
    v_ref[...] = jnp.dot(mn, wv_ref[...], preferred_element_type=F32)


def _memkv(mem, g, wk, wv, tm):
    m, d = mem.shape
    n = wk.shape[1]
    return pl.pallas_call(
        _memkv_kernel,
        grid=(m // tm,),
        in_specs=[pl.BlockSpec((tm, d), lambda i: (i, 0)),
                  pl.BlockSpec((1, d), lambda i: (0, 0)),
                  pl.BlockSpec((d, n), lambda i: (0, 0)),
                  pl.BlockSpec((d, n), lambda i: (0, 0))],
        out_specs=[pl.BlockSpec((tm, n), lambda i: (i, 0)), pl.BlockSpec((tm, n), lambda i: (i, 0))],
        out_shape=[jax.ShapeDtypeStruct((m, n), F32), jax.ShapeDtypeStruct((m, n), F32)],
        compiler_params=_params("parallel"),
        name="memkv",
    )(mem, g, wk, wv)


def _attn_kernel(q_ref, k_ref, v_ref, o_ref, *, nb, tq, heads, hd, interleaved):
    mem = k_ref.shape[1] // heads if interleaved else k_ref.shape[1]
    scale = hd ** -0.5
    units = [(b, h) for b in range(nb) for h in range(heads)]

    def head(ref, b, h):
        if interleaved:
            return ref[b, pl.ds(h, mem, stride=heads), :].astype(BF16)
        return ref[b, :, h * hd:(h + 1) * hd].astype(BF16)

    q = q_ref[...].astype(BF16)
    s = {(b, h): lax.dot_general(q[b * tq:(b + 1) * tq, h * hd:(h + 1) * hd], head(k_ref, b, h),
                                 (((1,), (1,)), ((), ())), preferred_element_type=F32) * scale for (b, h) in units}
    p = {u: jnp.exp(s[u] - jnp.max(s[u], axis=-1, keepdims=True)) for u in units}
    o = {(b, h): jnp.dot(p[(b, h)].astype(BF16), head(v_ref, b, h), preferred_element_type=F32)
         / jnp.sum(p[(b, h)], axis=-1, keepdims=True) for (b, h) in units}
    o_ref[...] = jnp.concatenate([jnp.concatenate([o[(b, h)] for h in range(heads)], axis=1) for b in range(nb)],
                                 axis=0).astype(BF16)


def _attn(q, row0, batch, seq, nb, tq, k, v, heads, hd, interleaved):
    nt = seq // tq
    rows = nb * tq
    rb0 = row0 // rows
    n = heads * hd
    return pl.pallas_call(
        functools.partial(_attn_kernel, nb=nb, tq=tq, heads=heads, hd=hd, interleaved=interleaved),
        grid=(batch // nb, nt),
        in_specs=[pl.BlockSpec((rows, n), lambda b, t: (rb0 + b * nt + t, 0)),
                  pl.BlockSpec((nb,) + k.shape[1:], lambda b, t: (b, 0, 0)),
                  pl.BlockSpec((nb,) + v.shape[1:], lambda b, t: (b, 0, 0))],
        out_specs=pl.BlockSpec((rows, n), lambda b, t: (b * nt + t, 0)),
        out_shape=jax.ShapeDtypeStruct((batch * seq, n), BF16),
        compiler_params=_params("parallel", "parallel"),
        name="attn",
    )(q, k, v)


def _xo_router_kernel(oa_ref, ob_ref, wo_ref, x1_ref, g_ref, rw_ref, rb_ref, x2_ref, h_ref, gate_ref, idx_ref,
                      *, na):
    o = jnp.where(pl.program_id(0) < na, oa_ref[...], ob_ref[...]).astype(BF16)
    x2 = x1_ref[...] + jnp.dot(o, wo_ref[...], preferred_element_type=F32)
    x2_ref[...] = x2
    h = _rms(x2, g_ref[...])
    h_ref[...] = h
    logits = _dot3(h, rw_ref[...]) + rb_ref[...]
    ne = logits.shape[1]
    lane = lax.broadcasted_iota(jnp.int32, logits.shape, 1)
    vals, ids = [], []
    for _ in range(TOP_K):
        m = jnp.max(logits, axis=-1, keepdims=True)
        i = jnp.min(jnp.where(logits == m, lane, ne), axis=-1, keepdims=True)
        vals.append(m)
        ids.append(i)
        logits = jnp.where(lane == i, -jnp.inf, logits)
    e = jnp.exp(jnp.concatenate(vals, axis=1) - vals[0])
    gate_ref[...] = e / jnp.sum(e, axis=-1, keepdims=True)
    idx_ref[...] = jnp.concatenate(ids, axis=1)


def _xo_router(oa, ob, wo, x1, g, rw, rb, tm):
    m, d = x1.shape
    row = lambda w: pl.BlockSpec((tm, w), lambda i: (i, 0))
    full = lambda a: pl.BlockSpec(a.shape, lambda i: (0, 0))
    return pl.pallas_call(
        functools.partial(_xo_router_kernel, na=oa.shape[0] // tm),
        grid=(m // tm,),
        in_specs=[*_split_rows(oa, ob, tm), full(wo), row(d), full(g), full(rw), full(rb)],
        out_specs=[row(d), row(d), row(TOP_K), row(TOP_K)],
        out_shape=[jax.ShapeDtypeStruct((m, d), F32), jax.ShapeDtypeStruct((m, d), F32),
                   jax.ShapeDtypeStruct((m, TOP_K), F32), jax.ShapeDtypeStruct((m, TOP_K), jnp.int32)],
        compiler_params=_params("parallel"),
        name="xo_router",
    )(oa, ob, wo, x1, g, rw, rb)


def _moe_kernel(e_ref, r0_ref, n_ref, tok_ref, dst_ref, h_hbm, wg_ref, wu_ref, wd_ref, bg_ref, bu_ref, bd_ref,
                slots_hbm, xbuf, acc, gsem, ssem, *, sub, big, nf):
    s = pl.program_id(0)
    f = pl.program_id(1)
    n = n_ref[s]
    r0 = r0_ref[s]
    d = acc.shape[2]
    s_next = jnp.minimum(s + 1, pl.num_programs(0) - 1)
    n_next = jnp.where(s + 1 < pl.num_programs(0), n_ref[s_next], 0)
    r0_next = r0_ref[s_next]

    @pl.when(jnp.logical_and(s == 0, f == 0))
    def _():
        xbuf[...] = jnp.zeros_like(xbuf)

    def row_in(base, c, k):
        tok = tok_ref[base + c * SUBLANES + k]
        return pltpu.make_async_copy(h_hbm.at[pl.ds(tok, 1), :], xbuf.at[c, pl.ds(k, 1), :], gsem)

    def row_out(c, k):
        dst = dst_ref[r0 + c * SUBLANES + k]
        return pltpu.make_async_copy(acc.at[c, pl.ds(k, 1), :], slots_hbm.at[pl.ds(dst, 1), :], ssem)

    def for_rows(count, fn):
        nfull = count // SUBLANES

        def group(c, carry):
            for k in range(SUBLANES):
                fn(c, k)
            return carry

        lax.fori_loop(0, nfull, group, 0)
        lax.fori_loop(0, count - nfull * SUBLANES, lambda k, carry: (fn(nfull, k), carry)[1], 0)

    @pl.when(n > 0)
    def _active():
        @pl.when(f == 0)
        def _gather():
            @pl.when(s == 0)
            def _first():
                for_rows(n, lambda c, k: row_in(r0, c, k).start())

            for_rows(n, lambda c, k: row_in(r0, c, k).wait())

        bg = bg_ref[0]
        bu = bu_ref[0]
        bd = bd_ref[0]

        nsub = (n + sub - 1) // sub
        sub8 = sub // SUBLANES
        big8 = big // SUBLANES

        @pl.when(f == 0)
        def _init():
            def fill(j, carry):
                acc[pl.ds(pl.multiple_of(j * sub8, sub8), sub8)] = jnp.broadcast_to(bd, (sub8, SUBLANES, d))
                return carry

            lax.fori_loop(0, nsub, fill, 0)

        def block(off8, m):
            m8 = m // SUBLANES
            x = xbuf[pl.ds(off8, m8)].reshape(m, d).astype(BF16)
            gate = jnp.dot(x, wg_ref[0].astype(BF16), preferred_element_type=F32) + bg
            up = jnp.dot(x, wu_ref[0].astype(BF16), preferred_element_type=F32) + bu
            gate = jnp.minimum(gate, SWIGLU_LIMIT)
            up = jnp.clip(up, -SWIGLU_LIMIT, SWIGLU_LIMIT)
            act = gate * _sigmoid(SWIGLU_ALPHA * gate) * (up + 1.0)
            part = jnp.dot(act.astype(BF16), wd_ref[0].astype(BF16), preferred_element_type=F32)
            acc[pl.ds(off8, m8)] += part.reshape(m8, SUBLANES, d)

        per_big = big // sub
        nbig = nsub // per_big
        tail = nsub - nbig * per_big

        def big_block(j, carry):
            block(pl.multiple_of(j * big8, big8), big)
            return carry

        lax.fori_loop(0, nbig, big_block, 0)
        for t in range(1, per_big):
            @pl.when(tail == t)
            def _tail(t=t):
                block(pl.multiple_of(nbig * big8, sub8), t * sub)

        @pl.when(f == nf - 1)
        def _scatter():
            for_rows(n, lambda c, k: row_out(c, k).start())
            for_rows(n_next, lambda c, k: row_in(r0_next, c, k).start())
            for_rows(n, lambda c, k: row_out(c, k).wait())


def _moe(h, sb_e, sb_r0, sb_n, tok, dst, wg, wu, wd, bg, bu, bd, cap, sub, big, tf):
    n_tok, d = h.shape
    ne, _, dff = wg.shape
    nf = dff // tf
    g = sb_e.shape[0]
    fi = lambda s, f, n_ref: jnp.where(n_ref[s] > 0, f, nf - 1)
    grid_spec = pltpu.PrefetchScalarGridSpec(
        num_scalar_prefetch=5,
        grid=(g, nf),
        in_specs=[pl.BlockSpec(memory_space=pl.ANY),
                  pl.BlockSpec((1, d, tf), lambda s, f, e, r, n, t, o: (e[s], 0, fi(s, f, n))),
                  pl.BlockSpec((1, d, tf), lambda s, f, e, r, n, t, o: (e[s], 0, fi(s, f, n))),
                  pl.BlockSpec((1, tf, d), lambda s, f, e, r, n, t, o: (e[s], fi(s, f, n), 0)),
                  pl.BlockSpec((1, 1, tf), lambda s, f, e, r, n, t, o: (e[s], 0, fi(s, f, n))),
                  pl.BlockSpec((1, 1, tf), lambda s, f, e, r, n, t, o: (e[s], 0, fi(s, f, n))),
                  pl.BlockSpec((1, 1, d), lambda s, f, e, r, n, t, o: (e[s], 0, 0))],
        out_specs=pl.BlockSpec(memory_space=pl.ANY),
        scratch_shapes=[pltpu.VMEM((cap // SUBLANES, SUBLANES, d), F32),
                        pltpu.VMEM((cap // SUBLANES, SUBLANES, d), F32),
                        pltpu.SemaphoreType.DMA(()), pltpu.SemaphoreType.DMA(())],
    )
    return pl.pallas_call(
        functools.partial(_moe_kernel, sub=sub, big=big, nf=nf),
        grid_spec=grid_spec,
        out_shape=jax.ShapeDtypeStruct((n_tok * TOP_K, d), F32),
        compiler_params=_params("arbitrary", "arbitrary"),
        name="moe",
    )(sb_e, sb_r0, sb_n, tok, dst, h, wg, wu, wd, bg.reshape(ne, 1, dff), bu.reshape(ne, 1, dff),
      bd.reshape(ne, 1, d))


def _route(top_idx, ne, cap, n_sb_max):
    n_tok = top_idx.shape[0]
    e_flat = top_idx.reshape(-1)
    codes = jnp.argsort(e_flat, stable=True).astype(jnp.int32)
    counts = jnp.zeros((ne,), jnp.int32).at[e_flat].add(1)
    start = jnp.cumsum(counts) - counts
    n_sb = (counts + cap - 1) // cap
    sb_end = jnp.cumsum(n_sb)
    sb_start = sb_end - n_sb
    total = sb_end[-1]
    s = jnp.arange(n_sb_max, dtype=jnp.int32)
    e_s = jnp.minimum(jnp.searchsorted(sb_end, s, side='right'), ne - 1).astype(jnp.int32)
    e_last = e_s[jnp.maximum(total - 1, 0)]
    active = s < total
    e_s = jnp.where(active, e_s, e_last)
    within = s - sb_start[e_s]
    r0 = jnp.where(active, start[e_s] + within * cap, 0).astype(jnp.int32)
    n = jnp.where(active, jnp.clip(counts[e_s] - within * cap, 0, cap), 0).astype(jnp.int32)
    tok = lax.shift_right_logical(codes, TOP_K_SHIFT)
    dst = (codes & (TOP_K - 1)) * n_tok + tok
    return tok, dst, e_s, r0, n


def _final_kernel(x2_ref, s0_ref, s1_ref, s2_ref, s3_ref, gates_ref, g_ref, y_ref):
    x = x2_ref[...]
    gt = gates_ref[...]
    for k, s_ref in enumerate((s0_ref, s1_ref, s2_ref, s3_ref)):
        x = x + s_ref[...] * gt[:, k:k + 1]
    y_ref[...] = _rms(x, g_ref[...])


def _final(x2, slots, gates, g, row0, rows, tm):
    n_tok, d = x2.shape
    rb0 = row0 // tm
    nblk = n_tok // tm
    slot = lambda k: pl.BlockSpec((tm, d), lambda i: (k * nblk + rb0 + i, 0))
    return pl.pallas_call(
        _final_kernel,
        grid=(rows // tm,),
        in_specs=[pl.BlockSpec((tm, d), lambda i: (rb0 + i, 0)),
                  slot(0), slot(1), slot(2), slot(3),
                  pl.BlockSpec((tm, TOP_K), lambda i: (rb0 + i, 0)),
                  pl.BlockSpec((1, d), lambda i: (0, 0))],
        out_specs=pl.BlockSpec((tm, d), lambda i: (i, 0)),
        out_shape=jax.ShapeDtypeStruct((rows, d), F32),
        compiler_params=_params("parallel"),
        name="final",
    )(x2, slots, slots, slots, slots, gates, g)


def _pad_tail(buf):
    return jnp.pad(buf, ((0, 0), (SUBLANES - (CONV_W - 1), 0), (0, 0)))


def kernel(x_prompt, x_sample, mem_prompt, state_lru_conv, state_lru_h, state_gdn_conv, state_gdn_S, cache_mem_k, cache_mem_v, norm_mix_g, w_in, lru_conv_w, lru_conv_b, lru_wx, lru_bx, lru_wa, lru_ba, lru_a_param, gdn_conv_w, gdn_A_log, gdn_dt_bias, gdn_norm_g, w_branch_a, w_branch_b, w_out, norm_xa_g, norm_mem_g, xa_wq, xa_wk, xa_wv, xa_wo, norm_moe_g, router_w, router_b, moe_w_gate, moe_b_gate, moe_w_up, moe_b_up, moe_w_down, moe_b_down, norm_final_g):
    depth = w_in.shape[0]
    assert depth == 1
    bp, tp, d = x_prompt.shape
    bs, ts, _ = x_sample.shape
    np_, ns = bp * tp, bs * ts
    n_tok = np_ + ns
    lw = lru_conv_w.shape[2]
    heads, dk, dv = state_gdn_S.shape[2:]
    kdim = heads * dk
    mem_len = mem_prompt.shape[1]
    xa_heads, xa_hd = cache_mem_k.shape[3:]
    xa_dim = xa_heads * xa_hd
    ne = router_w.shape[2]
    past_len = 16384
    gdn_chunk = 64

    row = lambda v: v.reshape(1, -1)
    xp = x_prompt.reshape(np_, d)
    xs = x_sample.reshape(ns, d)

    c_main = 2 * lw + 3 * kdim + heads * dv
    w_in_t = jnp.swapaxes(w_in[0], 0, 1)
    w_ba_t = w_in_t[c_main:c_main + 2 * heads].astype(BF16)
    w_gates = jnp.concatenate([lru_wx[0], lru_wa[0]], axis=-1).astype(BF16)

    rows_gcd = math.gcd(np_, ns)
    tm_s = _tile(rows_gcd, 256)
    tm_m = _tile(rows_gcd, 512)
    tm_l = _tile(rows_gcd, 1024)
    h1, ba = _norm_ba(xp, xs, row(norm_mix_g[0]), w_ba_t, tm_m)
    proj = _matmul(h1, w_in_t, c_main, tm_l, _tile(c_main, 1024), F32, "in_proj", w_is_t=True)

    lru_args = (lru_conv_w[0], row(lru_conv_b[0]), w_gates, row(lru_bx[0]), row(lru_ba[0]), row(lru_a_param[0]))
    ya_p, lru_h_p = _lru_seq(proj, 0, bp, tp, _tile(tp, 256), jnp.zeros((bp, SUBLANES, lw), F32),
                             jnp.zeros((bp, 1, lw), F32), *lru_args, pos0=0)
    if ts == SUBLANES:
        ya_s, lru_h_s = _lru_short(proj, np_, bs, math.gcd(bs, 32), _pad_tail(state_lru_conv[0]),
                                   state_lru_h[0][:, None, :], *lru_args)
    else:
        ya_s, lru_h_s = _lru_seq(proj, np_, bs, ts, ts, _pad_tail(state_lru_conv[0]), state_lru_h[0][:, None, :],
                                 *lru_args, pos0=past_len)

    gdn_args = (gdn_conv_w[0], row(gdn_A_log[0]), row(gdn_dt_bias[0]), row(gdn_norm_g[0]), heads, dk, dv)
    cp = min(gdn_chunk, tp)
    assert tp % cp == 0 and ts <= gdn_chunk
    gp = 4 if (tp // cp) % 4 == 0 else 1
    yb_p, gdn_s_p = _gdn(proj, 2 * lw, 0, bp, tp, cp, gp, cp, True, ba,
                         jnp.zeros((bp, SUBLANES, 3 * kdim), F32), jnp.zeros((bp, heads, dk, dv), F32), *gdn_args)
    gr_s = ts * math.gcd(bs, max(gdn_chunk // ts, 1))
    gs = 2 if (ns // gr_s) % 2 == 0 else 1
    yb_s, gdn_s_s = _gdn(proj, 2 * lw, np_, bs, ts, ts, gs, gr_s, False, ba,
                         _pad_tail(state_gdn_conv[0]), state_gdn_S[0], *gdn_args)

    g_row0 = c_main + 2 * heads
    w_gate_src = w_in_t
    if g_row0 % SUBLANES:
        w_gate_src, g_row0 = w_in_t[g_row0:], 0
    u = _merge(h1, ya_p, ya_s, yb_p, yb_s, w_gate_src, g_row0, w_branch_a[0].astype(BF16),
               w_branch_b[0].astype(BF16), tm_l, _tile(d, 256))
    x1, h2 = _out_norm(u, w_out[0].astype(BF16), xp, xs, row(norm_xa_g[0]), tm_m)

    k_p, v_p = _memkv(mem_prompt.reshape(bp * mem_len, d), row(norm_mem_g[0]), xa_wk[0].astype(BF16),
                      xa_wv[0].astype(BF16), _tile(bp * mem_len, 256))
    q = _matmul(h2, xa_wq[0], xa_dim, tm_l, xa_dim, BF16, "xa_q")
    o_p = _attn(q, 0, bp, tp, 1, _tile(tp, 512), k_p.reshape(bp, mem_len, xa_dim), v_p.reshape(bp, mem_len, xa_dim),
                xa_heads, xa_hd, False)
    nb_s = math.gcd(bs, 4)
    o_s = _attn(q, np_, bs, ts, nb_s, ts, cache_mem_k[0].reshape(bs, mem_len * xa_heads, xa_hd),
                cache_mem_v[0].reshape(bs, mem_len * xa_heads, xa_hd), xa_heads, xa_hd, True)
    x2, h3, gates, top_idx = _xo_router(o_p, o_s, xa_wo[0].astype(BF16), x1, row(norm_moe_g[0]), router_w[0],
                                row(router_b[0]), tm_m)

    cap, sub, big, tf = 1536, 128, 768, _tile(moe_w_gate.shape[3], 256)
    n_sb_max = (n_tok * TOP_K + ne * (cap - 1)) // cap
    tok, dst, sb_e, sb_r0, sb_n = _route(top_idx, ne, cap, n_sb_max)
    slots = _moe(h3, sb_e, sb_r0, sb_n, tok, dst, moe_w_gate[0], moe_w_up[0], moe_w_down[0],
                 moe_b_gate[0], moe_b_up[0], moe_b_down[0], cap, sub, big, tf)
    y_p =_final(x2, slots, gates, row(norm_final_g), 0, np_, tm_s)
    y_s = _final(x2, slots, gates, row(norm_final_g), np_, ns, tm_s)

    keep = CONV_W - 1
    pp = jnp.stack([lax.slice(proj, ((b + 1) * tp - keep, 0), ((b + 1) * tp, c_main)) for b in range(bp)])
    ps = proj[np_:].reshape(bs, ts, c_main)
    return (y_p.reshape(bp, tp, d), y_s.reshape(bs, ts, d),
            pp[:, :, :lw][None], lru_h_p.reshape(1, bp, lw),
            pp[:, :, 2 * lw:2 * lw + 3 * kdim][None], gdn_s_p[None],
            k_p.reshape(1, bp, mem_len, xa_heads, xa_hd), v_p.reshape(1, bp, mem_len, xa_heads, xa_hd),
            ps[:, ts - keep:, :lw][None], lru_h_s.reshape(1, bs, lw),
            ps[:, ts - keep:, 2 * lw:2 * lw + 3 * kdim][None], gdn_s_s[None])
```
